```python
import math
import jax, jax.numpy as jnp
from jax import lax
import numpy as np

D_MODEL = 1024
BATCH = 8
SEQ = 2048
DEPTH = 1

N_META = 16
DA_HEADS = 4
DA_HEAD_DIM = 64
DA_V_DIM = 2 * DA_HEAD_DIM
DA_QK_WIDTH = DA_HEADS * 2 * DA_HEAD_DIM
DA_V_WIDTH = DA_HEADS * DA_V_DIM
ROPE_THETA = 10000.0
Q_BLOCK = 128
CONV_CH = D_MODEL // 2
CONV_WIDTH = 31
N_EXPERTS = 256
TOP_K = 8
N_GROUPS = 8
TOPK_GROUPS = 4
EXPERT_HIDDEN = D_MODEL // 4
SHARED_HIDDEN = D_MODEL // 4
ROUTED_SCALE = 2.5
MOE_BLOCK = 128
LN_EPS = 1e-5
DEEPNORM_ALPHA = (2.0 * DEPTH) ** 0.25
DEEPNORM_BETA = (8.0 * DEPTH) ** -0.25
SPLIT_POINTS = (DA_QK_WIDTH,
                2 * DA_QK_WIDTH,
                2 * DA_QK_WIDTH + DA_V_WIDTH,
                2 * DA_QK_WIDTH + DA_V_WIDTH + CONV_CH,
                2 * DA_QK_WIDTH + DA_V_WIDTH + 2 * CONV_CH,
                2 * DA_QK_WIDTH + DA_V_WIDTH + 2 * CONV_CH + D_MODEL)
IN_WIDTH = 2 * DA_QK_WIDTH + DA_V_WIDTH + 2 * CONV_CH + 2 * D_MODEL

kernel_name = 'hybrid_diffattn_conformer_moe_deepnorm'


def layer_norm(x, g, b):
    xf = x.astype(jnp.float32)
    mu = jnp.mean(xf, axis=-1, keepdims=True)
    var = jnp.mean(jnp.square(xf - mu), axis=-1, keepdims=True)
    y = (xf - mu) * lax.rsqrt(var + LN_EPS) * g.astype(jnp.float32) + b.astype(jnp.float32)
    return y.astype(x.dtype)


def rms_norm(x, g):
    xf = x.astype(jnp.float32)
    y = xf * lax.rsqrt(jnp.mean(jnp.square(xf), axis=-1, keepdims=True) + LN_EPS) * g.astype(jnp.float32)
    return y.astype(x.dtype)


def rope_tables(n, dim):
    inv = 1.0 / (ROPE_THETA ** (jnp.arange(0, dim, 2, dtype=jnp.float32) / dim))
    ang = jnp.arange(n, dtype=jnp.float32)[:, None] * inv[None, :]
    return jnp.cos(ang), jnp.sin(ang)


def apply_rope(x, cos, sin):
    x1, x2 = jnp.split(x, 2, axis=-1)
    c = cos.astype(x.dtype)
    s = sin.astype(x.dtype)
    return jnp.concatenate([x1 * c - x2 * s, x2 * c + x1 * s], axis=-1)


def diff_attention(q, k, v, lam, lam_init, subln_g):
    bsz, seq_len = q.shape[0], q.shape[1]
    n_blk = -(-seq_len // Q_BLOCK)
    lp = n_blk * Q_BLOCK
    pad = lp - seq_len
    q = jnp.pad(q, ((0, 0), (0, pad), (0, 0), (0, 0), (0, 0)))
    k = jnp.pad(k, ((0, 0), (0, pad), (0, 0), (0, 0), (0, 0)))
    v = jnp.pad(v, ((0, 0), (0, pad), (0, 0), (0, 0)))
    q = jnp.transpose(q, (0, 2, 3, 1, 4))
    k = jnp.transpose(k, (0, 2, 3, 1, 4))
    v = jnp.transpose(v, (0, 2, 1, 3))
    cos, sin = rope_tables(lp, DA_HEAD_DIM)
    q = apply_rope(q, cos, sin) * (DA_HEAD_DIM ** -0.5)
    k = apply_rope(k, cos, sin)
    key_pos = jnp.arange(lp)

    def q_block(i):
        start = i * Q_BLOCK
        qi = lax.dynamic_slice_in_dim(q, start, Q_BLOCK, axis=3)
        s = jnp.einsum('bhcqd,bhckd->bhcqk', qi, k).astype(jnp.float32)
        q_pos = start + jnp.arange(Q_BLOCK)
        causal = key_pos[None, :] <= q_pos[:, None]
        p = jax.nn.softmax(jnp.where(causal, s, -jnp.inf), axis=-1)
        a = p[:, :, 0] - lam * p[:, :, 1]
        return jnp.einsum('bhqk,bhkv->bhqv', a.astype(v.dtype), v)

    o = lax.map(q_block, jnp.arange(n_blk))
    o = jnp.transpose(o, (1, 0, 3, 2, 4)).reshape(bsz, lp, DA_HEADS, DA_V_DIM)[:, :seq_len]
    o = rms_norm(o, subln_g) * (1.0 - lam_init)
    return o.reshape(bsz, seq_len, DA_V_WIDTH)


def conformer_conv(glu_a, glu_b, conv_w, conv_b, ln_g, ln_b, w_o, b_o):
    u = glu_a * jax.nn.sigmoid(glu_b)
    y = lax.conv_general_dilated(u, conv_w[:, None, :].astype(u.dtype), window_strides=(1,),
                                 padding=[(CONV_WIDTH - 1, 0)],
                                 dimension_numbers=('NWC', 'WIO', 'NWC'),
                                 feature_group_count=CONV_CH) + conv_b
    y = jax.nn.silu(layer_norm(y, ln_g, ln_b))
    return y @ w_o + b_o


def moe_ffn(h, w_router, router_bias, w_gate_e, w_up_e, w_down_e, w_gate_s, w_up_s, w_down_s):
    n_tok, d = h.shape
    scores = jax.nn.sigmoid((h @ w_router).astype(jnp.float32))
    sel = scores + router_bias.astype(jnp.float32)
    grp = sel.reshape(n_tok, N_GROUPS, N_EXPERTS // N_GROUPS)
    grp_score = jnp.sum(lax.top_k(grp, 2)[0], axis=-1)
    _, top_g = lax.top_k(grp_score, TOPK_GROUPS)
    gmask = jnp.any(top_g[:, :, None] == jnp.arange(N_GROUPS)[None, None, :], axis=1)
    emask = jnp.repeat(gmask, N_EXPERTS // N_GROUPS, axis=1)
    _, top_e = lax.top_k(jnp.where(emask, sel, -jnp.inf), TOP_K)
    gw = jnp.take_along_axis(scores, top_e, axis=-1)
    gw = gw / jnp.sum(gw, axis=-1, keepdims=True) * ROUTED_SCALE

    n_assign = n_tok * TOP_K
    flat_e = top_e.reshape(-1)
    flat_tok = jnp.repeat(jnp.arange(n_tok, dtype=jnp.int32), TOP_K)
    flat_w = gw.reshape(-1)
    order = jnp.argsort(flat_e)
    se = flat_e[order]
    counts = jnp.bincount(flat_e, length=N_EXPERTS)
    padded = (counts + MOE_BLOCK - 1) // MOE_BLOCK * MOE_BLOCK
    start = jnp.cumsum(counts) - counts
    pend = jnp.cumsum(padded)
    pstart = pend - padded
    dest = pstart[se] + jnp.arange(n_assign) - start[se]
    n_rows = -(-(n_assign + N_EXPERTS * (MOE_BLOCK - 1)) // MOE_BLOCK) * MOE_BLOCK
    n_blocks = n_rows // MOE_BLOCK
    row_tok = jnp.full((n_rows,), n_tok, jnp.int32).at[dest].set(flat_tok[order])
    row_w = jnp.zeros((n_rows,), jnp.float32).at[dest].set(flat_w[order])
    blk_e = jnp.minimum(jnp.searchsorted(pend, jnp.arange(n_blocks) * MOE_BLOCK, side='right'), N_EXPERTS - 1)
    h_pad = jnp.concatenate([h, jnp.zeros((1, d), h.dtype)], axis=0)

    def expert_block(args):
        tok, e = args
        xb = h_pad[tok]
        return (jax.nn.silu(xb @ w_gate_e[e]) * (xb @ w_up_e[e])) @ w_down_e[e]

    y = lax.map(expert_block, (row_tok.reshape(n_blocks, MOE_BLOCK), blk_e))
    y = y.reshape(n_rows, d) * row_w[:, None].astype(y.dtype)
    routed = jax.ops.segment_sum(y, row_tok, num_segments=n_tok + 1)[:n_tok]
    shared = (jax.nn.silu(h @ w_gate_s) * (h @ w_up_s)) @ w_down_s
    return routed + shared


def setup_inputs(seed: int = 0) -> dict:
    key = jax.random.key(seed)
    ks = jax.random.split(key, 32)
    f32 = jnp.float32
    D = D_MODEL
    L_ = DEPTH

    def nrm(k, shape, scale):
        return jax.random.normal(k, shape, f32) * scale

    col_scale = jnp.concatenate([jnp.ones((2 * DA_QK_WIDTH,), f32),
                                 jnp.full((DA_V_WIDTH + CONV_CH,), DEEPNORM_BETA, f32),
                                 jnp.ones((CONV_CH + 2 * D,), f32)])
    return {
        'x': nrm(ks[0], (BATCH, SEQ, D), 1.0),
        'meta_tokens': nrm(ks[1], (N_META, D), 1.0),
        'ln_in_g': 1.0 + nrm(ks[2], (D,), 0.02),
        'ln_in_b': nrm(ks[3], (D,), 0.02),
        'w_in': nrm(ks[4], (L_, D, IN_WIDTH), D ** -0.5) * col_scale,
        'lambda_q1': nrm(ks[5], (L_, DA_HEAD_DIM), 0.1),
        'lambda_k1': nrm(ks[6], (L_, DA_HEAD_DIM), 0.1),
        'lambda_q2': nrm(ks[7], (L_, DA_HEAD_DIM), 0.1),
        'lambda_k2': nrm(ks[8], (L_, DA_HEAD_DIM), 0.1),
        'subln_g': 1.0 + nrm(ks[9], (L_, DA_V_DIM), 0.02),
        'w_attn_o': nrm(ks[10], (L_, DA_V_WIDTH, D), DA_V_WIDTH ** -0.5 * DEEPNORM_BETA),
        'conv_w': nrm(ks[11], (L_, CONV_WIDTH, CONV_CH), CONV_WIDTH ** -0.5),
        'conv_b': nrm(ks[12], (L_, CONV_CH), 0.02),
        'conv_ln_g': 1.0 + nrm(ks[13], (L_, CONV_CH), 0.02),
        'conv_ln_b': nrm(ks[14], (L_, CONV_CH), 0.02),
        'w_conv_o': nrm(ks[15], (L_, CONV_CH, D), CONV_CH ** -0.5 * DEEPNORM_BETA),
        'b_conv_o': nrm(ks[16], (L_, D), 0.02),
        'w_out': nrm(ks[17], (L_, D, D), D ** -0.5 * DEEPNORM_BETA),
        'ln1_g': 1.0 + nrm(ks[18], (L_, D), 0.02),
        'ln1_b': nrm(ks[19], (L_, D), 0.02),
        'w_router': nrm(ks[20], (L_, D, N_EXPERTS), D ** -0.5),
        'router_bias': nrm(ks[21], (L_, N_EXPERTS), 0.01),
        'w_gate_e': nrm(ks[22], (L_, N_EXPERTS, D, EXPERT_HIDDEN), D ** -0.5 * DEEPNORM_BETA),
        'w_up_e': nrm(ks[23], (L_, N_EXPERTS, D, EXPERT_HIDDEN), D ** -0.5 * DEEPNORM_BETA),
        'w_down_e': nrm(ks[24], (L_, N_EXPERTS, EXPERT_HIDDEN, D), EXPERT_HIDDEN ** -0.5 * DEEPNORM_BETA),
        'w_gate_s': nrm(ks[25], (L_, D, SHARED_HIDDEN), D ** -0.5 * DEEPNORM_BETA),
        'w_up_s': nrm(ks[26], (L_, D, SHARED_HIDDEN), D ** -0.5 * DEEPNORM_BETA),
        'w_down_s': nrm(ks[27], (L_, SHARED_HIDDEN, D), SHARED_HIDDEN ** -0.5 * DEEPNORM_BETA),
        'ln2_g': 1.0 + nrm(ks[28], (L_, D), 0.02),
        'ln2_b': nrm(ks[29], (L_, D), 0.02),
    }


def reference(x, meta_tokens, ln_in_g, ln_in_b, w_in, lambda_q1, lambda_k1, lambda_q2, lambda_k2,
              subln_g, w_attn_o, conv_w, conv_b, conv_ln_g, conv_ln_b, w_conv_o, b_conv_o, w_out,
              ln1_g, ln1_b, w_router, router_bias, w_gate_e, w_up_e, w_down_e, w_gate_s, w_up_s,
              w_down_s, ln2_g, ln2_b):
    bsz, seq, d = x.shape
    meta = jnp.broadcast_to(meta_tokens[None].astype(x.dtype), (bsz, N_META, d))
    h = layer_norm(jnp.concatenate([meta, x], axis=1), ln_in_g, ln_in_b)
    seq_len = seq + N_META
    for l in range(DEPTH):
        lam_init = 0.8 - 0.6 * math.exp(-0.3 * l)
        proj = h @ w_in[l]
        q, k, v, glu_a, glu_b, g_attn, g_conv = jnp.split(proj, SPLIT_POINTS, axis=-1)
        q = q.reshape(bsz, seq_len, DA_HEADS, 2, DA_HEAD_DIM)
        k = k.reshape(bsz, seq_len, DA_HEADS, 2, DA_HEAD_DIM)
        v = v.reshape(bsz, seq_len, DA_HEADS, DA_V_DIM)
        lam = (jnp.exp(jnp.sum(lambda_q1[l].astype(jnp.float32) * lambda_k1[l].astype(jnp.float32)))
               - jnp.exp(jnp.sum(lambda_q2[l].astype(jnp.float32) * lambda_k2[l].astype(jnp.float32)))
               + lam_init)
        y_attn = diff_attention(q, k, v, lam, lam_init, subln_g[l]) @ w_attn_o[l]
        y_conv = conformer_conv(glu_a, glu_b, conv_w[l], conv_b[l], conv_ln_g[l], conv_ln_b[l],
                                w_conv_o[l], b_conv_o[l])
        merged = jax.nn.sigmoid(g_attn) * y_attn + jax.nn.sigmoid(g_conv) * y_conv
        h = layer_norm(DEEPNORM_ALPHA * h + merged @ w_out[l], ln1_g[l], ln1_b[l])
        m = moe_ffn(h.reshape(bsz * seq_len, d), w_router[l], router_bias[l], w_gate_e[l], w_up_e[l],
                    w_down_e[l], w_gate_s[l], w_up_s[l], w_down_s[l]).reshape(bsz, seq_len, d)
        h = layer_norm(DEEPNORM_ALPHA * h + m, ln2_g[l], ln2_b[l])
    return h[:, N_META:]
```

```python
import functools
import math

import jax
import jax.numpy as jnp
from jax import lax
from jax.experimental import pallas as pl
from jax.experimental.pallas import tpu as pltpu

N_META = 16
HEADS = 4
HEAD_DIM = 64
V_DIM = 2 * HEAD_DIM
QK_WIDTH = HEADS * 2 * HEAD_DIM
V_WIDTH = HEADS * V_DIM
ROPE_THETA = 10000.0
CONV_WIDTH = 31
N_EXPERTS = 256
TOP_K = 8
N_GROUPS = 8
TOPK_GROUPS = 4
GROUP_SIZE = N_EXPERTS // N_GROUPS
ROUTED_SCALE = 2.5
LN_EPS = 1e-5
DEPTH = 1
ALPHA = (2.0 * DEPTH) ** 0.25
LAM_INIT = 0.8 - 0.6 * math.exp(-0.3 * 0)

F32 = jnp.float32
BF16 = jnp.bfloat16
NEG_INF = float("-inf")

VMEM_LIMIT = 56 * 1024 * 1024


def _ln(x, g, b):
    mu = jnp.mean(x, axis=-1, keepdims=True)
    xc = x - mu
    var = jnp.mean(xc * xc, axis=-1, keepdims=True)
    return xc * lax.rsqrt(var + LN_EPS) * g + b


def _dot(a, b):
    return jnp.dot(a, b, preferred_element_type=F32)


def _dot_nt(a, b):
    return lax.dot_general(a, b, (((1,), (1,)), ((), ())), preferred_element_type=F32)


def _silu(x):
    return x * jax.nn.sigmoid(x)


def _inproj_kernel(x_ref, g_ref, b_ref, w_ref, cos_ref, sin_ref,
                   q_ref, k_ref, v_ref, u_ref, ga_ref, gc_ref, *, d_model):
    h = _ln(x_ref[...], g_ref[...], b_ref[...]).astype(BF16)
    c = d_model // 2
    o_k = QK_WIDTH
    o_v = 2 * QK_WIDTH
    o_a = o_v + V_WIDTH
    o_b = o_a + c
    o_ga = o_b + c
    o_gc = o_ga + d_model

    def proj(lo, width):
        return _dot(h, w_ref[:, lo:lo + width])

    reps = QK_WIDTH // cos_ref.shape[1]
    cos = jnp.tile(cos_ref[...], (1, reps))
    sin = jnp.tile(sin_ref[...], (1, reps))
    lane = lax.broadcasted_iota(jnp.int32, (x_ref.shape[0], QK_WIDTH), 1)
    first_half = (lane % HEAD_DIM) < (HEAD_DIM // 2)

    def rope(t):
        partner = jnp.where(first_half,
                            pltpu.roll(t, QK_WIDTH - HEAD_DIM // 2, 1),
                            pltpu.roll(t, HEAD_DIM // 2, 1))
        return t * cos + partner * sin

    q_ref[...] = (rope(proj(0, QK_WIDTH)) * (HEAD_DIM ** -0.5)).astype(BF16)
    k_ref[...] = rope(proj(o_k, QK_WIDTH)).astype(BF16)
    v_ref[...] = proj(o_v, V_WIDTH).astype(BF16)
    u_ref[...] = proj(o_a, c) * jax.nn.sigmoid(proj(o_b, c))
    ga_ref[...] = jax.nn.sigmoid(proj(o_ga, d_model)).astype(BF16)
    gc_ref[...] = jax.nn.sigmoid(proj(o_gc, d_model)).astype(BF16)


def _inproj(x2, g, b, w_bf, cos, sin, tm):
    t, d = x2.shape
    c = d // 2
    n_pos_tiles = cos.shape[0] // tm
    row = lambda i: (i, 0)
    const = lambda i: (0, 0)
    pos = lambda i: (i % n_pos_tiles, 0)
    return pl.pallas_call(
        functools.partial(_inproj_kernel, d_model=d),
        grid=(t // tm,),
        in_specs=[
            pl.BlockSpec((tm, d), row),
            pl.BlockSpec((1, d), const),
            pl.BlockSpec((1, d), const),
            pl.BlockSpec(w_bf.shape, const),
            pl.BlockSpec((tm, cos.shape[1]), pos),
            pl.BlockSpec((tm, cos.shape[1]), pos),
        ],
        out_specs=[
            pl.BlockSpec((tm, QK_WIDTH), row),
            pl.BlockSpec((tm, QK_WIDTH), row),
            pl.BlockSpec((tm, V_WIDTH), row),
            pl.BlockSpec((tm, c), row),
            pl.BlockSpec((tm, d), row),
            pl.BlockSpec((tm, d), row),
        ],
        out_shape=[
            jax.ShapeDtypeStruct((t, QK_WIDTH), BF16),
            jax.ShapeDtypeStruct((t, QK_WIDTH), BF16),
            jax.ShapeDtypeStruct((t, V_WIDTH), BF16),
            jax.ShapeDtypeStruct((t, c), F32),
            jax.ShapeDtypeStruct((t, d), BF16),
            jax.ShapeDtypeStruct((t, d), BF16),
        ],
        compiler_params=pltpu.CompilerParams(
            dimension_semantics=("arbitrary",), vmem_limit_bytes=VMEM_LIMIT),
        name="inproj",
    )(x2, g, b, w_bf, cos, sin)


def _attn_kernel(lam_ref, q_ref, k_ref, v_ref, km_ref, vm_ref, g_ref, o_ref,
                 m_sc, l_sc, acc_sc, *, tq):
    i = pl.program_id(2)
    q = q_ref[...]
    lane = lax.broadcasted_iota(jnp.int32, q.shape, 1)
    zero = jnp.zeros_like(q)
    q_maps = (jnp.where(lane < HEAD_DIM, q, zero), jnp.where(lane >= HEAD_DIM, q, zero))

    m_sc[...] = jnp.full(m_sc.shape, NEG_INF, F32)
    l_sc[...] = jnp.zeros(l_sc.shape, F32)
    acc_sc[...] = jnp.zeros(acc_sc.shape, F32)

    def update(kc, vc, mask):
        for mi in range(2):
            s = _dot_nt(q_maps[mi], kc)
            if mask is not None:
                s = jnp.where(mask, s, NEG_INF)
            m_old = m_sc[mi]
            m_new = jnp.maximum(m_old, jnp.max(s, axis=-1, keepdims=True))
            alpha = jnp.exp(m_old - m_new)
            p = jnp.exp(s - m_new)
            l_sc[mi] = alpha * l_sc[mi] + jnp.sum(p, axis=-1, keepdims=True)
            acc_sc[mi] = alpha * acc_sc[mi] + _dot(p.astype(BF16), vc)
            m_sc[mi] = m_new

    update(km_ref[...], vm_ref[...], None)

    def body(j, carry):
        r0 = pl.multiple_of(j * tq, tq)
        update(k_ref[pl.ds(r0, tq), :], v_ref[pl.ds(r0, tq), :], None)
        return carry

    lax.fori_loop(0, i, body, 0)

    r0 = pl.multiple_of(i * tq, tq)
    rows = lax.broadcasted_iota(jnp.int32, (tq, tq), 0)
    cols = lax.broadcasted_iota(jnp.int32, (tq, tq), 1)
    update(k_ref[pl.ds(r0, tq), :], v_ref[pl.ds(r0, tq), :], cols <= rows)

    lam = lam_ref[0, 0]
    o = acc_sc[0] / l_sc[0] - lam * (acc_sc[1] / l_sc[1])
    o = o * lax.rsqrt(jnp.mean(o * o, axis=-1, keepdims=True) + LN_EPS) * g_ref[...]
    o_ref[...] = (o * (1.0 - LAM_INIT)).astype(o_ref.dtype)


def _attention(lam, q, k, v, km, vm, subln_g, bsz, seq, tq):
    t = q.shape[0]
    nq = seq // tq
    qmap = lambda b, h, i: (b * nq + i, h)
    kvmap = lambda b, h, i: (b, h)
    mmap = lambda b, h, i: (0, h)
    return pl.pallas_call(
        functools.partial(_attn_kernel, tq=tq),
        grid=(bsz, HEADS, nq),
        in_specs=[
            pl.BlockSpec(memory_space=pltpu.SMEM),
            pl.BlockSpec((tq, V_DIM), qmap),
            pl.BlockSpec((seq, V_DIM), kvmap),
            pl.BlockSpec((seq, V_DIM), kvmap),
            pl.BlockSpec((N_META, V_DIM), mmap),
            pl.BlockSpec((N_META, V_DIM), mmap),
            pl.BlockSpec((1, V_DIM), lambda b, h, i: (0, 0)),
        ],
        out_specs=pl.BlockSpec((tq, V_DIM), qmap),
        out_shape=jax.ShapeDtypeStruct((t, V_WIDTH), BF16),
        scratch_shapes=[
            pltpu.VMEM((2, tq, 1), F32),
            pltpu.VMEM((2, tq, 1), F32),
            pltpu.VMEM((2, tq, V_DIM), F32),
        ],
        compiler_params=pltpu.CompilerParams(
            dimension_semantics=("arbitrary", "arbitrary", "arbitrary"),
            vmem_limit_bytes=VMEM_LIMIT),
        name="diff_attention",
    )(lam, q, k, v, km, vm, subln_g)


CONV_ROWS = 32
CONV_WINDOW = 64
CONV_PAD = 32


def _conv_kernel(u_ref, um_ref, w_ref, cb_ref, g_ref, b_ref, z_ref, ucat, *, seq):
    c = u_ref.shape[1]
    ucat[0:CONV_PAD - N_META, :] = jnp.zeros((CONV_PAD - N_META, c), F32)
    ucat[CONV_PAD - N_META:CONV_PAD, :] = um_ref[...]
    ucat[CONV_PAD:CONV_PAD + seq, :] = u_ref[...]
    shift = CONV_PAD - (CONV_WIDTH - 1)

    def body(t, carry):
        r0 = pl.multiple_of(t * CONV_ROWS, CONV_ROWS)
        win = ucat[pl.ds(r0, CONV_WINDOW), :]
        acc = jnp.zeros((CONV_ROWS, c), F32)
        for j in range(CONV_WIDTH):
            acc = acc + win[shift + j:shift + j + CONV_ROWS, :] * w_ref[j:j + 1, :]
        y = _ln(acc + cb_ref[...], g_ref[...], b_ref[...])
        z_ref[pl.ds(r0, CONV_ROWS), :] = _silu(y).astype(z_ref.dtype)
        return carry

    lax.fori_loop(0, seq // CONV_ROWS, body, 0)


def _conv(u, um, conv_w, conv_b, ln_g, ln_b, bsz, seq):
    t, c = u.shape
    const = lambda b: (0, 0)
    return pl.pallas_call(
        functools.partial(_conv_kernel, seq=seq),
        grid=(bsz,),
        in_specs=[
            pl.BlockSpec((seq, c), lambda b: (b, 0)),
            pl.BlockSpec((N_META, c), const),
            pl.BlockSpec((CONV_WIDTH, c), const),
            pl.BlockSpec((1, c), const),
            pl.BlockSpec((1, c), const),
            pl.BlockSpec((1, c), const),
        ],
        out_specs=pl.BlockSpec((seq, c), lambda b: (b, 0)),
        out_shape=jax.ShapeDtypeStruct((t, c), BF16),
        scratch_shapes=[pltpu.VMEM((CONV_PAD + seq, c), F32)],
        compiler_params=pltpu.CompilerParams(
            dimension_semantics=("arbitrary",), vmem_limit_bytes=VMEM_LIMIT),
        name="conformer_conv",
    )(u, um, conv_w, conv_b, ln_g, ln_b)


def _merge_route_kernel(x_ref, lg_ref, lb_ref, oa_ref, z_ref, ga_ref, gc_ref,
                        wao_ref, wco_ref, bco_ref, wout_ref, g1_ref, b1_ref,
                        wrt_ref, rb_ref,
                        h1_ref, te_ref, gw_ref, rk_ref, cnt_ref, carry_sc):
    tm = x_ref.shape[0]

    @pl.when(pl.program_id(0) == 0)
    def _():
        carry_sc[...] = jnp.zeros(carry_sc.shape, F32)

    h0 = _ln(x_ref[...], lg_ref[...], lb_ref[...])
    y_attn = _dot(oa_ref[...], wao_ref[...])
    y_conv = _dot(z_ref[...], wco_ref[...]) + bco_ref[...]
    merged = ga_ref[...].astype(F32) * y_attn + gc_ref[...].astype(F32) * y_conv
    h1 = _ln(ALPHA * h0 + _dot(merged.astype(BF16), wout_ref[...]), g1_ref[...], b1_ref[...])
    h1_ref[...] = h1

    scores = jax.nn.sigmoid(_dot_nt(wrt_ref[...], h1.astype(BF16)))
    sel = scores + rb_ref[...]

    sub = lax.broadcasted_iota(jnp.int32, (GROUP_SIZE, tm), 0)
    gscore = []
    for g in range(N_GROUPS):
        slab = sel[g * GROUP_SIZE:(g + 1) * GROUP_SIZE, :]
        top1 = jnp.max(slab, axis=0, keepdims=True)
        arg1 = jnp.min(jnp.where(slab == top1, sub, GROUP_SIZE), axis=0, keepdims=True)
        top2 = jnp.max(jnp.where(sub == arg1, NEG_INF, slab), axis=0, keepdims=True)
        gscore.append(top1 + top2)

    eiota = lax.broadcasted_iota(jnp.int32, (N_EXPERTS, tm), 0)
    egroup = eiota // GROUP_SIZE
    allowed = jnp.zeros((N_EXPERTS, tm), F32)
    for _ in range(TOPK_GROUPS):
        best = functools.reduce(jnp.maximum, gscore)
        pick = jnp.full((1, tm), N_GROUPS, jnp.int32)
        for g in reversed(range(N_GROUPS)):
            pick = jnp.where(gscore[g] == best, g, pick)
        gscore = [jnp.where(pick == g, NEG_INF, gscore[g]) for g in range(N_GROUPS)]
        allowed = jnp.where(egroup == pick, 1.0, allowed)

    masked = jnp.where(allowed > 0.5, sel, NEG_INF)

    picks = []
    weights = []
    chosen = jnp.zeros((N_EXPERTS, tm), F32)
    for _ in range(TOP_K):
        best = jnp.max(masked, axis=0, keepdims=True)
        pick = jnp.min(jnp.where(masked == best, eiota, N_EXPERTS), axis=0, keepdims=True)
        hit = eiota == pick
        picks.append(pick)
        weights.append(jnp.sum(jnp.where(hit, scores, 0.0), axis=0, keepdims=True))
        chosen = jnp.where(hit, 1.0, chosen)
        masked = jnp.where(hit, NEG_INF, masked)

    gw = jnp.concatenate(weights, axis=0)
    gw_ref[...] = gw / jnp.sum(gw, axis=0, keepdims=True) * ROUTED_SCALE
    te_ref[...] = jnp.concatenate(picks, axis=0)

    before = (lax.broadcasted_iota(jnp.int32, (tm, tm), 0)
              < lax.broadcasted_iota(jnp.int32, (tm, tm), 1))
    prior = _dot(chosen.astype(BF16), jnp.where(before, 1.0, 0.0).astype(BF16)) + carry_sc[...]
    ranks = [jnp.sum(jnp.where(eiota == p, prior, 0.0), axis=0, keepdims=True) for p in picks]
    rk_ref[...] = jnp.concatenate(ranks, axis=0).astype(jnp.int32)
    carry_sc[...] = carry_sc[...] + jnp.sum(chosen, axis=1, keepdims=True)
    cnt_ref[...] = carry_sc[...]


def _merge_route(x2, lg, lb, oa, z, ga, gc, wao, wco, bco, wout, g1, b1, wrt, rb, tm):
    t, d = x2.shape
    c = d // 2
    row = lambda i: (i, 0)
    col = lambda i: (0, i)
    const = lambda i: (0, 0)
    full = lambda a: pl.BlockSpec(a.shape, const)
    return pl.pallas_call(
        _merge_route_kernel,
        grid=(t // tm,),
        in_specs=[
            pl.BlockSpec((tm, d), row), full(lg), full(lb),
            pl.BlockSpec((tm, V_WIDTH), row), pl.BlockSpec((tm, c), row),
            pl.BlockSpec((tm, d), row), pl.BlockSpec((tm, d), row),
            full(wao), full(wco), full(bco), full(wout), full(g1), full(b1),
            full(wrt), full(rb),
        ],
        out_specs=[
            pl.BlockSpec((tm, d), row),
            pl.BlockSpec((TOP_K, tm), col),
            pl.BlockSpec((TOP_K, tm), col),
            pl.BlockSpec((TOP_K, tm), col),
            pl.BlockSpec((N_EXPERTS, 1), const),
        ],
        out_shape=[
            jax.ShapeDtypeStruct((t, d), F32),
            jax.ShapeDtypeStruct((TOP_K, t), jnp.int32),
            jax.ShapeDtypeStruct((TOP_K, t), F32),
            jax.ShapeDtypeStruct((TOP_K, t), jnp.int32),
            jax.ShapeDtypeStruct((N_EXPERTS, 1), F32),
        ],
        scratch_shapes=[pltpu.VMEM((N_EXPERTS, 1), F32)],
        compiler_params=pltpu.CompilerParams(
            dimension_semantics=("arbitrary",), vmem_limit_bytes=VMEM_LIMIT),
        name="merge_route",
    )(x2, lg, lb, oa, z, ga, gc, wao, wco, bco, wout, g1, b1, wrt, rb)


def _dispatch_kernel(dest_ref, h_ref, xs_ref, sem, *, n_tok):
    tm = h_ref.shape[0]
    base = pl.program_id(0) * tm

    def row_copy(r, k):
        return pltpu.make_async_copy(
            h_ref.at[pl.ds(r, 1)], xs_ref.at[pl.ds(dest_ref[k * n_tok + base + r], 1)], sem)

    def issue(r, carry):
        for k in range(TOP_K):
            row_copy(r, k).start()
        return carry

    lax.fori_loop(0, tm, issue, 0)
    for k in range(TOP_K):
        pltpu.make_async_copy(h_ref, xs_ref.at[pl.ds(0, tm)], sem).wait()


def _dispatch(dest_flat, h1, tm):
    t, d = h1.shape
    return pl.pallas_call(
        functools.partial(_dispatch_kernel, n_tok=t),
        grid_spec=pltpu.PrefetchScalarGridSpec(
            num_scalar_prefetch=1,
            grid=(t // tm,),
            in_specs=[pl.BlockSpec((tm, d), lambda i, dest: (i, 0))],
            out_specs=pl.BlockSpec(memory_space=pl.ANY),
            scratch_shapes=[pltpu.SemaphoreType.DMA],
        ),
        out_shape=jax.ShapeDtypeStruct((t * TOP_K, d), F32),
        compiler_params=pltpu.CompilerParams(
            dimension_semantics=("arbitrary",), has_side_effects=True),
        name="moe_dispatch",
    )(dest_flat, h1)


def _experts_kernel(blk_ref, exp_ref, lo_ref, hi_ref, first_ref,
                    x_ref, wg_ref, wu_ref, wd_ref, y_ref):
    w = pl.program_id(0)
    lo = lo_ref[w]
    hi = hi_ref[w]

    @pl.when(hi > lo)
    def _():
        x = x_ref[...].astype(BF16)
        a = _silu(_dot(x, wg_ref[...].astype(BF16))) * _dot(x, wu_ref[...].astype(BF16))
        y = _dot(a.astype(BF16), wd_ref[...].astype(BF16))
        rows = lax.broadcasted_iota(jnp.int32, (x_ref.shape[0], 1), 0)
        y = jnp.where(jnp.logical_and(rows >= lo, rows < hi), y, 0.0)

        @pl.when(first_ref[w] == 1)
        def _():
            y_ref[...] = y

        @pl.when(first_ref[w] == 0)
        def _():
            y_ref[...] = y_ref[...] + y


def _experts(items, xs, wg, wu, wd, tm):
    n, d = xs.shape
    hid = wg.shape[2]
    n_items = items[0].shape[0]
    xmap = lambda w, blk, exp, lo, hi, first: (blk[w], 0)
    wmap = lambda w, blk, exp, lo, hi, first: (exp[w], 0, 0)
    return pl.pallas_call(
        _experts_kernel,
        grid_spec=pltpu.PrefetchScalarGridSpec(
            num_scalar_prefetch=5,
            grid=(n_items,),
            in_specs=[
                pl.BlockSpec((tm, d), xmap),
                pl.BlockSpec((None, d, hid), wmap),
                pl.BlockSpec((None, d, hid), wmap),
                pl.BlockSpec((None, hid, d), wmap),
            ],
            out_specs=pl.BlockSpec((tm, d), xmap),
        ),
        out_shape=jax.ShapeDtypeStruct((n, d), F32),
        compiler_params=pltpu.CompilerParams(
            dimension_semantics=("arbitrary",), vmem_limit_bytes=VMEM_LIMIT),
        name="moe_experts",
    )(*items, xs, wg, wu, wd)


def _combine_kernel(dest_ref, h_ref, gw_ref, ys_ref, wgs_ref, wus_ref, wds_ref, g2_ref, b2_ref,
                    o_ref, buf, sem, *, n_tok):
    tm = h_ref.shape[0]
    base = pl.program_id(0) * tm

    def issue(r, carry):
        for k in range(TOP_K):
            pltpu.make_async_copy(
                ys_ref.at[pl.ds(dest_ref[k * n_tok + base + r], 1)],
                buf.at[k, pl.ds(r, 1)], sem).start()
        return carry

    lax.fori_loop(0, tm, issue, 0)

    h = h_ref[...]
    hb = h.astype(BF16)
    shared = _dot((_silu(_dot(hb, wgs_ref[...])) * _dot(hb, wus_ref[...])).astype(BF16),
                  wds_ref[...])
    acc = ALPHA * h + shared

    for k in range(TOP_K):
        pltpu.make_async_copy(ys_ref.at[pl.ds(0, tm)], buf.at[k], sem).wait()
    gw = gw_ref[...]
    for k in range(TOP_K):
        acc = acc + buf[k] * gw[:, k:k + 1]
    o_ref[...] = _ln(acc, g2_ref[...], b2_ref[...])


def _combine(dest_flat, h1, gw, ys, wgs, wus, wds, g2, b2, tm):
    t, d = h1.shape
    row = lambda i, dest: (i, 0)
    const = lambda i, dest: (0, 0)
    full = lambda a: pl.BlockSpec(a.shape, const)
    return pl.pallas_call(
        functools.partial(_combine_kernel, n_tok=t),
        grid_spec=pltpu.PrefetchScalarGridSpec(
            num_scalar_prefetch=1,
            grid=(t // tm,),
            in_specs=[
                pl.BlockSpec((tm, d), row),
                pl.BlockSpec((tm, TOP_K), row),
                pl.BlockSpec(memory_space=pl.ANY),
                full(wgs), full(wus), full(wds), full(g2), full(b2),
            ],
            out_specs=pl.BlockSpec((tm, d), row),
            scratch_shapes=[pltpu.VMEM((TOP_K, tm, d), F32), pltpu.SemaphoreType.DMA],
        ),
        out_shape=jax.ShapeDtypeStruct((t, d), F32),
        compiler_params=pltpu.CompilerParams(
            dimension_semantics=("arbitrary",), vmem_limit_bytes=VMEM_LIMIT),
        name="moe_combine",
    )(dest_flat, h1, gw, ys, wgs, wus, wds, g2, b2)


def _work_items(counts, n_rows, tm):
    n_blocks = n_rows // tm
    n_items = n_blocks + N_EXPERTS - 1
    start = jnp.cumsum(counts) - counts
    end = start + counts
    first_blk = start // tm
    last_blk = jnp.maximum(end - 1, start) // tm
    per_expert = jnp.where(counts > 0, last_blk - first_blk + 1, 0)
    item_end = jnp.cumsum(per_expert)
    item_start = item_end - per_expert
    total = item_end[-1]
    w = jnp.arange(n_items, dtype=jnp.int32)
    wc = jnp.minimum(w, total - 1)
    exp = jnp.searchsorted(item_end, wc, side="right").astype(jnp.int32)
    blk = (first_blk[exp] + wc - item_start[exp]).astype(jnp.int32)
    lo = jnp.maximum(start[exp], blk * tm) - blk * tm
    hi = jnp.minimum(end[exp], (blk + 1) * tm) - blk * tm
    valid = w < total
    lo = jnp.where(valid, lo, 0).astype(jnp.int32)
    hi = jnp.where(valid, hi, 0).astype(jnp.int32)
    prev_blk = jnp.concatenate([jnp.full((1,), -1, jnp.int32), blk[:-1]])
    first = jnp.logical_and(valid, blk != prev_blk).astype(jnp.int32)
    return blk, exp, lo, hi, first, start


def _rope_tables(positions):
    inv = 1.0 / (ROPE_THETA ** (jnp.arange(0, HEAD_DIM, 2, dtype=F32) / HEAD_DIM))
    ang = positions.astype(F32)[:, None] * inv[None, :]
    cos = jnp.cos(ang)
    sin = jnp.sin(ang)
    cos = jnp.concatenate([cos, cos], axis=-1)
    sin = jnp.concatenate([-sin, sin], axis=-1)
    reps = V_DIM // HEAD_DIM
    return jnp.tile(cos, (1, reps)), jnp.tile(sin, (1, reps))


def _tile(n, pref):
    return pref if n % pref == 0 else n


def kernel(x, meta_tokens, ln_in_g, ln_in_b, w_in, lambda_q1, lambda_k1, lambda_q2, lambda_k2, subln_g, w_attn_o, conv_w, conv_b, conv_ln_g, conv_ln_b, w_conv_o, b_conv_o, w_out, ln1_g, ln1_b, w_router, router_bias, w_gate_e, w_up_e, w_down_e, w_gate_s, w_up_s, w_down_s, ln2_g, ln2_b):
    bsz, seq, d = x.shape
    t = bsz * seq
    l = 0
    row = lambda a: a.reshape(1, -1).astype(F32)
    x2 = x.reshape(t, d)
    lg, lb = row(ln_in_g), row(ln_in_b)
    w_in_bf = w_in[l].astype(BF16)

    cos_x, sin_x = _rope_tables(jnp.arange(N_META, N_META + seq))
    cos_m, sin_m = _rope_tables(jnp.arange(N_META))

    tm_in = _tile(seq, 512)
    q, k, v, u, ga, gc = _inproj(x2, lg, lb, w_in_bf, cos_x, sin_x, tm_in)
    _, km, vm, um, _, _ = _inproj(meta_tokens.astype(F32), lg, lb, w_in_bf, cos_m, sin_m, N_META)

    lam = (jnp.exp(jnp.sum(lambda_q1[l].astype(F32) * lambda_k1[l].astype(F32)))
           - jnp.exp(jnp.sum(lambda_q2[l].astype(F32) * lambda_k2[l].astype(F32)))
           + LAM_INIT).reshape(1, 1)
    o_attn = _attention(lam, q, k, v, km, vm, row(subln_g[l]), bsz, seq, _tile(seq, 512))

    z = _conv(u, um, conv_w[l].astype(F32), row(conv_b[l]), row(conv_ln_g[l]),
              row(conv_ln_b[l]), bsz, seq)

    h1, top_e, gw_t, rank, counts = _merge_route(
        x2, lg, lb, o_attn, z, ga, gc,
        w_attn_o[l].astype(BF16), w_conv_o[l].astype(BF16), row(b_conv_o[l]),
        w_out[l].astype(BF16), row(ln1_g[l]), row(ln1_b[l]),
        w_router[l].T.astype(BF16), router_bias[l].reshape(-1, 1).astype(F32),
        _tile(t, 256))

    tm_e = 256
    counts = counts.reshape(-1).astype(jnp.int32)
    blk, exp, lo, hi, first, start = _work_items(counts, t * TOP_K, tm_e)
    dest = (start[top_e] + rank).astype(jnp.int32).reshape(-1)

    xs = _dispatch(dest, h1, _tile(t, 128))
    ys = _experts((blk, exp, lo, hi, first), xs, w_gate_e[l], w_up_e[l], w_down_e[l], tm_e)
    out = _combine(dest, h1, gw_t.T, ys,
                   w_gate_s[l].astype(BF16), w_up_s[l].astype(BF16), w_down_s[l].astype(BF16),
                   row(ln2_g[l]), row(ln2_b[l]), _tile(t, 128))
    return out.reshape(bsz, seq, d)
```

```python
import functools
import math

import jax
import jax.numpy as jnp
from jax import lax
from jax.experimental import pallas as pl
from jax.experimental.pallas import tpu as pltpu

N_META = 16
HEADS = 4
HEAD_DIM = 64
V_DIM = 2 * HEAD_DIM
QK_WIDTH = HEADS * 2 * HEAD_DIM
V_WIDTH = HEADS * V_DIM
ROPE_THETA = 10000.0
CONV_WIDTH = 31
N_EXPERTS = 256
TOP_K = 8
N_GROUPS = 8
TOPK_GROUPS = 4
GROUP_SIZE = N_EXPERTS // N_GROUPS
ROUTED_SCALE = 2.5
LN_EPS = 1e-5
DEPTH = 1
ALPHA = (2.0 * DEPTH) ** 0.25
LAM_INIT = 0.8 - 0.6 * math.exp(-0.3 * 0)

F32 = jnp.float32
BF16 = jnp.bfloat16
NEG_INF = float("-inf")

VMEM_LIMIT = 56 * 1024 * 1024


def _ln(x, g, b):
    mu = jnp.mean(x, axis=-1, keepdims=True)
    xc = x - mu
    var = jnp.mean(xc * xc, axis=-1, keepdims=True)
    return xc * lax.rsqrt(var + LN_EPS) * g + b


def _dot(a, b):
    return jnp.dot(a, b, preferred_element_type=F32)


def _dot_nt(a, b):
    return lax.dot_general(a, b, (((1,), (1,)), ((), ())), preferred_element_type=F32)


def _silu(x):
    return x * jax.nn.sigmoid(x)


LOG2E = math.log2(math.e)
HALF_HEAD = HEAD_DIM // 2


def _inproj_kernel(x_ref, g_ref, b_ref, wqt_ref, wk_ref, wvt_ref, wr_ref,
                   cos_ref, sin_ref, cost_ref, sint_ref,
                   qt_ref, k_ref, vt_ref, u_ref, ga_ref, gc_ref, *, d_model):
    h = _ln(x_ref[...], g_ref[...], b_ref[...]).astype(BF16)
    tm = x_ref.shape[0]
    c = d_model // 2

    qt = _dot_nt(wqt_ref[...], h)
    pieces = []
    for g in range(QK_WIDTH // HEAD_DIM):
        lo = g * HEAD_DIM
        pieces.append(qt[lo + HALF_HEAD:lo + HEAD_DIM, :])
        pieces.append(qt[lo:lo + HALF_HEAD, :])
    partner = jnp.concatenate(pieces, axis=0)
    reps_t = QK_WIDTH // cost_ref.shape[0]
    qt = qt * jnp.tile(cost_ref[...], (reps_t, 1)) + partner * jnp.tile(sint_ref[...], (reps_t, 1))
    qt_ref[...] = (qt * (HEAD_DIM ** -0.5 * LOG2E)).astype(BF16)

    kk = _dot(h, wk_ref[...])
    reps = QK_WIDTH // cos_ref.shape[1]
    lane = lax.broadcasted_iota(jnp.int32, (tm, QK_WIDTH), 1)
    partner = jnp.where((lane % HEAD_DIM) < HALF_HEAD,
                        pltpu.roll(kk, QK_WIDTH - HALF_HEAD, 1),
                        pltpu.roll(kk, HALF_HEAD, 1))
    kk = kk * jnp.tile(cos_ref[...], (1, reps)) + partner * jnp.tile(sin_ref[...], (1, reps))
    k_ref[...] = kk.astype(BF16)

    vt = _dot_nt(wvt_ref[...], h).astype(BF16)
    chunk = vt_ref.shape[2]
    for j in range(vt_ref.shape[0]):
        vt_ref[j] = vt[:, j * chunk:(j + 1) * chunk]

    def proj(lo, width):
        return _dot(h, wr_ref[:, lo:lo + width])

    u_ref[...] = proj(0, c) * jax.nn.sigmoid(proj(c, c))
    ga_ref[...] = jax.nn.sigmoid(proj(2 * c, d_model)).astype(BF16)
    gc_ref[...] = jax.nn.sigmoid(proj(2 * c + d_model, d_model)).astype(BF16)


def _inproj(x2, g, b, wqt, wk, wvt, wr, tables, tm, chunk):
    t, d = x2.shape
    c = d // 2
    cos, sin, cost, sint = tables
    n_pos_tiles = cos.shape[0] // tm
    row = lambda i: (i, 0)
    col = lambda i: (0, i)
    const = lambda i: (0, 0)
    pos = lambda i: (i % n_pos_tiles, 0)
    post = lambda i: (0, i % n_pos_tiles)
    full = lambda a: pl.BlockSpec(a.shape, const)
    return pl.pallas_call(
        functools.partial(_inproj_kernel, d_model=d),
        grid=(t // tm,),
        in_specs=[
            pl.BlockSpec((tm, d), row), full(g), full(b),
            full(wqt), full(wk), full(wvt), full(wr),
            pl.BlockSpec((tm, cos.shape[1]), pos),
            pl.BlockSpec((tm, cos.shape[1]), pos),
            pl.BlockSpec((cost.shape[0], tm), post),
            pl.BlockSpec((cost.shape[0], tm), post),
        ],
        out_specs=[
            pl.BlockSpec((QK_WIDTH, tm), col),
            pl.BlockSpec((tm, QK_WIDTH), row),
            pl.BlockSpec((tm // chunk, V_WIDTH, chunk), lambda i: (i, 0, 0)),
            pl.BlockSpec((tm, c), row),
            pl.BlockSpec((tm, d), row),
            pl.BlockSpec((tm, d), row),
        ],
        out_shape=[
            jax.ShapeDtypeStruct((QK_WIDTH, t), BF16),
            jax.ShapeDtypeStruct((t, QK_WIDTH), BF16),
            jax.ShapeDtypeStruct((t // chunk, V_WIDTH, chunk), BF16),
            jax.ShapeDtypeStruct((t, c), F32),
            jax.ShapeDtypeStruct((t, d), BF16),
            jax.ShapeDtypeStruct((t, d), BF16),
        ],
        compiler_params=pltpu.CompilerParams(
            dimension_semantics=("arbitrary",), vmem_limit_bytes=VMEM_LIMIT),
        name="inproj",
    )(x2, g, b, wqt, wk, wvt, wr, cos, sin, cost, sint)


def _attn_kernel(lam_ref, qt_ref, k_ref, vt_ref, km_ref, vmt_ref, g_ref, o_ref, acc_sc, *, tq, tk):
    i = pl.program_id(2)
    qt = qt_ref[...]
    feat = lax.broadcasted_iota(jnp.int32, qt.shape, 0)
    zero = jnp.zeros_like(qt)
    q_maps = (jnp.where(feat < HEAD_DIM, qt, zero), jnp.where(feat >= HEAD_DIM, qt, zero))

    stats = []
    for mi in range(2):
        s = _dot(km_ref[...], q_maps[mi])
        m = jnp.max(s, axis=0, keepdims=True)
        p = jnp.exp2(s - m)
        acc_sc[mi] = _dot(vmt_ref[...], p.astype(BF16))
        stats += [m, jnp.sum(p, axis=0, keepdims=True)]

    def update(carry, kc, vtc, mask):
        new = []
        for mi in range(2):
            m_old, l_old = carry[2 * mi], carry[2 * mi + 1]
            s = _dot(kc, q_maps[mi])
            if mask is not None:
                s = jnp.where(mask, s, NEG_INF)
            m_new = jnp.maximum(m_old, jnp.max(s, axis=0, keepdims=True))
            alpha = jnp.exp2(m_old - m_new)
            p = jnp.exp2(s - m_new)
            acc_sc[mi] = alpha * acc_sc[mi] + _dot(vtc, p.astype(BF16))
            new += [m_new, alpha * l_old + jnp.sum(p, axis=0, keepdims=True)]
        return tuple(new)

    per_tile = tq // tk

    def body(j, carry):
        r0 = pl.multiple_of(j * tk, tk)
        return update(carry, k_ref[pl.ds(r0, tk), :], vt_ref[j], None)

    carry = lax.fori_loop(0, i * per_tile, body, tuple(stats))

    key = lax.broadcasted_iota(jnp.int32, (tk, tq), 0)
    qry = lax.broadcasted_iota(jnp.int32, (tk, tq), 1)
    for d in range(per_tile):
        j = i * per_tile + d
        r0 = pl.multiple_of(j * tk, tk)
        carry = update(carry, k_ref[pl.ds(r0, tk), :], vt_ref[j], key + d * tk <= qry)

    _, l1, _, l2 = carry
    lam = lam_ref[0, 0]
    o = acc_sc[0] / l1 - lam * (acc_sc[1] / l2)
    o = o * lax.rsqrt(jnp.mean(o * o, axis=0, keepdims=True) + LN_EPS) * g_ref[...]
    o_ref[...] = (o * (1.0 - LAM_INIT)).T.astype(o_ref.dtype)


def _attention(lam, qt, k, vt, km, vmt, subln_col, bsz, seq, tq, tk):
    t = k.shape[0]
    nq = seq // tq
    nk = seq // tk
    return pl.pallas_call(
        functools.partial(_attn_kernel, tq=tq, tk=tk),
        grid=(bsz, HEADS, nq),
        in_specs=[
            pl.BlockSpec(memory_space=pltpu.SMEM),
            pl.BlockSpec((V_DIM, tq), lambda b, h, i: (h, b * nq + i)),
            pl.BlockSpec((seq, V_DIM), lambda b, h, i: (b, h)),
            pl.BlockSpec((nk, V_DIM, tk), lambda b, h, i: (b, h, 0)),
            pl.BlockSpec((N_META, V_DIM), lambda b, h, i: (0, h)),
            pl.BlockSpec((V_DIM, N_META), lambda b, h, i: (h, 0)),
            pl.BlockSpec((V_DIM, 1), lambda b, h, i: (0, 0)),
        ],
        out_specs=pl.BlockSpec((tq, V_DIM), lambda b, h, i: (b * nq + i, h)),
        out_shape=jax.ShapeDtypeStruct((t, V_WIDTH), BF16),
        scratch_shapes=[pltpu.VMEM((2, V_DIM, tq), F32)],
        compiler_params=pltpu.CompilerParams(
            dimension_semantics=("arbitrary", "arbitrary", "arbitrary"),
            vmem_limit_bytes=VMEM_LIMIT),
        name="diff_attention",
    )(lam, qt, k, vt, km, vmt, subln_col)


CONV_ROWS = 32
CONV_WINDOW = 64
CONV_PAD = 32


def _conv_kernel(u_ref, um_ref, w_ref, cb_ref, g_ref, b_ref, z_ref, ucat, *, seq):
    c = u_ref.shape[1]
    ucat[0:CONV_PAD - N_META, :] = jnp.zeros((CONV_PAD - N_META, c), F32)
    ucat[CONV_PAD - N_META:CONV_PAD, :] = um_ref[...]
    ucat[CONV_PAD:CONV_PAD + seq, :] = u_ref[...]
    shift = CONV_PAD - (CONV_WIDTH - 1)

    def body(t, carry):
        r0 = pl.multiple_of(t * CONV_ROWS, CONV_ROWS)
        win = ucat[pl.ds(r0, CONV_WINDOW), :]
        acc = jnp.zeros((CONV_ROWS, c), F32)
        for j in range(CONV_WIDTH):
            acc = acc + win[shift + j:shift + j + CONV_ROWS, :] * w_ref[j:j + 1, :]
        y = _ln(acc + cb_ref[...], g_ref[...], b_ref[...])
        z_ref[pl.ds(r0, CONV_ROWS), :] = _silu(y).astype(z_ref.dtype)
        return carry

    lax.fori_loop(0, seq // CONV_ROWS, body, 0)


def _conv(u, um, conv_w, conv_b, ln_g, ln_b, bsz, seq):
    t, c = u.shape
    const = lambda b: (0, 0)
    return pl.pallas_call(
        functools.partial(_conv_kernel, seq=seq),
        grid=(bsz,),
        in_specs=[
            pl.BlockSpec((seq, c), lambda b: (b, 0)),
            pl.BlockSpec((N_META, c), const),
            pl.BlockSpec((CONV_WIDTH, c), const),
            pl.BlockSpec((1, c), const),
            pl.BlockSpec((1, c), const),
            pl.BlockSpec((1, c), const),
        ],
        out_specs=pl.BlockSpec((seq, c), lambda b: (b, 0)),
        out_shape=jax.ShapeDtypeStruct((t, c), BF16),
        scratch_shapes=[pltpu.VMEM((CONV_PAD + seq, c), F32)],
        compiler_params=pltpu.CompilerParams(
            dimension_semantics=("arbitrary",), vmem_limit_bytes=VMEM_LIMIT),
        name="conformer_conv",
    )(u, um, conv_w, conv_b, ln_g, ln_b)


def _merge_route_kernel(x_ref, lg_ref, lb_ref, oa_ref, z_ref, ga_ref, gc_ref,
                        wao_ref, wco_ref, bco_ref, wout_ref, g1_ref, b1_ref,
                        wrt_ref, rb_ref,
                        h1_ref, te_ref, gw_ref, rk_ref, cnt_ref, carry_sc):
    tm = x_ref.shape[0]

    @pl.when(pl.program_id(0) == 0)
    def _():
        carry_sc[...] = jnp.zeros(carry_sc.shape, F32)

    h0 = _ln(x_ref[...], lg_ref[...], lb_ref[...])
    y_attn = _dot(oa_ref[...], wao_ref[...])
    y_conv = _dot(z_ref[...], wco_ref[...]) + bco_ref[...]
    merged = ga_ref[...].astype(F32) * y_attn + gc_ref[...].astype(F32) * y_conv
    h1 = _ln(ALPHA * h0 + _dot(merged.astype(BF16), wout_ref[...]), g1_ref[...], b1_ref[...])
    h1_ref[...] = h1
    h1b = h1.astype(BF16)

    scores = jax.nn.sigmoid(_dot_nt(wrt_ref[...], h1b))
    sel = scores + rb_ref[...]

    sub = lax.broadcasted_iota(jnp.int32, (GROUP_SIZE, tm), 0)
    gscore = []
    for g in range(N_GROUPS):
        slab = sel[g * GROUP_SIZE:(g + 1) * GROUP_SIZE, :]
        top1 = jnp.max(slab, axis=0, keepdims=True)
        arg1 = jnp.min(jnp.where(slab == top1, sub, GROUP_SIZE), axis=0, keepdims=True)
        top2 = jnp.max(jnp.where(sub == arg1, NEG_INF, slab), axis=0, keepdims=True)
        gscore.append(top1 + top2)

    eiota = lax.broadcasted_iota(jnp.int32, (N_EXPERTS, tm), 0)
    egroup = eiota // GROUP_SIZE
    allowed = jnp.zeros((N_EXPERTS, tm), F32)
    for _ in range(TOPK_GROUPS):
        best = functools.reduce(jnp.maximum, gscore)
        pick = jnp.full((1, tm), N_GROUPS, jnp.int32)
        for g in reversed(range(N_GROUPS)):
            pick = jnp.where(gscore[g] == best, g, pick)
        gscore = [jnp.where(pick == g, NEG_INF, gscore[g]) for g in range(N_GROUPS)]
        allowed = jnp.where(egroup == pick, 1.0, allowed)

    masked = jnp.where(allowed > 0.5, sel, NEG_INF)

    picks = []
    weights = []
    chosen = jnp.zeros((N_EXPERTS, tm), F32)
    for _ in range(TOP_K):
        best = jnp.max(masked, axis=0, keepdims=True)
        pick = jnp.min(jnp.where(masked == best, eiota, N_EXPERTS), axis=0, keepdims=True)
        hit = eiota == pick
        picks.append(pick)
        weights.append(jnp.sum(jnp.where(hit, scores, 0.0), axis=0, keepdims=True))
        chosen = jnp.where(hit, 1.0, chosen)
        masked = jnp.where(hit, NEG_INF, masked)

    gw = jnp.concatenate(weights, axis=0)
    gw_ref[...] = gw / jnp.sum(gw, axis=0, keepdims=True) * ROUTED_SCALE
    te_ref[...] = jnp.concatenate(picks, axis=0)

    before = (lax.broadcasted_iota(jnp.int32, (tm, tm), 0)
              < lax.broadcasted_iota(jnp.int32, (tm, tm), 1))
    prior = _dot(chosen.astype(BF16), jnp.where(before, 1.0, 0.0).astype(BF16)) + carry_sc[...]
    ranks = [jnp.sum(jnp.where(eiota == p, prior, 0.0), axis=0, keepdims=True) for p in picks]
    rk_ref[...] = jnp.concatenate(ranks, axis=0).astype(jnp.int32)
    carry_sc[...] = carry_sc[...] + jnp.sum(chosen, axis=1, keepdims=True)
    cnt_ref[...] = carry_sc[...]


def _merge_route(x2, lg, lb, oa, z, ga, gc, wao, wco, bco, wout, g1, b1, wrt, rb, tm):
    t, d = x2.shape
    c = d // 2
    row = lambda i: (i, 0)
    col = lambda i: (0, i)
    const = lambda i: (0, 0)
    full = lambda a: pl.BlockSpec(a.shape, const)
    return pl.pallas_call(
        _merge_route_kernel,
        grid=(t // tm,),
        in_specs=[
            pl.BlockSpec((tm, d), row), full(lg), full(lb),
            pl.BlockSpec((tm, V_WIDTH), row), pl.BlockSpec((tm, c), row),
            pl.BlockSpec((tm, d), row), pl.BlockSpec((tm, d), row),
            full(wao), full(wco), full(bco), full(wout), full(g1), full(b1),
            full(wrt), full(rb),
        ],
        out_specs=[
            pl.BlockSpec((tm, d), row),
            pl.BlockSpec((TOP_K, tm), col),
            pl.BlockSpec((TOP_K, tm), col),
            pl.BlockSpec((TOP_K, tm), col),
            pl.BlockSpec((N_EXPERTS, 1), const),
        ],
        out_shape=[
            jax.ShapeDtypeStruct((t, d), F32),
            jax.ShapeDtypeStruct((TOP_K, t), jnp.int32),
            jax.ShapeDtypeStruct((TOP_K, t), F32),
            jax.ShapeDtypeStruct((TOP_K, t), jnp.int32),
            jax.ShapeDtypeStruct((N_EXPERTS, 1), F32),
        ],
        scratch_shapes=[pltpu.VMEM((N_EXPERTS, 1), F32)],
        compiler_params=pltpu.CompilerParams(
            dimension_semantics=("arbitrary",), vmem_limit_bytes=VMEM_LIMIT),
        name="merge_route",
    )(x2, lg, lb, oa, z, ga, gc, wao, wco, bco, wout, g1, b1, wrt, rb)


def _dest_kernel(te_ref, rk_ref, start_ref, d_ref):
    tm = te_ref.shape[1]
    eiota = lax.broadcasted_iota(jnp.int32, (N_EXPERTS, tm), 0)
    start = start_ref[...]
    rows = [jnp.sum(jnp.where(eiota == te_ref[k:k + 1, :], start, 0.0), axis=0, keepdims=True)
            for k in range(TOP_K)]
    d_ref[...] = jnp.concatenate(rows, axis=0).astype(jnp.int32) + rk_ref[...]


def _dest(top_e, rank, start_col, tm):
    t = top_e.shape[1]
    col = lambda i: (0, i)
    return pl.pallas_call(
        _dest_kernel,
        grid=(t // tm,),
        in_specs=[
            pl.BlockSpec((TOP_K, tm), col),
            pl.BlockSpec((TOP_K, tm), col),
            pl.BlockSpec((N_EXPERTS, 1), lambda i: (0, 0)),
        ],
        out_specs=pl.BlockSpec((TOP_K, tm), col),
        out_shape=jax.ShapeDtypeStruct((TOP_K, t), jnp.int32),
        compiler_params=pltpu.CompilerParams(dimension_semantics=("arbitrary",)),
        name="moe_dest",
    )(top_e, rank, start_col)


def _dispatch_kernel(dest_ref, h_ref, xs_ref, sem, *, n_tok):
    tm = h_ref.shape[0]
    base = pl.program_id(0) * tm

    def row_copy(r, k):
        return pltpu.make_async_copy(
            h_ref.at[pl.ds(r, 1)], xs_ref.at[pl.ds(dest_ref[k * n_tok + base + r], 1)], sem)

    def issue(r, carry):
        for k in range(TOP_K):
            row_copy(r, k).start(priority=k % 2)
        return carry

    lax.fori_loop(0, tm, issue, 0)
    for k in range(TOP_K):
        pltpu.make_async_copy(h_ref, xs_ref.at[pl.ds(0, tm)], sem).wait()


def _dispatch(dest_flat, hp, tm):
    t, d = hp.shape
    return pl.pallas_call(
        functools.partial(_dispatch_kernel, n_tok=t),
        grid_spec=pltpu.PrefetchScalarGridSpec(
            num_scalar_prefetch=1,
            grid=(t // tm,),
            in_specs=[pl.BlockSpec((tm, d), lambda i, dest: (i, 0))],
            out_specs=pl.BlockSpec(memory_space=pl.ANY),
            scratch_shapes=[pltpu.SemaphoreType.DMA],
        ),
        out_shape=jax.ShapeDtypeStruct((t * TOP_K, d), hp.dtype),
        compiler_params=pltpu.CompilerParams(
            dimension_semantics=("arbitrary",), has_side_effects=True),
        name="moe_dispatch",
    )(dest_flat, hp)


def _experts_kernel(blk_ref, exp_ref, lo_ref, hi_ref, first_ref,
                    x_ref, wg_ref, wu_ref, wd_ref, y_ref):
    w = pl.program_id(0)
    lo = lo_ref[w]
    hi = hi_ref[w]

    @pl.when(hi > lo)
    def _():
        x = x_ref[...].astype(BF16)
        a = _silu(_dot(x, wg_ref[...].astype(BF16))) * _dot(x, wu_ref[...].astype(BF16))
        y = _dot(a.astype(BF16), wd_ref[...].astype(BF16))
        rows = lax.broadcasted_iota(jnp.int32, (x_ref.shape[0], 1), 0)
        owned = jnp.logical_and(rows >= lo, rows < hi)

        @pl.when(first_ref[w] == 1)
        def _():
            y_ref[...] = y

        @pl.when(first_ref[w] == 0)
        def _():
            y_ref[...] = jnp.where(owned, y, y_ref[...])


def _experts(items, xs, wg, wu, wd, tm):
    n, d = xs.shape
    hid = wg.shape[2]
    n_items = items[0].shape[0]
    xmap = lambda w, blk, exp, lo, hi, first: (blk[w], 0)
    wmap = lambda w, blk, exp, lo, hi, first: (exp[w], 0, 0)
    return pl.pallas_call(
        _experts_kernel,
        grid_spec=pltpu.PrefetchScalarGridSpec(
            num_scalar_prefetch=5,
            grid=(n_items,),
            in_specs=[
                pl.BlockSpec((tm, d), xmap),
                pl.BlockSpec((None, d, hid), wmap),
                pl.BlockSpec((None, d, hid), wmap),
                pl.BlockSpec((None, hid, d), wmap),
            ],
            out_specs=pl.BlockSpec((tm, d), xmap),
        ),
        out_shape=jax.ShapeDtypeStruct((n, d), xs.dtype),
        compiler_params=pltpu.CompilerParams(
            dimension_semantics=("arbitrary",), vmem_limit_bytes=VMEM_LIMIT),
        name="moe_experts",
    )(*items, xs, wg, wu, wd)


def _combine_kernel(dest_ref, h_ref, gw_ref, ys_ref, wgs_ref, wus_ref, wds_ref, g2_ref, b2_ref,
                    o_ref, buf, sem, *, n_tok):
    tm = h_ref.shape[0]
    base = pl.program_id(0) * tm

    def issue(r, carry):
        for k in range(TOP_K):
            pltpu.make_async_copy(
                ys_ref.at[pl.ds(dest_ref[k * n_tok + base + r], 1)],
                buf.at[k, pl.ds(r, 1)], sem).start(priority=k % 2)
        return carry

    lax.fori_loop(0, tm, issue, 0)

    h = h_ref[...]
    hb = h.astype(BF16)
    shared = _dot((_silu(_dot(hb, wgs_ref[...])) * _dot(hb, wus_ref[...])).astype(BF16),
                  wds_ref[...])
    acc = ALPHA * h + shared

    for k in range(TOP_K):
        pltpu.make_async_copy(ys_ref.at[pl.ds(0, tm)], buf.at[k], sem).wait()
    gw = gw_ref[...]
    for k in range(TOP_K):
        acc = acc + buf[k] * gw[:, k:k + 1]
    o_ref[...] = _ln(acc, g2_ref[...], b2_ref[...])


def _combine(dest_flat, h1, gw, ys, wgs, wus, wds, g2, b2, tm):
    t, d = h1.shape
    row = lambda i, dest: (i, 0)
    const = lambda i, dest: (0, 0)
    full = lambda a: pl.BlockSpec(a.shape, const)
    return pl.pallas_call(
        functools.partial(_combine_kernel, n_tok=t),
        grid_spec=pltpu.PrefetchScalarGridSpec(
            num_scalar_prefetch=1,
            grid=(t // tm,),
            in_specs=[
                pl.BlockSpec((tm, d), row),
                pl.BlockSpec((tm, TOP_K), row),
                pl.BlockSpec(memory_space=pl.ANY),
                full(wgs), full(wus), full(wds), full(g2), full(b2),
            ],
            out_specs=pl.BlockSpec((tm, d), row),
            scratch_shapes=[pltpu.VMEM((TOP_K, tm, ys.shape[1]), ys.dtype),
                            pltpu.SemaphoreType.DMA],
        ),
        out_shape=jax.ShapeDtypeStruct((t, d), F32),
        compiler_params=pltpu.CompilerParams(
            dimension_semantics=("arbitrary",), vmem_limit_bytes=VMEM_LIMIT),
        name="moe_combine",
    )(dest_flat, h1, gw, ys, wgs, wus, wds, g2, b2)


def _work_items(counts, n_rows, tm):
    n_blocks = n_rows // tm
    n_items = n_blocks + N_EXPERTS - 1
    start = jnp.cumsum(counts) - counts
    end = start + counts
    first_blk = start // tm
    last_blk = jnp.maximum(end - 1, start) // tm
    per_expert = jnp.where(counts > 0, last_blk - first_blk + 1, 0)
    item_end = jnp.cumsum(per_expert)
    item_start = item_end - per_expert
    total = item_end[-1]
    w = jnp.arange(n_items, dtype=jnp.int32)
    wc = jnp.minimum(w, total - 1)
    exp = jnp.sum((item_end[None, :] <= wc[:, None]).astype(jnp.int32), axis=1)
    blk = (first_blk[exp] + wc - item_start[exp]).astype(jnp.int32)
    lo = jnp.maximum(start[exp], blk * tm) - blk * tm
    hi = jnp.minimum(end[exp], (blk + 1) * tm) - blk * tm
    valid = w < total
    lo = jnp.where(valid, lo, 0).astype(jnp.int32)
    hi = jnp.where(valid, hi, 0).astype(jnp.int32)
    prev_blk = jnp.concatenate([jnp.full((1,), -1, jnp.int32), blk[:-1]])
    first = jnp.logical_and(valid, blk != prev_blk).astype(jnp.int32)
    return blk, exp, lo, hi, first, start


def _rope_tables(positions):
    inv = 1.0 / (ROPE_THETA ** (jnp.arange(0, HEAD_DIM, 2, dtype=F32) / HEAD_DIM))
    ang = positions.astype(F32)[:, None] * inv[None, :]
    cos = jnp.cos(ang)
    sin = jnp.sin(ang)
    cos = jnp.concatenate([cos, cos], axis=-1)
    sin = jnp.concatenate([-sin, sin], axis=-1)
    reps = V_DIM // HEAD_DIM
    return jnp.tile(cos, (1, reps)), jnp.tile(sin, (1, reps)), cos.T, sin.T


def _tile(n, pref):
    return pref if n % pref == 0 else n


def kernel(x, meta_tokens, ln_in_g, ln_in_b, w_in, lambda_q1, lambda_k1, lambda_q2, lambda_k2, subln_g, w_attn_o, conv_w, conv_b, conv_ln_g, conv_ln_b, w_conv_o, b_conv_o, w_out, ln1_g, ln1_b, w_router, router_bias, w_gate_e, w_up_e, w_down_e, w_gate_s, w_up_s, w_down_s, ln2_g, ln2_b):
    bsz, seq, d = x.shape
    t = bsz * seq
    l = 0
    row = lambda a: a.reshape(1, -1).astype(F32)
    x2 = x.reshape(t, d)
    lg, lb = row(ln_in_g), row(ln_in_b)
    w_in_bf = w_in[l].astype(BF16)
    o_v = 2 * QK_WIDTH
    wqt = w_in_bf[:, :QK_WIDTH].T
    wk = w_in_bf[:, QK_WIDTH:o_v]
    wvt = w_in_bf[:, o_v:o_v + V_WIDTH].T
    wr = w_in_bf[:, o_v + V_WIDTH:]

    tables_x = _rope_tables(jnp.arange(N_META, N_META + seq))
    tables_m = _rope_tables(jnp.arange(N_META))

    tq = _tile(seq, 512)
    tk = _tile(seq, 256)
    qt, k, vt, u, ga, gc = _inproj(x2, lg, lb, wqt, wk, wvt, wr, tables_x, tq, tk)
    _, km, vmt, um, _, _ = _inproj(meta_tokens.astype(F32), lg, lb, wqt, wk, wvt, wr, tables_m,
                                   N_META, N_META)

    lam = (jnp.exp(jnp.sum(lambda_q1[l].astype(F32) * lambda_k1[l].astype(F32)))
           - jnp.exp(jnp.sum(lambda_q2[l].astype(F32) * lambda_k2[l].astype(F32)))
           + LAM_INIT).reshape(1, 1)
    o_attn = _attention(lam, qt, k, vt, km, vmt.reshape(V_WIDTH, N_META),
                        subln_g[l].reshape(-1, 1).astype(F32), bsz, seq, tq, tk)

    z = _conv(u, um, conv_w[l].astype(F32), row(conv_b[l]), row(conv_ln_g[l]),
              row(conv_ln_b[l]), bsz, seq)

    h1, top_e, gw_t, rank, counts = _merge_route(
        x2, lg, lb, o_attn, z, ga, gc,
        w_attn_o[l].astype(BF16), w_conv_o[l].astype(BF16), row(b_conv_o[l]),
        w_out[l].astype(BF16), row(ln1_g[l]), row(ln1_b[l]),
        w_router[l].T.astype(BF16), router_bias[l].reshape(-1, 1).astype(F32),
        _tile(t, 256))

    tm_e = 256
    counts = counts.reshape(-1).astype(jnp.int32)
    blk, exp, lo, hi, first, start = _work_items(counts, t * TOP_K, tm_e)
    dest = _dest(top_e, rank, start.astype(F32).reshape(-1, 1), _tile(t, 512)).reshape(-1)

    xs = _dispatch(dest, h1, _tile(t, 128))
    ys = _experts((blk, exp, lo, hi, first), xs, w_gate_e[l], w_up_e[l], w_down_e[l], tm_e)
    out = _combine(dest, h1, gw_t.T, ys,
                   w_gate_s[l].astype(BF16), w_up_s[l].astype(BF16), w_down_s[l].astype(BF16),
                   row(ln2_g[l]), row(ln2_b[l]), _tile(t, 128))
    return out.reshape(bsz, seq, d)
```

```python
import functools
import math

import jax
import jax.numpy as jnp
from jax import lax
from jax.experimental import pallas as pl
from jax.experimental.pallas import tpu as pltpu

N_META = 16
HEADS = 4
HEAD_DIM = 64
V_DIM = 2 * HEAD_DIM
QK_WIDTH = HEADS * 2 * HEAD_DIM
V_WIDTH = HEADS * V_DIM
ROPE_THETA = 10000.0
CONV_WIDTH = 31
N_EXPERTS = 256
TOP_K = 8
N_GROUPS = 8
TOPK_GROUPS = 4
GROUP_SIZE = N_EXPERTS // N_GROUPS
ROUTED_SCALE = 2.5
LN_EPS = 1e-5
DEPTH = 1
ALPHA = (2.0 * DEPTH) ** 0.25
LAM_INIT = 0.8 - 0.6 * math.exp(-0.3 * 0)

F32 = jnp.float32
BF16 = jnp.bfloat16
NEG_INF = float("-inf")

VMEM_LIMIT = 56 * 1024 * 1024


def _ln(x, g, b):
    mu = jnp.mean(x, axis=-1, keepdims=True)
    xc = x - mu
    var = jnp.mean(xc * xc, axis=-1, keepdims=True)
    return xc * lax.rsqrt(var + LN_EPS) * g + b


def _dot(a, b):
    return jnp.dot(a, b, preferred_element_type=F32)


def _dot_nt(a, b):
    return lax.dot_general(a, b, (((1,), (1,)), ((), ())), preferred_element_type=F32)


def _silu(x):
    return x * jax.nn.sigmoid(x)


LOG2E = math.log2(math.e)
HALF_HEAD = HEAD_DIM // 2


def _inproj_kernel(x_ref, g_ref, b_ref, wqt_ref, wk_ref, wvt_ref, wr_ref,
                   cos_ref, sin_ref, cost_ref, sint_ref,
                   qt_ref, k_ref, vt_ref, u_ref, ga_ref, gc_ref, *, d_model):
    h = _ln(x_ref[...], g_ref[...], b_ref[...]).astype(BF16)
    tm = x_ref.shape[0]
    c = d_model // 2

    qt = _dot_nt(wqt_ref[...], h)
    pieces = []
    for g in range(QK_WIDTH // HEAD_DIM):
        lo = g * HEAD_DIM
        pieces.append(qt[lo + HALF_HEAD:lo + HEAD_DIM, :])
        pieces.append(qt[lo:lo + HALF_HEAD, :])
    partner = jnp.concatenate(pieces, axis=0)
    reps_t = QK_WIDTH // cost_ref.shape[0]
    qt = qt * jnp.tile(cost_ref[...], (reps_t, 1)) + partner * jnp.tile(sint_ref[...], (reps_t, 1))
    qt_ref[...] = (qt * (HEAD_DIM ** -0.5 * LOG2E)).astype(BF16)

    kk = _dot(h, wk_ref[...])
    reps = QK_WIDTH // cos_ref.shape[1]
    lane = lax.broadcasted_iota(jnp.int32, (tm, QK_WIDTH), 1)
    partner = jnp.where((lane % HEAD_DIM) < HALF_HEAD,
                        pltpu.roll(kk, QK_WIDTH - HALF_HEAD, 1),
                        pltpu.roll(kk, HALF_HEAD, 1))
    kk = kk * jnp.tile(cos_ref[...], (1, reps)) + partner * jnp.tile(sin_ref[...], (1, reps))
    k_ref[...] = kk.astype(BF16)

    vt = _dot_nt(wvt_ref[...], h).astype(BF16)
    chunk = vt_ref.shape[2]
    for j in range(vt_ref.shape[0]):
        vt_ref[j] = vt[:, j * chunk:(j + 1) * chunk]

    def proj(lo, width):
        return _dot(h, wr_ref[:, lo:lo + width])

    u_ref[...] = proj(0, c) * jax.nn.sigmoid(proj(c, c))
    ga_ref[...] = jax.nn.sigmoid(proj(2 * c, d_model)).astype(BF16)
    gc_ref[...] = jax.nn.sigmoid(proj(2 * c + d_model, d_model)).astype(BF16)


def _inproj(x2, g, b, wqt, wk, wvt, wr, tables, tm, chunk):
    t, d = x2.shape
    c = d // 2
    cos, sin, cost, sint = tables
    n_pos_tiles = cos.shape[0] // tm
    row = lambda i: (i, 0)
    col = lambda i: (0, i)
    const = lambda i: (0, 0)
    pos = lambda i: (i % n_pos_tiles, 0)
    post = lambda i: (0, i % n_pos_tiles)
    full = lambda a: pl.BlockSpec(a.shape, const)
    return pl.pallas_call(
        functools.partial(_inproj_kernel, d_model=d),
        grid=(t // tm,),
        in_specs=[
            pl.BlockSpec((tm, d), row), full(g), full(b),
            full(wqt), full(wk), full(wvt), full(wr),
            pl.BlockSpec((tm, cos.shape[1]), pos),
            pl.BlockSpec((tm, cos.shape[1]), pos),
            pl.BlockSpec((cost.shape[0], tm), post),
            pl.BlockSpec((cost.shape[0], tm), post),
        ],
        out_specs=[
            pl.BlockSpec((QK_WIDTH, tm), col),
            pl.BlockSpec((tm, QK_WIDTH), row),
            pl.BlockSpec((tm // chunk, V_WIDTH, chunk), lambda i: (i, 0, 0)),
            pl.BlockSpec((tm, c), row),
            pl.BlockSpec((tm, d), row),
            pl.BlockSpec((tm, d), row),
        ],
        out_shape=[
            jax.ShapeDtypeStruct((QK_WIDTH, t), BF16),
            jax.ShapeDtypeStruct((t, QK_WIDTH), BF16),
            jax.ShapeDtypeStruct((t // chunk, V_WIDTH, chunk), BF16),
            jax.ShapeDtypeStruct((t, c), F32),
            jax.ShapeDtypeStruct((t, d), BF16),
            jax.ShapeDtypeStruct((t, d), BF16),
        ],
        compiler_params=pltpu.CompilerParams(
            dimension_semantics=("arbitrary",), vmem_limit_bytes=VMEM_LIMIT),
        name="inproj",
    )(x2, g, b, wqt, wk, wvt, wr, cos, sin, cost, sint)


def _attn_kernel(lam_ref, qt_ref, k_ref, vt_ref, km_ref, vmt_ref, g_ref, o_ref, acc_sc, *, tq, tk):
    i = pl.program_id(2)
    qt = qt_ref[...]
    feat = lax.broadcasted_iota(jnp.int32, qt.shape, 0)
    zero = jnp.zeros_like(qt)
    q2 = jnp.concatenate([jnp.where(feat < HEAD_DIM, qt, zero),
                          jnp.where(feat >= HEAD_DIM, qt, zero)], axis=1)

    def scores(j):
        r0 = pl.multiple_of(j * tk, tk)
        return _dot(k_ref[pl.ds(r0, tk), :], q2)

    def absorb(m_old, l_old, s, vtc):
        m_new = jnp.maximum(m_old, jnp.max(s, axis=0, keepdims=True))
        alpha = jnp.exp2(m_old - m_new)
        p = jnp.exp2(s - m_new)
        acc_sc[...] = alpha * acc_sc[...] + _dot(vtc, p.astype(BF16))
        return m_new, alpha * l_old + jnp.sum(p, axis=0, keepdims=True)

    s = _dot(km_ref[...], q2)
    m = jnp.max(s, axis=0, keepdims=True)
    p = jnp.exp2(s - m)
    acc_sc[...] = _dot(vmt_ref[...], p.astype(BF16))
    l = jnp.sum(p, axis=0, keepdims=True)

    per_tile = tq // tk
    n_full = i * per_tile

    def body(j, carry):
        m_old, l_old, s_cur = carry
        s_next = scores(j + 1)
        m_new, l_new = absorb(m_old, l_old, s_cur, vt_ref[j])
        return m_new, l_new, s_next

    m, l, s = lax.fori_loop(0, n_full, body, (m, l, scores(0)))

    key = lax.broadcasted_iota(jnp.int32, (tk, 2 * tq), 0)
    qry = lax.broadcasted_iota(jnp.int32, (tk, 2 * tq), 1) % tq
    for d in range(per_tile):
        s_cur = s
        if d + 1 < per_tile:
            s = scores(n_full + d + 1)
        m, l = absorb(m, l, jnp.where(key + d * tk <= qry, s_cur, NEG_INF), vt_ref[n_full + d])

    lam = lam_ref[0, 0]
    o = acc_sc[:, :tq] / l[:, :tq] - lam * (acc_sc[:, tq:] / l[:, tq:])
    o = o * lax.rsqrt(jnp.mean(o * o, axis=0, keepdims=True) + LN_EPS) * g_ref[...]
    o_ref[...] = (o * (1.0 - LAM_INIT)).T.astype(o_ref.dtype)


def _attention(lam, qt, k, vt, km, vmt, subln_col, bsz, seq, tq, tk):
    t = k.shape[0]
    nq = seq // tq
    nk = seq // tk
    return pl.pallas_call(
        functools.partial(_attn_kernel, tq=tq, tk=tk),
        grid=(bsz, HEADS, nq),
        in_specs=[
            pl.BlockSpec(memory_space=pltpu.SMEM),
            pl.BlockSpec((V_DIM, tq), lambda b, h, i: (h, b * nq + i)),
            pl.BlockSpec((seq, V_DIM), lambda b, h, i: (b, h)),
            pl.BlockSpec((nk, V_DIM, tk), lambda b, h, i: (b, h, 0)),
            pl.BlockSpec((N_META, V_DIM), lambda b, h, i: (0, h)),
            pl.BlockSpec((V_DIM, N_META), lambda b, h, i: (h, 0)),
            pl.BlockSpec((V_DIM, 1), lambda b, h, i: (0, 0)),
        ],
        out_specs=pl.BlockSpec((tq, V_DIM), lambda b, h, i: (b * nq + i, h)),
        out_shape=jax.ShapeDtypeStruct((t, V_WIDTH), BF16),
        scratch_shapes=[pltpu.VMEM((V_DIM, 2 * tq), F32)],
        compiler_params=pltpu.CompilerParams(
            dimension_semantics=("arbitrary", "arbitrary", "arbitrary"),
            vmem_limit_bytes=VMEM_LIMIT),
        name="diff_attention",
    )(lam, qt, k, vt, km, vmt, subln_col)


SUBLANES = 8
CONV_ROWS = 32
CONV_WINDOW = 64
CONV_PAD = 32


def _conv_kernel(u_ref, um_ref, w_ref, cb_ref, g_ref, b_ref, z_ref, ucat, grp, *, seq):
    c = u_ref.shape[1]
    ucat[0:CONV_PAD - N_META, :] = jnp.zeros((CONV_PAD - N_META, c), F32)
    ucat[CONV_PAD - N_META:CONV_PAD, :] = um_ref[...]
    ucat[CONV_PAD:CONV_PAD + seq, :] = u_ref[...]
    shift = CONV_PAD - (CONV_WIDTH - 1)

    def body(t, carry):
        r0 = pl.multiple_of(t * CONV_ROWS, CONV_ROWS)
        acc = jnp.zeros((CONV_ROWS, c), F32)
        for r in range(SUBLANES):
            taps = [j for j in range(CONV_WIDTH) if (shift + j) % SUBLANES == r]
            if not taps:
                continue
            lo = shift + taps[0]
            rows = taps[-1] - taps[0] + CONV_ROWS
            start = pl.multiple_of(r0 + lo - r, SUBLANES)
            span = rows + (SUBLANES if r else 0)
            grp[r, 0:rows, :] = ucat[pl.ds(start, span), :][r:r + rows, :]
            for j in taps:
                off = shift + j - lo
                acc = acc + grp[r, off:off + CONV_ROWS, :] * w_ref[j:j + 1, :]
        y = _ln(acc + cb_ref[...], g_ref[...], b_ref[...])
        z_ref[pl.ds(r0, CONV_ROWS), :] = _silu(y).astype(z_ref.dtype)
        return carry

    lax.fori_loop(0, seq // CONV_ROWS, body, 0)


def _conv(u, um, conv_w, conv_b, ln_g, ln_b, bsz, seq):
    t, c = u.shape
    const = lambda b: (0, 0)
    return pl.pallas_call(
        functools.partial(_conv_kernel, seq=seq),
        grid=(bsz,),
        in_specs=[
            pl.BlockSpec((seq, c), lambda b: (b, 0)),
            pl.BlockSpec((N_META, c), const),
            pl.BlockSpec((CONV_WIDTH, c), const),
            pl.BlockSpec((1, c), const),
            pl.BlockSpec((1, c), const),
            pl.BlockSpec((1, c), const),
        ],
        out_specs=pl.BlockSpec((seq, c), lambda b: (b, 0)),
        out_shape=jax.ShapeDtypeStruct((t, c), BF16),
        scratch_shapes=[pltpu.VMEM((CONV_PAD + seq, c), F32),
                        pltpu.VMEM((SUBLANES, CONV_WINDOW, c), F32)],
        compiler_params=pltpu.CompilerParams(
            dimension_semantics=("arbitrary",), vmem_limit_bytes=VMEM_LIMIT),
        name="conformer_conv",
    )(u, um, conv_w, conv_b, ln_g, ln_b)


def _merge_route_kernel(x_ref, lg_ref, lb_ref, oa_ref, z_ref, ga_ref, gc_ref,
                        wao_ref, wco_ref, bco_ref, wout_ref, g1_ref, b1_ref,
                        wrt_ref, rb_ref,
                        h1_ref, te_ref, gw_ref, rk_ref, cnt_ref, carry_sc):
    tm = x_ref.shape[0]

    @pl.when(pl.program_id(0) == 0)
    def _():
        carry_sc[...] = jnp.zeros(carry_sc.shape, F32)

    h0 = _ln(x_ref[...], lg_ref[...], lb_ref[...])
    y_attn = _dot(oa_ref[...], wao_ref[...])
    y_conv = _dot(z_ref[...], wco_ref[...]) + bco_ref[...]
    merged = ga_ref[...].astype(F32) * y_attn + gc_ref[...].astype(F32) * y_conv
    h1 = _ln(ALPHA * h0 + _dot(merged.astype(BF16), wout_ref[...]), g1_ref[...], b1_ref[...])
    h1_ref[...] = h1
    h1b = h1.astype(BF16)

    scores = jax.nn.sigmoid(_dot_nt(wrt_ref[...], h1b))
    sel = scores + rb_ref[...]

    sub = lax.broadcasted_iota(jnp.int32, (GROUP_SIZE, tm), 0)
    gscore = []
    for g in range(N_GROUPS):
        slab = sel[g * GROUP_SIZE:(g + 1) * GROUP_SIZE, :]
        top1 = jnp.max(slab, axis=0, keepdims=True)
        arg1 = jnp.min(jnp.where(slab == top1, sub, GROUP_SIZE), axis=0, keepdims=True)
        top2 = jnp.max(jnp.where(sub == arg1, NEG_INF, slab), axis=0, keepdims=True)
        gscore.append(top1 + top2)

    eiota = lax.broadcasted_iota(jnp.int32, (N_EXPERTS, tm), 0)
    egroup = eiota // GROUP_SIZE
    allowed = jnp.zeros((N_EXPERTS, tm), F32)
    for _ in range(TOPK_GROUPS):
        best = functools.reduce(jnp.maximum, gscore)
        pick = jnp.full((1, tm), N_GROUPS, jnp.int32)
        for g in reversed(range(N_GROUPS)):
            pick = jnp.where(gscore[g] == best, g, pick)
        gscore = [jnp.where(pick == g, NEG_INF, gscore[g]) for g in range(N_GROUPS)]
        allowed = jnp.where(egroup == pick, 1.0, allowed)

    masked = jnp.where(allowed > 0.5, sel, NEG_INF)

    picks = []
    weights = []
    chosen = jnp.zeros((N_EXPERTS, tm), F32)
    for _ in range(TOP_K):
        best = jnp.max(masked, axis=0, keepdims=True)
        pick = jnp.min(jnp.where(masked == best, eiota, N_EXPERTS), axis=0, keepdims=True)
        hit = eiota == pick
        picks.append(pick)
        weights.append(jnp.sum(jnp.where(hit, scores, 0.0), axis=0, keepdims=True))
        chosen = jnp.where(hit, 1.0, chosen)
        masked = jnp.where(hit, NEG_INF, masked)

    gw = jnp.concatenate(weights, axis=0)
    gw_ref[...] = gw / jnp.sum(gw, axis=0, keepdims=True) * ROUTED_SCALE
    te_ref[...] = jnp.concatenate(picks, axis=0)

    before = (lax.broadcasted_iota(jnp.int32, (tm, tm), 0)
              < lax.broadcasted_iota(jnp.int32, (tm, tm), 1))
    prior = _dot(chosen.astype(BF16), jnp.where(before, 1.0, 0.0).astype(BF16)) + carry_sc[...]
    ranks = [jnp.sum(jnp.where(eiota == p, prior, 0.0), axis=0, keepdims=True) for p in picks]
    rk_ref[...] = jnp.concatenate(ranks, axis=0).astype(jnp.int32)
    carry_sc[...] = carry_sc[...] + jnp.sum(chosen, axis=1, keepdims=True)
    cnt_ref[...] = carry_sc[...]


def _merge_route(x2, lg, lb, oa, z, ga, gc, wao, wco, bco, wout, g1, b1, wrt, rb, tm):
    t, d = x2.shape
    c = d // 2
    row = lambda i: (i, 0)
    col = lambda i: (0, i)
    const = lambda i: (0, 0)
    full = lambda a: pl.BlockSpec(a.shape, const)
    return pl.pallas_call(
        _merge_route_kernel,
        grid=(t // tm,),
        in_specs=[
            pl.BlockSpec((tm, d), row), full(lg), full(lb),
            pl.BlockSpec((tm, V_WIDTH), row), pl.BlockSpec((tm, c), row),
            pl.BlockSpec((tm, d), row), pl.BlockSpec((tm, d), row),
            full(wao), full(wco), full(bco), full(wout), full(g1), full(b1),
            full(wrt), full(rb),
        ],
        out_specs=[
            pl.BlockSpec((tm, d), row),
            pl.BlockSpec((TOP_K, tm), col),
            pl.BlockSpec((TOP_K, tm), col),
            pl.BlockSpec((TOP_K, tm), col),
            pl.BlockSpec((N_EXPERTS, 1), const),
        ],
        out_shape=[
            jax.ShapeDtypeStruct((t, d), F32),
            jax.ShapeDtypeStruct((TOP_K, t), jnp.int32),
            jax.ShapeDtypeStruct((TOP_K, t), F32),
            jax.ShapeDtypeStruct((TOP_K, t), jnp.int32),
            jax.ShapeDtypeStruct((N_EXPERTS, 1), F32),
        ],
        scratch_shapes=[pltpu.VMEM((N_EXPERTS, 1), F32)],
        compiler_params=pltpu.CompilerParams(
            dimension_semantics=("arbitrary",), vmem_limit_bytes=VMEM_LIMIT),
        name="merge_route",
    )(x2, lg, lb, oa, z, ga, gc, wao, wco, bco, wout, g1, b1, wrt, rb)


def _dest_kernel(te_ref, rk_ref, start_ref, d_ref):
    tm = te_ref.shape[1]
    eiota = lax.broadcasted_iota(jnp.int32, (N_EXPERTS, tm), 0)
    start = start_ref[...]
    rows = [jnp.sum(jnp.where(eiota == te_ref[k:k + 1, :], start, 0.0), axis=0, keepdims=True)
            for k in range(TOP_K)]
    d_ref[...] = jnp.concatenate(rows, axis=0).astype(jnp.int32) + rk_ref[...]


def _dest(top_e, rank, start_col, tm):
    t = top_e.shape[1]
    col = lambda i: (0, i)
    return pl.pallas_call(
        _dest_kernel,
        grid=(t // tm,),
        in_specs=[
            pl.BlockSpec((TOP_K, tm), col),
            pl.BlockSpec((TOP_K, tm), col),
            pl.BlockSpec((N_EXPERTS, 1), lambda i: (0, 0)),
        ],
        out_specs=pl.BlockSpec((TOP_K, tm), col),
        out_shape=jax.ShapeDtypeStruct((TOP_K, t), jnp.int32),
        compiler_params=pltpu.CompilerParams(dimension_semantics=("arbitrary",)),
        name="moe_dest",
    )(top_e, rank, start_col)


def _dispatch_kernel(dest_ref, h_ref, xs_ref, sem, *, n_tok):
    tm = h_ref.shape[0]
    base = pl.program_id(0) * tm

    def row_copy(r, k):
        return pltpu.make_async_copy(
            h_ref.at[pl.ds(r, 1)], xs_ref.at[pl.ds(dest_ref[k * n_tok + base + r], 1)], sem)

    def issue(r, carry):
        for k in range(TOP_K):
            row_copy(r, k).start(priority=k % 2)
        return carry

    lax.fori_loop(0, tm, issue, 0)
    for k in range(TOP_K):
        pltpu.make_async_copy(h_ref, xs_ref.at[pl.ds(0, tm)], sem).wait()


def _dispatch(dest_flat, hp, tm):
    t, d = hp.shape
    return pl.pallas_call(
        functools.partial(_dispatch_kernel, n_tok=t),
        grid_spec=pltpu.PrefetchScalarGridSpec(
            num_scalar_prefetch=1,
            grid=(t // tm,),
            in_specs=[pl.BlockSpec((tm, d), lambda i, dest: (i, 0))],
            out_specs=pl.BlockSpec(memory_space=pl.ANY),
            scratch_shapes=[pltpu.SemaphoreType.DMA],
        ),
        out_shape=jax.ShapeDtypeStruct((t * TOP_K, d), hp.dtype),
        compiler_params=pltpu.CompilerParams(
            dimension_semantics=("arbitrary",), has_side_effects=True),
        name="moe_dispatch",
    )(dest_flat, hp)


def _experts_kernel(blk_ref, exp_ref, lo_ref, hi_ref, first_ref,
                    x_ref, wg_ref, wu_ref, wd_ref, y_ref):
    w = pl.program_id(0)
    lo = lo_ref[w]
    hi = hi_ref[w]

    @pl.when(hi > lo)
    def _():
        x = x_ref[...].astype(BF16)
        a = _silu(_dot(x, wg_ref[...].astype(BF16))) * _dot(x, wu_ref[...].astype(BF16))
        y = _dot(a.astype(BF16), wd_ref[...].astype(BF16))
        rows = lax.broadcasted_iota(jnp.int32, (x_ref.shape[0], 1), 0)
        owned = jnp.logical_and(rows >= lo, rows < hi)

        @pl.when(first_ref[w] == 1)
        def _():
            y_ref[...] = y

        @pl.when(first_ref[w] == 0)
        def _():
            y_ref[...] = jnp.where(owned, y, y_ref[...])


def _experts(items, xs, wg, wu, wd, tm):
    n, d = xs.shape
    hid = wg.shape[2]
    n_items = items[0].shape[0]
    xmap = lambda w, blk, exp, lo, hi, first: (blk[w], 0)
    wmap = lambda w, blk, exp, lo, hi, first: (exp[w], 0, 0)
    return pl.pallas_call(
        _experts_kernel,
        grid_spec=pltpu.PrefetchScalarGridSpec(
            num_scalar_prefetch=5,
            grid=(n_items,),
            in_specs=[
                pl.BlockSpec((tm, d), xmap),
                pl.BlockSpec((None, d, hid), wmap),
                pl.BlockSpec((None, d, hid), wmap),
                pl.BlockSpec((None, hid, d), wmap),
            ],
            out_specs=pl.BlockSpec((tm, d), xmap),
        ),
        out_shape=jax.ShapeDtypeStruct((n, d), xs.dtype),
        compiler_params=pltpu.CompilerParams(
            dimension_semantics=("arbitrary",), vmem_limit_bytes=VMEM_LIMIT),
        name="moe_experts",
    )(*items, xs, wg, wu, wd)


def _combine_kernel(dest_ref, h_ref, gw_ref, ys_ref, wgs_ref, wus_ref, wds_ref, g2_ref, b2_ref,
                    o_ref, buf, sem, *, n_tok):
    tm = h_ref.shape[0]
    step = pl.program_id(0)
    slot = step % 2

    def gather(tile, into):
        base = tile * tm

        def issue(r, carry):
            for k in range(TOP_K):
                pltpu.make_async_copy(
                    ys_ref.at[pl.ds(dest_ref[k * n_tok + base + r], 1)],
                    buf.at[into, k, pl.ds(r, 1)], sem.at[into]).start(priority=k % 2)
            return carry

        lax.fori_loop(0, tm, issue, 0)

    @pl.when(step == 0)
    def _():
        gather(0, 0)

    @pl.when(step + 1 < pl.num_programs(0))
    def _():
        gather(step + 1, 1 - slot)

    h = h_ref[...]
    hb = h.astype(BF16)
    shared = _dot((_silu(_dot(hb, wgs_ref[...])) * _dot(hb, wus_ref[...])).astype(BF16),
                  wds_ref[...])
    acc = ALPHA * h + shared

    for k in range(TOP_K):
        pltpu.make_async_copy(ys_ref.at[pl.ds(0, tm)], buf.at[slot, k], sem.at[slot]).wait()
    gw = gw_ref[...]
    for k in range(TOP_K):
        acc = acc + buf[slot, k] * gw[:, k:k + 1]
    o_ref[...] = _ln(acc, g2_ref[...], b2_ref[...])


def _combine(dest_flat, h1, gw, ys, wgs, wus, wds, g2, b2, tm):
    t, d = h1.shape
    row = lambda i, dest: (i, 0)
    const = lambda i, dest: (0, 0)
    full = lambda a: pl.BlockSpec(a.shape, const)
    return pl.pallas_call(
        functools.partial(_combine_kernel, n_tok=t),
        grid_spec=pltpu.PrefetchScalarGridSpec(
            num_scalar_prefetch=1,
            grid=(t // tm,),
            in_specs=[
                pl.BlockSpec((tm, d), row),
                pl.BlockSpec((tm, TOP_K), row),
                pl.BlockSpec(memory_space=pl.ANY),
                full(wgs), full(wus), full(wds), full(g2), full(b2),
            ],
            out_specs=pl.BlockSpec((tm, d), row),
            scratch_shapes=[pltpu.VMEM((2, TOP_K, tm, ys.shape[1]), ys.dtype),
                            pltpu.SemaphoreType.DMA((2,))],
        ),
        out_shape=jax.ShapeDtypeStruct((t, d), F32),
        compiler_params=pltpu.CompilerParams(
            dimension_semantics=("arbitrary",), vmem_limit_bytes=VMEM_LIMIT),
        name="moe_combine",
    )(dest_flat, h1, gw, ys, wgs, wus, wds, g2, b2)


def _work_items(counts, n_rows, tm):
    n_blocks = n_rows // tm
    n_items = n_blocks + N_EXPERTS - 1
    start = jnp.cumsum(counts) - counts
    end = start + counts
    first_blk = start // tm
    last_blk = jnp.maximum(end - 1, start) // tm
    per_expert = jnp.where(counts > 0, last_blk - first_blk + 1, 0)
    item_end = jnp.cumsum(per_expert)
    item_start = item_end - per_expert
    total = item_end[-1]
    w = jnp.arange(n_items, dtype=jnp.int32)
    wc = jnp.minimum(w, total - 1)
    exp = jnp.sum((item_end[None, :] <= wc[:, None]).astype(jnp.int32), axis=1)
    blk = (first_blk[exp] + wc - item_start[exp]).astype(jnp.int32)
    lo = jnp.maximum(start[exp], blk * tm) - blk * tm
    hi = jnp.minimum(end[exp], (blk + 1) * tm) - blk * tm
    valid = w < total
    lo = jnp.where(valid, lo, 0).astype(jnp.int32)
    hi = jnp.where(valid, hi, 0).astype(jnp.int32)
    prev_blk = jnp.concatenate([jnp.full((1,), -1, jnp.int32), blk[:-1]])
    first = jnp.logical_and(valid, blk != prev_blk).astype(jnp.int32)
    return blk, exp, lo, hi, first, start


def _rope_tables(positions):
    inv = 1.0 / (ROPE_THETA ** (jnp.arange(0, HEAD_DIM, 2, dtype=F32) / HEAD_DIM))
    ang = positions.astype(F32)[:, None] * inv[None, :]
    cos = jnp.cos(ang)
    sin = jnp.sin(ang)
    cos = jnp.concatenate([cos, cos], axis=-1)
    sin = jnp.concatenate([-sin, sin], axis=-1)
    reps = V_DIM // HEAD_DIM
    return jnp.tile(cos, (1, reps)), jnp.tile(sin, (1, reps)), cos.T, sin.T


def _tile(n, pref):
    return pref if n % pref == 0 else n


def kernel(x, meta_tokens, ln_in_g, ln_in_b, w_in, lambda_q1, lambda_k1, lambda_q2, lambda_k2, subln_g, w_attn_o, conv_w, conv_b, conv_ln_g, conv_ln_b, w_conv_o, b_conv_o, w_out, ln1_g, ln1_b, w_router, router_bias, w_gate_e, w_up_e, w_down_e, w_gate_s, w_up_s, w_down_s, ln2_g, ln2_b):
    bsz, seq, d = x.shape
    t = bsz * seq
    l = 0
    row = lambda a: a.reshape(1, -1).astype(F32)
    x2 = x.reshape(t, d)
    lg, lb = row(ln_in_g), row(ln_in_b)
    w_in_bf = w_in[l].astype(BF16)
    o_v = 2 * QK_WIDTH
    wqt = w_in_bf[:, :QK_WIDTH].T
    wk = w_in_bf[:, QK_WIDTH:o_v]
    wvt = w_in_bf[:, o_v:o_v + V_WIDTH].T
    wr = w_in_bf[:, o_v + V_WIDTH:]

    tables_x = _rope_tables(jnp.arange(N_META, N_META + seq))
    tables_m = _rope_tables(jnp.arange(N_META))

    tq = _tile(seq, 512)
    tk = _tile(seq, 256)
    qt, k, vt, u, ga, gc = _inproj(x2, lg, lb, wqt, wk, wvt, wr, tables_x, tq, tk)
    _, km, vmt, um, _, _ = _inproj(meta_tokens.astype(F32), lg, lb, wqt, wk, wvt, wr, tables_m,
                                   N_META, N_META)

    lam = (jnp.exp(jnp.sum(lambda_q1[l].astype(F32) * lambda_k1[l].astype(F32)))
           - jnp.exp(jnp.sum(lambda_q2[l].astype(F32) * lambda_k2[l].astype(F32)))
           + LAM_INIT).reshape(1, 1)
    o_attn = _attention(lam, qt, k, vt, km, vmt.reshape(V_WIDTH, N_META),
                        subln_g[l].reshape(-1, 1).astype(F32), bsz, seq, tq, tk)

    z = _conv(u, um, conv_w[l].astype(F32), row(conv_b[l]), row(conv_ln_g[l]),
              row(conv_ln_b[l]), bsz, seq)

    h1, top_e, gw_t, rank, counts = _merge_route(
        x2, lg, lb, o_attn, z, ga, gc,
        w_attn_o[l].astype(BF16), w_conv_o[l].astype(BF16), row(b_conv_o[l]),
        w_out[l].astype(BF16), row(ln1_g[l]), row(ln1_b[l]),
        w_router[l].T.astype(BF16), router_bias[l].reshape(-1, 1).astype(F32),
        _tile(t, 256))

    tm_e = 256
    counts = counts.reshape(-1).astype(jnp.int32)
    blk, exp, lo, hi, first, start = _work_items(counts, t * TOP_K, tm_e)
    dest = _dest(top_e, rank, start.astype(F32).reshape(-1, 1), _tile(t, 512)).reshape(-1)

    xs = _dispatch(dest, h1, _tile(t, 128))
    ys = _experts((blk, exp, lo, hi, first), xs, w_gate_e[l], w_up_e[l], w_down_e[l], tm_e)
    out = _combine(dest, h1, gw_t.T, ys,
                   w_gate_s[l].astype(BF16), w_up_s[l].astype(BF16), w_down_s[l].astype(BF16),
                   row(ln2_g[l]), row(ln2_b[l]), _tile(t, 128))
    return out.reshape(bsz, seq, d)
```

```python
import functools
import math

import jax
import jax.numpy as jnp
from jax import lax
from jax.experimental import pallas as pl
from jax.experimental.pallas import tpu as pltpu

N_META = 16
HEADS = 4
HEAD_DIM = 64
V_DIM = 2 * HEAD_DIM
QK_WIDTH = HEADS * 2 * HEAD_DIM
V_WIDTH = HEADS * V_DIM
ROPE_THETA = 10000.0
CONV_WIDTH = 31
N_EXPERTS = 256
TOP_K = 8
N_GROUPS = 8
TOPK_GROUPS = 4
GROUP_SIZE = N_EXPERTS // N_GROUPS
ROUTED_SCALE = 2.5
LN_EPS = 1e-5
DEPTH = 1
ALPHA = (2.0 * DEPTH) ** 0.25
LAM_INIT = 0.8 - 0.6 * math.exp(-0.3 * 0)

F32 = jnp.float32
BF16 = jnp.bfloat16
NEG_INF = float("-inf")

VMEM_LIMIT = 56 * 1024 * 1024


def _ln(x, g, b):
    mu = jnp.mean(x, axis=-1, keepdims=True)
    xc = x - mu
    var = jnp.mean(xc * xc, axis=-1, keepdims=True)
    return xc * lax.rsqrt(var + LN_EPS) * g + b


def _dot(a, b):
    return jnp.dot(a, b, preferred_element_type=F32)


def _dot_nt(a, b):
    return lax.dot_general(a, b, (((1,), (1,)), ((), ())), preferred_element_type=F32)


def _silu(x):
    return x * jax.nn.sigmoid(x)


SUBLANES = 8
LANES = 128


def _rows_from_tiles(ref, n_rows, first=0):
    return jnp.concatenate(
        [ref[pl.ds(first * SUBLANES + c, n_rows, stride=SUBLANES), :] for c in range(SUBLANES)],
        axis=1)


def _rows_to_tiles(ref, x):
    for c in range(SUBLANES):
        ref[pl.ds(c, x.shape[0], stride=SUBLANES), :] = x[:, c * LANES:(c + 1) * LANES]


LOG2E = math.log2(math.e)
HALF_HEAD = HEAD_DIM // 2


def _inproj_kernel(x_ref, g_ref, b_ref, wqt_ref, wk_ref, wvt_ref, wr_ref,
                   cos_ref, sin_ref, cost_ref, sint_ref,
                   qt_ref, k_ref, vt_ref, u_ref, ga_ref, gc_ref, *, d_model):
    h = _ln(x_ref[...], g_ref[...], b_ref[...]).astype(BF16)
    tm = x_ref.shape[0]
    c = d_model // 2

    qt = _dot_nt(wqt_ref[...], h)
    pieces = []
    for g in range(QK_WIDTH // HEAD_DIM):
        lo = g * HEAD_DIM
        pieces.append(qt[lo + HALF_HEAD:lo + HEAD_DIM, :])
        pieces.append(qt[lo:lo + HALF_HEAD, :])
    partner = jnp.concatenate(pieces, axis=0)
    reps_t = QK_WIDTH // cost_ref.shape[0]
    qt = qt * jnp.tile(cost_ref[...], (reps_t, 1)) + partner * jnp.tile(sint_ref[...], (reps_t, 1))
    qt_ref[...] = (qt * (HEAD_DIM ** -0.5 * LOG2E)).astype(BF16)

    kk = _dot(h, wk_ref[...])
    reps = QK_WIDTH // cos_ref.shape[1]
    lane = lax.broadcasted_iota(jnp.int32, (tm, QK_WIDTH), 1)
    partner = jnp.where((lane % HEAD_DIM) < HALF_HEAD,
                        pltpu.roll(kk, QK_WIDTH - HALF_HEAD, 1),
                        pltpu.roll(kk, HALF_HEAD, 1))
    kk = kk * jnp.tile(cos_ref[...], (1, reps)) + partner * jnp.tile(sin_ref[...], (1, reps))
    k_ref[...] = kk.astype(BF16)

    vt = _dot_nt(wvt_ref[...], h).astype(BF16)
    chunk = vt_ref.shape[2]
    for j in range(vt_ref.shape[0]):
        vt_ref[j] = vt[:, j * chunk:(j + 1) * chunk]

    def proj(lo, width):
        return _dot(h, wr_ref[:, lo:lo + width])

    u_ref[...] = proj(0, c) * jax.nn.sigmoid(proj(c, c))
    ga_ref[...] = jax.nn.sigmoid(proj(2 * c, d_model)).astype(BF16)
    gc_ref[...] = jax.nn.sigmoid(proj(2 * c + d_model, d_model)).astype(BF16)


def _inproj(x2, g, b, wqt, wk, wvt, wr, tables, tm, chunk):
    t, d = x2.shape
    c = d // 2
    cos, sin, cost, sint = tables
    n_pos_tiles = cos.shape[0] // tm
    row = lambda i: (i, 0)
    col = lambda i: (0, i)
    const = lambda i: (0, 0)
    pos = lambda i: (i % n_pos_tiles, 0)
    post = lambda i: (0, i % n_pos_tiles)
    full = lambda a: pl.BlockSpec(a.shape, const)
    return pl.pallas_call(
        functools.partial(_inproj_kernel, d_model=d),
        grid=(t // tm,),
        in_specs=[
            pl.BlockSpec((tm, d), row), full(g), full(b),
            full(wqt), full(wk), full(wvt), full(wr),
            pl.BlockSpec((tm, cos.shape[1]), pos),
            pl.BlockSpec((tm, cos.shape[1]), pos),
            pl.BlockSpec((cost.shape[0], tm), post),
            pl.BlockSpec((cost.shape[0], tm), post),
        ],
        out_specs=[
            pl.BlockSpec((QK_WIDTH, tm), col),
            pl.BlockSpec((tm, QK_WIDTH), row),
            pl.BlockSpec((tm // chunk, V_WIDTH, chunk), lambda i: (i, 0, 0)),
            pl.BlockSpec((tm, c), row),
            pl.BlockSpec((tm, d), row),
            pl.BlockSpec((tm, d), row),
        ],
        out_shape=[
            jax.ShapeDtypeStruct((QK_WIDTH, t), BF16),
            jax.ShapeDtypeStruct((t, QK_WIDTH), BF16),
            jax.ShapeDtypeStruct((t // chunk, V_WIDTH, chunk), BF16),
            jax.ShapeDtypeStruct((t, c), F32),
            jax.ShapeDtypeStruct((t, d), BF16),
            jax.ShapeDtypeStruct((t, d), BF16),
        ],
        compiler_params=pltpu.CompilerParams(
            dimension_semantics=("arbitrary",), vmem_limit_bytes=VMEM_LIMIT),
        name="inproj",
    )(x2, g, b, wqt, wk, wvt, wr, cos, sin, cost, sint)


def _attn_kernel(lam_ref, qt_ref, k_ref, vt_ref, km_ref, vmt_ref, g_ref, o_ref, acc_sc, *, tq, tk):
    i = pl.program_id(2)
    qt = qt_ref[...]
    feat = lax.broadcasted_iota(jnp.int32, qt.shape, 0)
    zero = jnp.zeros_like(qt)
    q2 = jnp.concatenate([jnp.where(feat < HEAD_DIM, qt, zero),
                          jnp.where(feat >= HEAD_DIM, qt, zero)], axis=1)

    def scores(j):
        r0 = pl.multiple_of(j * tk, tk)
        return _dot(k_ref[pl.ds(r0, tk), :], q2)

    def absorb(m_old, l_old, s, vtc):
        m_new = jnp.maximum(m_old, jnp.max(s, axis=0, keepdims=True))
        alpha = jnp.exp2(m_old - m_new)
        p = jnp.exp2(s - m_new)
        acc_sc[...] = alpha * acc_sc[...] + _dot(vtc, p.astype(BF16))
        return m_new, alpha * l_old + jnp.sum(p, axis=0, keepdims=True)

    s = _dot(km_ref[...], q2)
    m = jnp.max(s, axis=0, keepdims=True)
    p = jnp.exp2(s - m)
    acc_sc[...] = _dot(vmt_ref[...], p.astype(BF16))
    l = jnp.sum(p, axis=0, keepdims=True)

    per_tile = tq // tk
    n_full = i * per_tile

    def body(j, carry):
        m_old, l_old, s_cur = carry
        s_next = scores(j + 1)
        m_new, l_new = absorb(m_old, l_old, s_cur, vt_ref[j])
        return m_new, l_new, s_next

    m, l, s = lax.fori_loop(0, n_full, body, (m, l, scores(0)))

    key = lax.broadcasted_iota(jnp.int32, (tk, 2 * tq), 0)
    qry = lax.broadcasted_iota(jnp.int32, (tk, 2 * tq), 1) % tq
    for d in range(per_tile):
        s_cur = s
        if d + 1 < per_tile:
            s = scores(n_full + d + 1)
        m, l = absorb(m, l, jnp.where(key + d * tk <= qry, s_cur, NEG_INF), vt_ref[n_full + d])

    lam = lam_ref[0, 0]
    o = acc_sc[:, :tq] / l[:, :tq] - lam * (acc_sc[:, tq:] / l[:, tq:])
    o = o * lax.rsqrt(jnp.mean(o * o, axis=0, keepdims=True) + LN_EPS) * g_ref[...]
    o_ref[...] = (o * (1.0 - LAM_INIT)).T.astype(o_ref.dtype)


def _attention(lam, qt, k, vt, km, vmt, subln_col, bsz, seq, tq, tk):
    t = k.shape[0]
    nq = seq // tq
    nk = seq // tk
    return pl.pallas_call(
        functools.partial(_attn_kernel, tq=tq, tk=tk),
        grid=(bsz, HEADS, nq),
        in_specs=[
            pl.BlockSpec(memory_space=pltpu.SMEM),
            pl.BlockSpec((V_DIM, tq), lambda b, h, i: (h, b * nq + i)),
            pl.BlockSpec((seq, V_DIM), lambda b, h, i: (b, h)),
            pl.BlockSpec((nk, V_DIM, tk), lambda b, h, i: (b, h, 0)),
            pl.BlockSpec((N_META, V_DIM), lambda b, h, i: (0, h)),
            pl.BlockSpec((V_DIM, N_META), lambda b, h, i: (h, 0)),
            pl.BlockSpec((V_DIM, 1), lambda b, h, i: (0, 0)),
        ],
        out_specs=pl.BlockSpec((tq, V_DIM), lambda b, h, i: (b * nq + i, h)),
        out_shape=jax.ShapeDtypeStruct((t, V_WIDTH), BF16),
        scratch_shapes=[pltpu.VMEM((V_DIM, 2 * tq), F32)],
        compiler_params=pltpu.CompilerParams(
            dimension_semantics=("arbitrary", "arbitrary", "arbitrary"),
            vmem_limit_bytes=VMEM_LIMIT),
        name="diff_attention",
    )(lam, qt, k, vt, km, vmt, subln_col)


CONV_ROWS = 32
CONV_WINDOW = 64
CONV_PAD = 32


def _conv_kernel(u_ref, um_ref, w_ref, cb_ref, g_ref, b_ref, z_ref, ucat, grp, *, seq):
    c = u_ref.shape[1]
    ucat[0:CONV_PAD - N_META, :] = jnp.zeros((CONV_PAD - N_META, c), F32)
    ucat[CONV_PAD - N_META:CONV_PAD, :] = um_ref[...]
    ucat[CONV_PAD:CONV_PAD + seq, :] = u_ref[...]
    shift = CONV_PAD - (CONV_WIDTH - 1)

    def body(t, carry):
        r0 = pl.multiple_of(t * CONV_ROWS, CONV_ROWS)
        acc = jnp.zeros((CONV_ROWS, c), F32)
        for r in range(SUBLANES):
            taps = [j for j in range(CONV_WIDTH) if (shift + j) % SUBLANES == r]
            if not taps:
                continue
            lo = shift + taps[0]
            rows = taps[-1] - taps[0] + CONV_ROWS
            start = pl.multiple_of(r0 + lo - r, SUBLANES)
            span = rows + (SUBLANES if r else 0)
            grp[r, 0:rows, :] = ucat[pl.ds(start, span), :][r:r + rows, :]
            for j in taps:
                off = shift + j - lo
                acc = acc + grp[r, off:off + CONV_ROWS, :] * w_ref[j:j + 1, :]
        y = _ln(acc + cb_ref[...], g_ref[...], b_ref[...])
        z_ref[pl.ds(r0, CONV_ROWS), :] = _silu(y).astype(z_ref.dtype)
        return carry

    lax.fori_loop(0, seq // CONV_ROWS, body, 0)


def _conv(u, um, conv_w, conv_b, ln_g, ln_b, bsz, seq):
    t, c = u.shape
    const = lambda b: (0, 0)
    return pl.pallas_call(
        functools.partial(_conv_kernel, seq=seq),
        grid=(bsz,),
        in_specs=[
            pl.BlockSpec((seq, c), lambda b: (b, 0)),
            pl.BlockSpec((N_META, c), const),
            pl.BlockSpec((CONV_WIDTH, c), const),
            pl.BlockSpec((1, c), const),
            pl.BlockSpec((1, c), const),
            pl.BlockSpec((1, c), const),
        ],
        out_specs=pl.BlockSpec((seq, c), lambda b: (b, 0)),
        out_shape=jax.ShapeDtypeStruct((t, c), BF16),
        scratch_shapes=[pltpu.VMEM((CONV_PAD + seq, c), F32),
                        pltpu.VMEM((SUBLANES, CONV_WINDOW, c), F32)],
        compiler_params=pltpu.CompilerParams(
            dimension_semantics=("arbitrary",), vmem_limit_bytes=VMEM_LIMIT),
        name="conformer_conv",
    )(u, um, conv_w, conv_b, ln_g, ln_b)


def _merge_route_kernel(x_ref, lg_ref, lb_ref, oa_ref, z_ref, ga_ref, gc_ref,
                        wao_ref, wco_ref, bco_ref, wout_ref, g1_ref, b1_ref,
                        wrt_ref, rb_ref,
                        h1_ref, h1t_ref, te_ref, gw_ref, rk_ref, cnt_ref, carry_sc):
    tm = x_ref.shape[0]

    @pl.when(pl.program_id(0) == 0)
    def _():
        carry_sc[...] = jnp.zeros(carry_sc.shape, F32)

    h0 = _ln(x_ref[...], lg_ref[...], lb_ref[...])
    y_attn = _dot(oa_ref[...], wao_ref[...])
    y_conv = _dot(z_ref[...], wco_ref[...]) + bco_ref[...]
    merged = ga_ref[...].astype(F32) * y_attn + gc_ref[...].astype(F32) * y_conv
    h1 = _ln(ALPHA * h0 + _dot(merged.astype(BF16), wout_ref[...]), g1_ref[...], b1_ref[...])
    h1_ref[...] = h1
    _rows_to_tiles(h1t_ref, h1)
    h1b = h1.astype(BF16)

    scores = jax.nn.sigmoid(_dot_nt(wrt_ref[...], h1b))
    sel = scores + rb_ref[...]

    sub = lax.broadcasted_iota(jnp.int32, (GROUP_SIZE, tm), 0)
    gscore = []
    for g in range(N_GROUPS):
        slab = sel[g * GROUP_SIZE:(g + 1) * GROUP_SIZE, :]
        top1 = jnp.max(slab, axis=0, keepdims=True)
        arg1 = jnp.min(jnp.where(slab == top1, sub, GROUP_SIZE), axis=0, keepdims=True)
        top2 = jnp.max(jnp.where(sub == arg1, NEG_INF, slab), axis=0, keepdims=True)
        gscore.append(top1 + top2)

    eiota = lax.broadcasted_iota(jnp.int32, (N_EXPERTS, tm), 0)
    egroup = eiota // GROUP_SIZE
    allowed = jnp.zeros((N_EXPERTS, tm), F32)
    for _ in range(TOPK_GROUPS):
        best = functools.reduce(jnp.maximum, gscore)
        pick = jnp.full((1, tm), N_GROUPS, jnp.int32)
        for g in reversed(range(N_GROUPS)):
            pick = jnp.where(gscore[g] == best, g, pick)
        gscore = [jnp.where(pick == g, NEG_INF, gscore[g]) for g in range(N_GROUPS)]
        allowed = jnp.where(egroup == pick, 1.0, allowed)

    masked = jnp.where(allowed > 0.5, sel, NEG_INF)

    picks = []
    weights = []
    chosen = jnp.zeros((N_EXPERTS, tm), F32)
    for _ in range(TOP_K):
        best = jnp.max(masked, axis=0, keepdims=True)
        pick = jnp.min(jnp.where(masked == best, eiota, N_EXPERTS), axis=0, keepdims=True)
        hit = eiota == pick
        picks.append(pick)
        weights.append(jnp.sum(jnp.where(hit, scores, 0.0), axis=0, keepdims=True))
        chosen = jnp.where(hit, 1.0, chosen)
        masked = jnp.where(hit, NEG_INF, masked)

    gw = jnp.concatenate(weights, axis=0)
    gw_ref[...] = gw / jnp.sum(gw, axis=0, keepdims=True) * ROUTED_SCALE
    te_ref[...] = jnp.concatenate(picks, axis=0)

    before = (lax.broadcasted_iota(jnp.int32, (tm, tm), 0)
              < lax.broadcasted_iota(jnp.int32, (tm, tm), 1))
    prior = _dot(chosen.astype(BF16), jnp.where(before, 1.0, 0.0).astype(BF16)) + carry_sc[...]
    ranks = [jnp.sum(jnp.where(eiota == p, prior, 0.0), axis=0, keepdims=True) for p in picks]
    rk_ref[...] = jnp.concatenate(ranks, axis=0).astype(jnp.int32)
    carry_sc[...] = carry_sc[...] + jnp.sum(chosen, axis=1, keepdims=True)
    cnt_ref[...] = carry_sc[...]


def _merge_route(x2, lg, lb, oa, z, ga, gc, wao, wco, bco, wout, g1, b1, wrt, rb, tm):
    t, d = x2.shape
    c = d // 2
    row = lambda i: (i, 0)
    col = lambda i: (0, i)
    const = lambda i: (0, 0)
    full = lambda a: pl.BlockSpec(a.shape, const)
    return pl.pallas_call(
        _merge_route_kernel,
        grid=(t // tm,),
        in_specs=[
            pl.BlockSpec((tm, d), row), full(lg), full(lb),
            pl.BlockSpec((tm, V_WIDTH), row), pl.BlockSpec((tm, c), row),
            pl.BlockSpec((tm, d), row), pl.BlockSpec((tm, d), row),
            full(wao), full(wco), full(bco), full(wout), full(g1), full(b1),
            full(wrt), full(rb),
        ],
        out_specs=[
            pl.BlockSpec((tm, d), row),
            pl.BlockSpec((tm * SUBLANES, LANES), row),
            pl.BlockSpec((TOP_K, tm), col),
            pl.BlockSpec((TOP_K, tm), col),
            pl.BlockSpec((TOP_K, tm), col),
            pl.BlockSpec((N_EXPERTS, 1), const),
        ],
        out_shape=[
            jax.ShapeDtypeStruct((t, d), F32),
            jax.ShapeDtypeStruct((t * SUBLANES, LANES), F32),
            jax.ShapeDtypeStruct((TOP_K, t), jnp.int32),
            jax.ShapeDtypeStruct((TOP_K, t), F32),
            jax.ShapeDtypeStruct((TOP_K, t), jnp.int32),
            jax.ShapeDtypeStruct((N_EXPERTS, 1), F32),
        ],
        scratch_shapes=[pltpu.VMEM((N_EXPERTS, 1), F32)],
        compiler_params=pltpu.CompilerParams(
            dimension_semantics=("arbitrary",), vmem_limit_bytes=VMEM_LIMIT),
        name="merge_route",
    )(x2, lg, lb, oa, z, ga, gc, wao, wco, bco, wout, g1, b1, wrt, rb)


def _dest_kernel(te_ref, rk_ref, start_ref, d_ref):
    tm = te_ref.shape[1]
    eiota = lax.broadcasted_iota(jnp.int32, (N_EXPERTS, tm), 0)
    start = start_ref[...]
    rows = [jnp.sum(jnp.where(eiota == te_ref[k:k + 1, :], start, 0.0), axis=0, keepdims=True)
            for k in range(TOP_K)]
    d_ref[...] = (jnp.concatenate(rows, axis=0).astype(jnp.int32) + rk_ref[...]) * SUBLANES


def _dest(top_e, rank, start_col, tm):
    t = top_e.shape[1]
    col = lambda i: (0, i)
    return pl.pallas_call(
        _dest_kernel,
        grid=(t // tm,),
        in_specs=[
            pl.BlockSpec((TOP_K, tm), col),
            pl.BlockSpec((TOP_K, tm), col),
            pl.BlockSpec((N_EXPERTS, 1), lambda i: (0, 0)),
        ],
        out_specs=pl.BlockSpec((TOP_K, tm), col),
        out_shape=jax.ShapeDtypeStruct((TOP_K, t), jnp.int32),
        compiler_params=pltpu.CompilerParams(dimension_semantics=("arbitrary",)),
        name="moe_dest",
    )(top_e, rank, start_col)


def _dispatch_kernel(dest_ref, h_ref, xs_ref, sem, *, n_tok):
    tm = h_ref.shape[0] // SUBLANES
    base = pl.program_id(0) * tm
    for r in range(tm):
        for k in range(TOP_K):
            slot = pl.multiple_of(dest_ref[k * n_tok + base + r], SUBLANES)
            pltpu.make_async_copy(h_ref.at[pl.ds(r * SUBLANES, SUBLANES)],
                                  xs_ref.at[pl.ds(slot, SUBLANES)], sem).start(priority=k % 2)
    for k in range(TOP_K):
        pltpu.make_async_copy(h_ref, xs_ref.at[pl.ds(0, tm * SUBLANES)], sem).wait()


def _dispatch(dest_flat, h_tiles, tm):
    t = h_tiles.shape[0] // SUBLANES
    return pl.pallas_call(
        functools.partial(_dispatch_kernel, n_tok=t),
        grid_spec=pltpu.PrefetchScalarGridSpec(
            num_scalar_prefetch=1,
            grid=(t // tm,),
            in_specs=[pl.BlockSpec((tm * SUBLANES, LANES), lambda i, dest: (i, 0))],
            out_specs=pl.BlockSpec(memory_space=pl.ANY),
            scratch_shapes=[pltpu.SemaphoreType.DMA],
        ),
        out_shape=jax.ShapeDtypeStruct((t * TOP_K * SUBLANES, LANES), h_tiles.dtype),
        compiler_params=pltpu.CompilerParams(
            dimension_semantics=("arbitrary",), has_side_effects=True),
        name="moe_dispatch",
    )(dest_flat, h_tiles)


def _experts_kernel(blk_ref, exp_ref, lo_ref, hi_ref, first_ref,
                    x_ref, wg_ref, wu_ref, wd_ref, y_ref):
    w = pl.program_id(0)
    lo = lo_ref[w]
    hi = hi_ref[w]

    @pl.when(hi > lo)
    def _():
        tm = x_ref.shape[0] // SUBLANES
        x = _rows_from_tiles(x_ref, tm).astype(BF16)
        a = _silu(_dot(x, wg_ref[...].astype(BF16))) * _dot(x, wu_ref[...].astype(BF16))
        y = _dot(a.astype(BF16), wd_ref[...].astype(BF16))
        rows = lax.broadcasted_iota(jnp.int32, (tm, 1), 0)
        owned = jnp.logical_and(rows >= lo, rows < hi)

        @pl.when(first_ref[w] == 1)
        def _():
            _rows_to_tiles(y_ref, y)

        @pl.when(first_ref[w] == 0)
        def _():
            _rows_to_tiles(y_ref, jnp.where(owned, y, _rows_from_tiles(y_ref, tm)))


def _experts(items, xs, wg, wu, wd, tm):
    d = SUBLANES * LANES
    hid = wg.shape[2]
    n_items = items[0].shape[0]
    xmap = lambda w, blk, exp, lo, hi, first: (blk[w], 0)
    wmap = lambda w, blk, exp, lo, hi, first: (exp[w], 0, 0)
    return pl.pallas_call(
        _experts_kernel,
        grid_spec=pltpu.PrefetchScalarGridSpec(
            num_scalar_prefetch=5,
            grid=(n_items,),
            in_specs=[
                pl.BlockSpec((tm * SUBLANES, LANES), xmap),
                pl.BlockSpec((None, d, hid), wmap),
                pl.BlockSpec((None, d, hid), wmap),
                pl.BlockSpec((None, hid, d), wmap),
            ],
            out_specs=pl.BlockSpec((tm * SUBLANES, LANES), xmap),
        ),
        out_shape=jax.ShapeDtypeStruct(xs.shape, xs.dtype),
        compiler_params=pltpu.CompilerParams(
            dimension_semantics=("arbitrary",), vmem_limit_bytes=VMEM_LIMIT),
        name="moe_experts",
    )(*items, xs, wg, wu, wd)


def _combine_kernel(dest_ref, h_ref, gw_ref, ys_ref, wgs_ref, wus_ref, wds_ref, g2_ref, b2_ref,
                    o_ref, buf, sem, *, n_tok):
    tm = h_ref.shape[0]
    base = pl.program_id(0) * tm
    for r in range(tm):
        for k in range(TOP_K):
            slot = pl.multiple_of(dest_ref[k * n_tok + base + r], SUBLANES)
            pltpu.make_async_copy(ys_ref.at[pl.ds(slot, SUBLANES)],
                                  buf.at[k, pl.ds(r * SUBLANES, SUBLANES)],
                                  sem).start(priority=k % 2)

    h = h_ref[...]
    hb = h.astype(BF16)
    shared = _dot((_silu(_dot(hb, wgs_ref[...])) * _dot(hb, wus_ref[...])).astype(BF16),
                  wds_ref[...])
    acc = ALPHA * h + shared

    for k in range(TOP_K):
        pltpu.make_async_copy(ys_ref.at[pl.ds(0, tm * SUBLANES)], buf.at[k], sem).wait()
    gw = gw_ref[...]
    for k in range(TOP_K):
        acc = acc + _rows_from_tiles(buf.at[k], tm) * gw[:, k:k + 1]
    o_ref[...] = _ln(acc, g2_ref[...], b2_ref[...])


def _combine(dest_flat, h1, gw, ys, wgs, wus, wds, g2, b2, tm):
    t, d = h1.shape
    row = lambda i, dest: (i, 0)
    const = lambda i, dest: (0, 0)
    full = lambda a: pl.BlockSpec(a.shape, const)
    return pl.pallas_call(
        functools.partial(_combine_kernel, n_tok=t),
        grid_spec=pltpu.PrefetchScalarGridSpec(
            num_scalar_prefetch=1,
            grid=(t // tm,),
            in_specs=[
                pl.BlockSpec((tm, d), row),
                pl.BlockSpec((tm, TOP_K), row),
                pl.BlockSpec(memory_space=pl.ANY),
                full(wgs), full(wus), full(wds), full(g2), full(b2),
            ],
            out_specs=pl.BlockSpec((tm, d), row),
            scratch_shapes=[pltpu.VMEM((TOP_K, tm * SUBLANES, LANES), ys.dtype),
                            pltpu.SemaphoreType.DMA],
        ),
        out_shape=jax.ShapeDtypeStruct((t, d), F32),
        compiler_params=pltpu.CompilerParams(
            dimension_semantics=("arbitrary",), vmem_limit_bytes=VMEM_LIMIT),
        name="moe_combine",
    )(dest_flat, h1, gw, ys, wgs, wus, wds, g2, b2)


def _work_items(counts, n_rows, tm):
    n_blocks = n_rows // tm
    n_items = n_blocks + N_EXPERTS - 1
    start = jnp.cumsum(counts) - counts
    end = start + counts
    first_blk = start // tm
    last_blk = jnp.maximum(end - 1, start) // tm
    per_expert = jnp.where(counts > 0, last_blk - first_blk + 1, 0)
    item_end = jnp.cumsum(per_expert)
    item_start = item_end - per_expert
    total = item_end[-1]
    w = jnp.arange(n_items, dtype=jnp.int32)
    wc = jnp.minimum(w, total - 1)
    exp = jnp.sum((item_end[None, :] <= wc[:, None]).astype(jnp.int32), axis=1)
    blk = (first_blk[exp] + wc - item_start[exp]).astype(jnp.int32)
    lo = jnp.maximum(start[exp], blk * tm) - blk * tm
    hi = jnp.minimum(end[exp], (blk + 1) * tm) - blk * tm
    valid = w < total
    lo = jnp.where(valid, lo, 0).astype(jnp.int32)
    hi = jnp.where(valid, hi, 0).astype(jnp.int32)
    prev_blk = jnp.concatenate([jnp.full((1,), -1, jnp.int32), blk[:-1]])
    first = jnp.logical_and(valid, blk != prev_blk).astype(jnp.int32)
    return blk, exp, lo, hi, first, start


def _rope_tables(positions):
    inv = 1.0 / (ROPE_THETA ** (jnp.arange(0, HEAD_DIM, 2, dtype=F32) / HEAD_DIM))
    ang = positions.astype(F32)[:, None] * inv[None, :]
    cos = jnp.cos(ang)
    sin = jnp.sin(ang)
    cos = jnp.concatenate([cos, cos], axis=-1)
    sin = jnp.concatenate([-sin, sin], axis=-1)
    reps = V_DIM // HEAD_DIM
    return jnp.tile(cos, (1, reps)), jnp.tile(sin, (1, reps)), cos.T, sin.T


def _tile(n, pref):
    return pref if n % pref == 0 else n


def kernel(x, meta_tokens, ln_in_g, ln_in_b, w_in, lambda_q1, lambda_k1, lambda_q2, lambda_k2, subln_g, w_attn_o, conv_w, conv_b, conv_ln_g, conv_ln_b, w_conv_o, b_conv_o, w_out, ln1_g, ln1_b, w_router, router_bias, w_gate_e, w_up_e, w_down_e, w_gate_s, w_up_s, w_down_s, ln2_g, ln2_b):
    bsz, seq, d = x.shape
    t = bsz * seq
    l = 0
    row = lambda a: a.reshape(1, -1).astype(F32)
    x2 = x.reshape(t, d)
    lg, lb = row(ln_in_g), row(ln_in_b)
    w_in_bf = w_in[l].astype(BF16)
    o_v = 2 * QK_WIDTH
    wqt = w_in_bf[:, :QK_WIDTH].T
    wk = w_in_bf[:, QK_WIDTH:o_v]
    wvt = w_in_bf[:, o_v:o_v + V_WIDTH].T
    wr = w_in_bf[:, o_v + V_WIDTH:]

    tables_x = _rope_tables(jnp.arange(N_META, N_META + seq))
    tables_m = _rope_tables(jnp.arange(N_META))

    tq = _tile(seq, 512)
    tk = _tile(seq, 256)
    qt, k, vt, u, ga, gc = _inproj(x2, lg, lb, wqt, wk, wvt, wr, tables_x, tq, tk)
    _, km, vmt, um, _, _ = _inproj(meta_tokens.astype(F32), lg, lb, wqt, wk, wvt, wr, tables_m,
                                   N_META, N_META)

    lam = (jnp.exp(jnp.sum(lambda_q1[l].astype(F32) * lambda_k1[l].astype(F32)))
           - jnp.exp(jnp.sum(lambda_q2[l].astype(F32) * lambda_k2[l].astype(F32)))
           + LAM_INIT).reshape(1, 1)
    o_attn = _attention(lam, qt, k, vt, km, vmt.reshape(V_WIDTH, N_META),
                        subln_g[l].reshape(-1, 1).astype(F32), bsz, seq, tq, tk)

    z = _conv(u, um, conv_w[l].astype(F32), row(conv_b[l]), row(conv_ln_g[l]),
              row(conv_ln_b[l]), bsz, seq)

    assert d == SUBLANES * LANES, "the row-tile layout assumes one (8, 128) tile per token row"
    h1, h1_tiles, top_e, gw_t, rank, counts = _merge_route(
        x2, lg, lb, o_attn, z, ga, gc,
        w_attn_o[l].astype(BF16), w_conv_o[l].astype(BF16), row(b_conv_o[l]),
        w_out[l].astype(BF16), row(ln1_g[l]), row(ln1_b[l]),
        w_router[l].T.astype(BF16), router_bias[l].reshape(-1, 1).astype(F32),
        _tile(t, 256))

    tm_e = 256
    counts = counts.reshape(-1).astype(jnp.int32)
    blk, exp, lo, hi, first, start = _work_items(counts, t * TOP_K, tm_e)
    dest = _dest(top_e, rank, start.astype(F32).reshape(-1, 1), _tile(t, 512)).reshape(-1)

    xs = _dispatch(dest, h1_tiles, _tile(t, 128))
    ys = _experts((blk, exp, lo, hi, first), xs, w_gate_e[l], w_up_e[l], w_down_e[l], tm_e)
    out = _combine(dest, h1, gw_t.T, ys,
                   w_gate_s[l].astype(BF16), w_up_s[l].astype(BF16), w_down_s[l].astype(BF16),
                   row(ln2_g[l]), row(ln2_b[l]), _tile(t, 128))
    return out.reshape(bsz, seq, d)
```

```python
import functools
import math

import jax
import jax.numpy as jnp
from jax import lax
from jax.experimental import pallas as pl
from jax.experimental.pallas import tpu as pltpu

N_META = 16
HEADS = 4
HEAD_DIM = 64
V_DIM = 2 * HEAD_DIM
QK_WIDTH = HEADS * 2 * HEAD_DIM
V_WIDTH = HEADS * V_DIM
ROPE_THETA = 10000.0
CONV_WIDTH = 31
N_EXPERTS = 256
TOP_K = 8
N_GROUPS = 8
TOPK_GROUPS = 4
GROUP_SIZE = N_EXPERTS // N_GROUPS
ROUTED_SCALE = 2.5
LN_EPS = 1e-5
DEPTH = 1
ALPHA = (2.0 * DEPTH) ** 0.25
LAM_INIT = 0.8 - 0.6 * math.exp(-0.3 * 0)

F32 = jnp.float32
BF16 = jnp.bfloat16
NEG_INF = float("-inf")

VMEM_LIMIT = 56 * 1024 * 1024


def _ln(x, g, b):
    mu = jnp.mean(x, axis=-1, keepdims=True)
    xc = x - mu
    var = jnp.mean(xc * xc, axis=-1, keepdims=True)
    return xc * lax.rsqrt(var + LN_EPS) * g + b


def _dot(a, b):
    return jnp.dot(a, b, preferred_element_type=F32)


def _dot_nt(a, b):
    return lax.dot_general(a, b, (((1,), (1,)), ((), ())), preferred_element_type=F32)


def _silu(x):
    return x * jax.nn.sigmoid(x)


SUBLANES = 8
LANES = 128


def _rows_from_tiles(ref, n_rows, first=0):
    return jnp.concatenate(
        [ref[pl.ds(first * SUBLANES + c, n_rows, stride=SUBLANES), :] for c in range(SUBLANES)],
        axis=1)


def _rows_to_tiles(ref, x):
    for c in range(SUBLANES):
        ref[pl.ds(c, x.shape[0], stride=SUBLANES), :] = x[:, c * LANES:(c + 1) * LANES]


LOG2E = math.log2(math.e)
HALF_HEAD = HEAD_DIM // 2


def _inproj_kernel(x_ref, g_ref, b_ref, wqt_ref, wk_ref, wvt_ref, wr_ref,
                   cos_ref, sin_ref, cost_ref, sint_ref,
                   qt_ref, k_ref, vt_ref, u_ref, ga_ref, gc_ref, *, d_model):
    h = _ln(x_ref[...], g_ref[...], b_ref[...]).astype(BF16)
    tm = x_ref.shape[0]
    c = d_model // 2

    qt = _dot_nt(wqt_ref[...], h)
    pieces = []
    for g in range(QK_WIDTH // HEAD_DIM):
        lo = g * HEAD_DIM
        pieces.append(qt[lo + HALF_HEAD:lo + HEAD_DIM, :])
        pieces.append(qt[lo:lo + HALF_HEAD, :])
    partner = jnp.concatenate(pieces, axis=0)
    reps_t = QK_WIDTH // cost_ref.shape[0]
    qt = qt * jnp.tile(cost_ref[...], (reps_t, 1)) + partner * jnp.tile(sint_ref[...], (reps_t, 1))
    qt_ref[...] = (qt * (HEAD_DIM ** -0.5 * LOG2E)).astype(BF16)

    kk = _dot(h, wk_ref[...])
    reps = QK_WIDTH // cos_ref.shape[1]
    lane = lax.broadcasted_iota(jnp.int32, (tm, QK_WIDTH), 1)
    partner = jnp.where((lane % HEAD_DIM) < HALF_HEAD,
                        pltpu.roll(kk, QK_WIDTH - HALF_HEAD, 1),
                        pltpu.roll(kk, HALF_HEAD, 1))
    kk = kk * jnp.tile(cos_ref[...], (1, reps)) + partner * jnp.tile(sin_ref[...], (1, reps))
    k_ref[...] = kk.astype(BF16)

    vt = _dot_nt(wvt_ref[...], h).astype(BF16)
    chunk = vt_ref.shape[2]
    for j in range(vt_ref.shape[0]):
        vt_ref[j] = vt[:, j * chunk:(j + 1) * chunk]

    def proj(lo, width):
        return _dot(h, wr_ref[:, lo:lo + width])

    u_ref[...] = proj(0, c) * jax.nn.sigmoid(proj(c, c))
    ga_ref[...] = jax.nn.sigmoid(proj(2 * c, d_model)).astype(BF16)
    gc_ref[...] = jax.nn.sigmoid(proj(2 * c + d_model, d_model)).astype(BF16)


def _inproj(x2, g, b, wqt, wk, wvt, wr, tables, tm, chunk):
    t, d = x2.shape
    c = d // 2
    cos, sin, cost, sint = tables
    n_pos_tiles = cos.shape[0] // tm
    row = lambda i: (i, 0)
    col = lambda i: (0, i)
    const = lambda i: (0, 0)
    pos = lambda i: (i % n_pos_tiles, 0)
    post = lambda i: (0, i % n_pos_tiles)
    full = lambda a: pl.BlockSpec(a.shape, const)
    return pl.pallas_call(
        functools.partial(_inproj_kernel, d_model=d),
        grid=(t // tm,),
        in_specs=[
            pl.BlockSpec((tm, d), row), full(g), full(b),
            full(wqt), full(wk), full(wvt), full(wr),
            pl.BlockSpec((tm, cos.shape[1]), pos),
            pl.BlockSpec((tm, cos.shape[1]), pos),
            pl.BlockSpec((cost.shape[0], tm), post),
            pl.BlockSpec((cost.shape[0], tm), post),
        ],
        out_specs=[
            pl.BlockSpec((QK_WIDTH, tm), col),
            pl.BlockSpec((tm, QK_WIDTH), row),
            pl.BlockSpec((tm // chunk, V_WIDTH, chunk), lambda i: (i, 0, 0)),
            pl.BlockSpec((tm, c), row),
            pl.BlockSpec((tm, d), row),
            pl.BlockSpec((tm, d), row),
        ],
        out_shape=[
            jax.ShapeDtypeStruct((QK_WIDTH, t), BF16),
            jax.ShapeDtypeStruct((t, QK_WIDTH), BF16),
            jax.ShapeDtypeStruct((t // chunk, V_WIDTH, chunk), BF16),
            jax.ShapeDtypeStruct((t, c), F32),
            jax.ShapeDtypeStruct((t, d), BF16),
            jax.ShapeDtypeStruct((t, d), BF16),
        ],
        compiler_params=pltpu.CompilerParams(
            dimension_semantics=("arbitrary",), vmem_limit_bytes=VMEM_LIMIT),
        name="inproj",
    )(x2, g, b, wqt, wk, wvt, wr, cos, sin, cost, sint)


def _attn_kernel(lam_ref, qt_ref, k_ref, vt_ref, km_ref, vmt_ref, g_ref, o_ref, acc_sc, *, tq, tk):
    i = pl.program_id(2)
    qt = qt_ref[...]
    feat = lax.broadcasted_iota(jnp.int32, qt.shape, 0)
    zero = jnp.zeros_like(qt)
    q2 = jnp.concatenate([jnp.where(feat < HEAD_DIM, qt, zero),
                          jnp.where(feat >= HEAD_DIM, qt, zero)], axis=1)

    def scores(j):
        r0 = pl.multiple_of(j * tk, tk)
        return _dot(k_ref[pl.ds(r0, tk), :], q2)

    def absorb(m_old, l_old, s, vtc):
        m_new = jnp.maximum(m_old, jnp.max(s, axis=0, keepdims=True))
        alpha = jnp.exp2(m_old - m_new)
        p = jnp.exp2(s - m_new)
        acc_sc[...] = alpha * acc_sc[...] + _dot(vtc, p.astype(BF16))
        return m_new, alpha * l_old + jnp.sum(p, axis=0, keepdims=True)

    s = _dot(km_ref[...], q2)
    m = jnp.max(s, axis=0, keepdims=True)
    p = jnp.exp2(s - m)
    acc_sc[...] = _dot(vmt_ref[...], p.astype(BF16))
    l = jnp.sum(p, axis=0, keepdims=True)

    per_tile = tq // tk
    n_full = i * per_tile

    def body(j, carry):
        m_old, l_old, s_cur = carry
        s_next = scores(j + 1)
        m_new, l_new = absorb(m_old, l_old, s_cur, vt_ref[j])
        return m_new, l_new, s_next

    m, l, s = lax.fori_loop(0, n_full, body, (m, l, scores(0)))

    key = lax.broadcasted_iota(jnp.int32, (tk, 2 * tq), 0)
    qry = lax.broadcasted_iota(jnp.int32, (tk, 2 * tq), 1) % tq
    for d in range(per_tile):
        s_cur = s
        if d + 1 < per_tile:
            s = scores(n_full + d + 1)
        m, l = absorb(m, l, jnp.where(key + d * tk <= qry, s_cur, NEG_INF), vt_ref[n_full + d])

    lam = lam_ref[0, 0]
    o = acc_sc[:, :tq] / l[:, :tq] - lam * (acc_sc[:, tq:] / l[:, tq:])
    o = o * lax.rsqrt(jnp.mean(o * o, axis=0, keepdims=True) + LN_EPS) * g_ref[...]
    o_ref[...] = (o * (1.0 - LAM_INIT)).T.astype(o_ref.dtype)


def _attention(lam, qt, k, vt, km, vmt, subln_col, bsz, seq, tq, tk):
    t = k.shape[0]
    nq = seq // tq
    nk = seq // tk
    return pl.pallas_call(
        functools.partial(_attn_kernel, tq=tq, tk=tk),
        grid=(bsz, HEADS, nq),
        in_specs=[
            pl.BlockSpec(memory_space=pltpu.SMEM),
            pl.BlockSpec((V_DIM, tq), lambda b, h, i: (h, b * nq + i)),
            pl.BlockSpec((seq, V_DIM), lambda b, h, i: (b, h)),
            pl.BlockSpec((nk, V_DIM, tk), lambda b, h, i: (b, h, 0)),
            pl.BlockSpec((N_META, V_DIM), lambda b, h, i: (0, h)),
            pl.BlockSpec((V_DIM, N_META), lambda b, h, i: (h, 0)),
            pl.BlockSpec((V_DIM, 1), lambda b, h, i: (0, 0)),
        ],
        out_specs=pl.BlockSpec((tq, V_DIM), lambda b, h, i: (b * nq + i, h)),
        out_shape=jax.ShapeDtypeStruct((t, V_WIDTH), BF16),
        scratch_shapes=[pltpu.VMEM((V_DIM, 2 * tq), F32)],
        compiler_params=pltpu.CompilerParams(
            dimension_semantics=("arbitrary", "arbitrary", "arbitrary"),
            vmem_limit_bytes=VMEM_LIMIT),
        name="diff_attention",
    )(lam, qt, k, vt, km, vmt, subln_col)


CONV_ROWS = 32
CONV_WINDOW = 64
CONV_PAD = 32


def _conv_kernel(u_ref, um_ref, w_ref, cb_ref, g_ref, b_ref, z_ref, ucat, grp, *, seq):
    c = u_ref.shape[1]
    ucat[0:CONV_PAD - N_META, :] = jnp.zeros((CONV_PAD - N_META, c), F32)
    ucat[CONV_PAD - N_META:CONV_PAD, :] = um_ref[...]
    ucat[CONV_PAD:CONV_PAD + seq, :] = u_ref[...]
    shift = CONV_PAD - (CONV_WIDTH - 1)

    def body(t, carry):
        r0 = pl.multiple_of(t * CONV_ROWS, CONV_ROWS)
        acc = jnp.zeros((CONV_ROWS, c), F32)
        for r in range(SUBLANES):
            taps = [j for j in range(CONV_WIDTH) if (shift + j) % SUBLANES == r]
            if not taps:
                continue
            lo = shift + taps[0]
            rows = taps[-1] - taps[0] + CONV_ROWS
            start = pl.multiple_of(r0 + lo - r, SUBLANES)
            span = rows + (SUBLANES if r else 0)
            grp[r, 0:rows, :] = ucat[pl.ds(start, span), :][r:r + rows, :]
            for j in taps:
                off = shift + j - lo
                acc = acc + grp[r, off:off + CONV_ROWS, :] * w_ref[j:j + 1, :]
        y = _ln(acc + cb_ref[...], g_ref[...], b_ref[...])
        z_ref[pl.ds(r0, CONV_ROWS), :] = _silu(y).astype(z_ref.dtype)
        return carry

    lax.fori_loop(0, seq // CONV_ROWS, body, 0)


def _conv(u, um, conv_w, conv_b, ln_g, ln_b, bsz, seq):
    t, c = u.shape
    const = lambda b: (0, 0)
    return pl.pallas_call(
        functools.partial(_conv_kernel, seq=seq),
        grid=(bsz,),
        in_specs=[
            pl.BlockSpec((seq, c), lambda b: (b, 0)),
            pl.BlockSpec((N_META, c), const),
            pl.BlockSpec((CONV_WIDTH, c), const),
            pl.BlockSpec((1, c), const),
            pl.BlockSpec((1, c), const),
            pl.BlockSpec((1, c), const),
        ],
        out_specs=pl.BlockSpec((seq, c), lambda b: (b, 0)),
        out_shape=jax.ShapeDtypeStruct((t, c), BF16),
        scratch_shapes=[pltpu.VMEM((CONV_PAD + seq, c), F32),
                        pltpu.VMEM((SUBLANES, CONV_WINDOW, c), F32)],
        compiler_params=pltpu.CompilerParams(
            dimension_semantics=("arbitrary",), vmem_limit_bytes=VMEM_LIMIT),
        name="conformer_conv",
    )(u, um, conv_w, conv_b, ln_g, ln_b)


def _merge_route_kernel(x_ref, lg_ref, lb_ref, oa_ref, z_ref, ga_ref, gc_ref,
                        wao_ref, wco_ref, bco_ref, wout_ref, g1_ref, b1_ref,
                        wrt_ref, rb_ref,
                        h1_ref, h1t_ref, te_ref, gw_ref, rk_ref, cnt_ref, carry_sc):
    tm = x_ref.shape[0]

    @pl.when(pl.program_id(0) == 0)
    def _():
        carry_sc[...] = jnp.zeros(carry_sc.shape, F32)

    h0 = _ln(x_ref[...], lg_ref[...], lb_ref[...])
    y_attn = _dot(oa_ref[...], wao_ref[...])
    y_conv = _dot(z_ref[...], wco_ref[...]) + bco_ref[...]
    merged = ga_ref[...].astype(F32) * y_attn + gc_ref[...].astype(F32) * y_conv
    h1 = _ln(ALPHA * h0 + _dot(merged.astype(BF16), wout_ref[...]), g1_ref[...], b1_ref[...])
    h1_ref[...] = h1
    _rows_to_tiles(h1t_ref, h1)
    h1b = h1.astype(BF16)

    scores = jax.nn.sigmoid(_dot_nt(wrt_ref[...], h1b))
    sel = scores + rb_ref[...]

    sub = lax.broadcasted_iota(jnp.int32, (GROUP_SIZE, tm), 0)
    gscore = []
    for g in range(N_GROUPS):
        slab = sel[g * GROUP_SIZE:(g + 1) * GROUP_SIZE, :]
        top1 = jnp.max(slab, axis=0, keepdims=True)
        arg1 = jnp.min(jnp.where(slab == top1, sub, GROUP_SIZE), axis=0, keepdims=True)
        top2 = jnp.max(jnp.where(sub == arg1, NEG_INF, slab), axis=0, keepdims=True)
        gscore.append(top1 + top2)

    eiota = lax.broadcasted_iota(jnp.int32, (N_EXPERTS, tm), 0)
    egroup = eiota // GROUP_SIZE
    allowed = jnp.zeros((N_EXPERTS, tm), F32)
    for _ in range(TOPK_GROUPS):
        best = functools.reduce(jnp.maximum, gscore)
        pick = jnp.full((1, tm), N_GROUPS, jnp.int32)
        for g in reversed(range(N_GROUPS)):
            pick = jnp.where(gscore[g] == best, g, pick)
        gscore = [jnp.where(pick == g, NEG_INF, gscore[g]) for g in range(N_GROUPS)]
        allowed = jnp.where(egroup == pick, 1.0, allowed)

    masked = jnp.where(allowed > 0.5, sel, NEG_INF)

    picks = []
    weights = []
    chosen = jnp.zeros((N_EXPERTS, tm), F32)
    for _ in range(TOP_K):
        best = jnp.max(masked, axis=0, keepdims=True)
        pick = jnp.min(jnp.where(masked == best, eiota, N_EXPERTS), axis=0, keepdims=True)
        hit = eiota == pick
        picks.append(pick)
        weights.append(jnp.sum(jnp.where(hit, scores, 0.0), axis=0, keepdims=True))
        chosen = jnp.where(hit, 1.0, chosen)
        masked = jnp.where(hit, NEG_INF, masked)

    gw = jnp.concatenate(weights, axis=0)
    gw_ref[...] = gw / jnp.sum(gw, axis=0, keepdims=True) * ROUTED_SCALE
    te_ref[...] = jnp.concatenate(picks, axis=0)

    before = (lax.broadcasted_iota(jnp.int32, (tm, tm), 0)
              < lax.broadcasted_iota(jnp.int32, (tm, tm), 1))
    prior = _dot(chosen.astype(BF16), jnp.where(before, 1.0, 0.0).astype(BF16)) + carry_sc[...]
    ranks = [jnp.sum(jnp.where(eiota == p, prior, 0.0), axis=0, keepdims=True) for p in picks]
    rk_ref[...] = jnp.concatenate(ranks, axis=0).astype(jnp.int32)
    carry_sc[...] = carry_sc[...] + jnp.sum(chosen, axis=1, keepdims=True)
    cnt_ref[...] = carry_sc[...]


def _merge_route(x2, lg, lb, oa, z, ga, gc, wao, wco, bco, wout, g1, b1, wrt, rb, tm):
    t, d = x2.shape
    c = d // 2
    row = lambda i: (i, 0)
    col = lambda i: (0, i)
    const = lambda i: (0, 0)
    full = lambda a: pl.BlockSpec(a.shape, const)
    return pl.pallas_call(
        _merge_route_kernel,
        grid=(t // tm,),
        in_specs=[
            pl.BlockSpec((tm, d), row), full(lg), full(lb),
            pl.BlockSpec((tm, V_WIDTH), row), pl.BlockSpec((tm, c), row),
            pl.BlockSpec((tm, d), row), pl.BlockSpec((tm, d), row),
            full(wao), full(wco), full(bco), full(wout), full(g1), full(b1),
            full(wrt), full(rb),
        ],
        out_specs=[
            pl.BlockSpec((tm, d), row),
            pl.BlockSpec((tm * SUBLANES, LANES), row),
            pl.BlockSpec((TOP_K, tm), col),
            pl.BlockSpec((TOP_K, tm), col),
            pl.BlockSpec((TOP_K, tm), col),
            pl.BlockSpec((N_EXPERTS, 1), const),
        ],
        out_shape=[
            jax.ShapeDtypeStruct((t, d), F32),
            jax.ShapeDtypeStruct((t * SUBLANES, LANES), F32),
            jax.ShapeDtypeStruct((TOP_K, t), jnp.int32),
            jax.ShapeDtypeStruct((TOP_K, t), F32),
            jax.ShapeDtypeStruct((TOP_K, t), jnp.int32),
            jax.ShapeDtypeStruct((N_EXPERTS, 1), F32),
        ],
        scratch_shapes=[pltpu.VMEM((N_EXPERTS, 1), F32)],
        compiler_params=pltpu.CompilerParams(
            dimension_semantics=("arbitrary",), vmem_limit_bytes=VMEM_LIMIT),
        name="merge_route",
    )(x2, lg, lb, oa, z, ga, gc, wao, wco, bco, wout, g1, b1, wrt, rb)


def _dest_kernel(te_ref, rk_ref, start_ref, d_ref):
    tm = te_ref.shape[1]
    eiota = lax.broadcasted_iota(jnp.int32, (N_EXPERTS, tm), 0)
    start = start_ref[...]
    rows = [jnp.sum(jnp.where(eiota == te_ref[k:k + 1, :], start, 0.0), axis=0, keepdims=True)
            for k in range(TOP_K)]
    d_ref[...] = (jnp.concatenate(rows, axis=0).astype(jnp.int32) + rk_ref[...]) * SUBLANES


def _dest(top_e, rank, start_col, tm):
    t = top_e.shape[1]
    col = lambda i: (0, i)
    return pl.pallas_call(
        _dest_kernel,
        grid=(t // tm,),
        in_specs=[
            pl.BlockSpec((TOP_K, tm), col),
            pl.BlockSpec((TOP_K, tm), col),
            pl.BlockSpec((N_EXPERTS, 1), lambda i: (0, 0)),
        ],
        out_specs=pl.BlockSpec((TOP_K, tm), col),
        out_shape=jax.ShapeDtypeStruct((TOP_K, t), jnp.int32),
        compiler_params=pltpu.CompilerParams(dimension_semantics=("arbitrary",)),
        name="moe_dest",
    )(top_e, rank, start_col)


def _dispatch_kernel(dest_ref, h_ref, xs_ref, sem, *, n_tok):
    tm = h_ref.shape[0] // SUBLANES
    base = pl.program_id(0) * tm
    for r in range(tm):
        for k in range(TOP_K):
            slot = pl.multiple_of(dest_ref[k * n_tok + base + r], SUBLANES)
            pltpu.make_async_copy(h_ref.at[pl.ds(r * SUBLANES, SUBLANES)],
                                  xs_ref.at[pl.ds(slot, SUBLANES)], sem).start(priority=k % 2)
    for k in range(TOP_K):
        pltpu.make_async_copy(h_ref, xs_ref.at[pl.ds(0, tm * SUBLANES)], sem).wait()


def _dispatch(dest_flat, h_tiles, tm):
    t = h_tiles.shape[0] // SUBLANES
    return pl.pallas_call(
        functools.partial(_dispatch_kernel, n_tok=t),
        grid_spec=pltpu.PrefetchScalarGridSpec(
            num_scalar_prefetch=1,
            grid=(t // tm,),
            in_specs=[pl.BlockSpec((tm * SUBLANES, LANES), lambda i, dest: (i, 0))],
            out_specs=pl.BlockSpec(memory_space=pl.ANY),
            scratch_shapes=[pltpu.SemaphoreType.DMA],
        ),
        out_shape=jax.ShapeDtypeStruct((t * TOP_K * SUBLANES, LANES), h_tiles.dtype),
        compiler_params=pltpu.CompilerParams(
            dimension_semantics=("arbitrary",), has_side_effects=True),
        name="moe_dispatch",
    )(dest_flat, h_tiles)


def _experts_kernel(blk_ref, exp_ref, lo_ref, hi_ref, first_ref, fresh_ref,
                    x_ref, wg_ref, wu_ref, wd_ref, y_ref, wg_bf, wu_bf, wd_bf):
    w = pl.program_id(0)
    lo = lo_ref[w]
    hi = hi_ref[w]
    tm = x_ref.shape[0] // SUBLANES

    @pl.when(fresh_ref[w] == 1)
    def _():
        wg_bf[...] = wg_ref[...].astype(BF16)
        wu_bf[...] = wu_ref[...].astype(BF16)
        wd_bf[...] = wd_ref[...].astype(BF16)

    @pl.when(hi > lo)
    def _():
        x = _rows_from_tiles(x_ref, tm).astype(BF16)
        a = _silu(_dot(x, wg_bf[...])) * _dot(x, wu_bf[...])
        y = _dot(a.astype(BF16), wd_bf[...])
        rows = lax.broadcasted_iota(jnp.int32, (tm, 1), 0)
        owned = jnp.logical_and(rows >= lo, rows < hi)
        plain = jnp.logical_or(first_ref[w] == 1, hi - lo == tm)

        @pl.when(plain)
        def _():
            _rows_to_tiles(y_ref, y)

        @pl.when(jnp.logical_not(plain))
        def _():
            _rows_to_tiles(y_ref, jnp.where(owned, y, _rows_from_tiles(y_ref, tm)))


def _experts(items, xs, wg, wu, wd, tm):
    d = SUBLANES * LANES
    hid = wg.shape[2]
    n_items = items[0].shape[0]
    xmap = lambda w, blk, exp, lo, hi, first, fresh: (blk[w], 0)
    wmap = lambda w, blk, exp, lo, hi, first, fresh: (exp[w], 0, 0)
    return pl.pallas_call(
        _experts_kernel,
        grid_spec=pltpu.PrefetchScalarGridSpec(
            num_scalar_prefetch=len(items),
            grid=(n_items,),
            in_specs=[
                pl.BlockSpec((tm * SUBLANES, LANES), xmap),
                pl.BlockSpec((None, d, hid), wmap),
                pl.BlockSpec((None, d, hid), wmap),
                pl.BlockSpec((None, hid, d), wmap),
            ],
            out_specs=pl.BlockSpec((tm * SUBLANES, LANES), xmap),
            scratch_shapes=[pltpu.VMEM((d, hid), BF16), pltpu.VMEM((d, hid), BF16),
                            pltpu.VMEM((hid, d), BF16)],
        ),
        out_shape=jax.ShapeDtypeStruct(xs.shape, xs.dtype),
        compiler_params=pltpu.CompilerParams(
            dimension_semantics=("arbitrary",), vmem_limit_bytes=VMEM_LIMIT),
        name="moe_experts",
    )(*items, xs, wg, wu, wd)


def _combine_kernel(dest_ref, h_ref, gw_ref, ys_ref, wgs_ref, wus_ref, wds_ref, g2_ref, b2_ref,
                    o_ref, buf, sem, *, n_tok):
    tm = h_ref.shape[0] // 2
    step = pl.program_id(0)

    def gather(tile, into):
        base = tile * tm
        for r in range(tm):
            for k in range(TOP_K):
                slot = pl.multiple_of(dest_ref[k * n_tok + base + r], SUBLANES)
                pltpu.make_async_copy(ys_ref.at[pl.ds(slot, SUBLANES)],
                                      buf.at[into, k, pl.ds(r * SUBLANES, SUBLANES)],
                                      sem.at[into]).start(priority=k % 2)

    def finish(half):
        rows = slice(half * tm, (half + 1) * tm)
        h = h_ref[rows, :]
        hb = h.astype(BF16)
        shared = _dot((_silu(_dot(hb, wgs_ref[...])) * _dot(hb, wus_ref[...])).astype(BF16),
                      wds_ref[...])
        acc = ALPHA * h + shared
        for k in range(TOP_K):
            pltpu.make_async_copy(ys_ref.at[pl.ds(0, tm * SUBLANES)], buf.at[half, k],
                                  sem.at[half]).wait()
        gw = gw_ref[rows, :]
        for k in range(TOP_K):
            acc = acc + _rows_from_tiles(buf.at[half, k], tm) * gw[:, k:k + 1]
        o_ref[rows, :] = _ln(acc, g2_ref[...], b2_ref[...])

    @pl.when(step == 0)
    def _():
        gather(0, 0)

    gather(2 * step + 1, 1)
    finish(0)

    @pl.when(step + 1 < pl.num_programs(0))
    def _():
        gather(2 * step + 2, 0)

    finish(1)


def _combine(dest_flat, h1, gw, ys, wgs, wus, wds, g2, b2, tm):
    t, d = h1.shape
    row = lambda i, dest: (i, 0)
    const = lambda i, dest: (0, 0)
    full = lambda a: pl.BlockSpec(a.shape, const)
    return pl.pallas_call(
        functools.partial(_combine_kernel, n_tok=t),
        grid_spec=pltpu.PrefetchScalarGridSpec(
            num_scalar_prefetch=1,
            grid=(t // (2 * tm),),
            in_specs=[
                pl.BlockSpec((2 * tm, d), row),
                pl.BlockSpec((2 * tm, TOP_K), row),
                pl.BlockSpec(memory_space=pl.ANY),
                full(wgs), full(wus), full(wds), full(g2), full(b2),
            ],
            out_specs=pl.BlockSpec((2 * tm, d), row),
            scratch_shapes=[pltpu.VMEM((2, TOP_K, tm * SUBLANES, LANES), ys.dtype),
                            pltpu.SemaphoreType.DMA((2,))],
        ),
        out_shape=jax.ShapeDtypeStruct((t, d), F32),
        compiler_params=pltpu.CompilerParams(
            dimension_semantics=("arbitrary",), vmem_limit_bytes=VMEM_LIMIT),
        name="moe_combine",
    )(dest_flat, h1, gw, ys, wgs, wus, wds, g2, b2)


def _work_items(counts, n_rows, tm):
    n_blocks = n_rows // tm
    n_items = n_blocks + N_EXPERTS - 1
    start = jnp.cumsum(counts) - counts
    end = start + counts
    first_blk = start // tm
    last_blk = jnp.maximum(end - 1, start) // tm
    per_expert = jnp.where(counts > 0, last_blk - first_blk + 1, 0)
    item_end = jnp.cumsum(per_expert)
    item_start = item_end - per_expert
    total = item_end[-1]
    w = jnp.arange(n_items, dtype=jnp.int32)
    wc = jnp.minimum(w, total - 1)
    exp = jnp.sum((item_end[None, :] <= wc[:, None]).astype(jnp.int32), axis=1)
    blk = (first_blk[exp] + wc - item_start[exp]).astype(jnp.int32)
    lo = jnp.maximum(start[exp], blk * tm) - blk * tm
    hi = jnp.minimum(end[exp], (blk + 1) * tm) - blk * tm
    valid = w < total
    lo = jnp.where(valid, lo, 0).astype(jnp.int32)
    hi = jnp.where(valid, hi, 0).astype(jnp.int32)
    prev_blk = jnp.concatenate([jnp.full((1,), -1, jnp.int32), blk[:-1]])
    first = jnp.logical_and(valid, blk != prev_blk).astype(jnp.int32)
    prev_exp = jnp.concatenate([jnp.full((1,), -1, jnp.int32), exp[:-1]])
    fresh = jnp.logical_and(valid, exp != prev_exp).astype(jnp.int32)
    return (blk, exp, lo, hi, first, fresh), start


def _rope_tables(positions):
    inv = 1.0 / (ROPE_THETA ** (jnp.arange(0, HEAD_DIM, 2, dtype=F32) / HEAD_DIM))
    ang = positions.astype(F32)[:, None] * inv[None, :]
    cos = jnp.cos(ang)
    sin = jnp.sin(ang)
    cos = jnp.concatenate([cos, cos], axis=-1)
    sin = jnp.concatenate([-sin, sin], axis=-1)
    reps = V_DIM // HEAD_DIM
    return jnp.tile(cos, (1, reps)), jnp.tile(sin, (1, reps)), cos.T, sin.T


def _tile(n, pref):
    return pref if n % pref == 0 else n


def kernel(x, meta_tokens, ln_in_g, ln_in_b, w_in, lambda_q1, lambda_k1, lambda_q2, lambda_k2, subln_g, w_attn_o, conv_w, conv_b, conv_ln_g, conv_ln_b, w_conv_o, b_conv_o, w_out, ln1_g, ln1_b, w_router, router_bias, w_gate_e, w_up_e, w_down_e, w_gate_s, w_up_s, w_down_s, ln2_g, ln2_b):
    bsz, seq, d = x.shape
    t = bsz * seq
    l = 0
    row = lambda a: a.reshape(1, -1).astype(F32)
    x2 = x.reshape(t, d)
    lg, lb = row(ln_in_g), row(ln_in_b)
    w_in_bf = w_in[l].astype(BF16)
    o_v = 2 * QK_WIDTH
    wqt = w_in_bf[:, :QK_WIDTH].T
    wk = w_in_bf[:, QK_WIDTH:o_v]
    wvt = w_in_bf[:, o_v:o_v + V_WIDTH].T
    wr = w_in_bf[:, o_v + V_WIDTH:]

    tables_x = _rope_tables(jnp.arange(N_META, N_META + seq))
    tables_m = _rope_tables(jnp.arange(N_META))

    tq = _tile(seq, 512)
    tk = _tile(seq, 256)
    qt, k, vt, u, ga, gc = _inproj(x2, lg, lb, wqt, wk, wvt, wr, tables_x, tq, tk)
    _, km, vmt, um, _, _ = _inproj(meta_tokens.astype(F32), lg, lb, wqt, wk, wvt, wr, tables_m,
                                   N_META, N_META)

    lam = (jnp.exp(jnp.sum(lambda_q1[l].astype(F32) * lambda_k1[l].astype(F32)))
           - jnp.exp(jnp.sum(lambda_q2[l].astype(F32) * lambda_k2[l].astype(F32)))
           + LAM_INIT).reshape(1, 1)
    o_attn = _attention(lam, qt, k, vt, km, vmt.reshape(V_WIDTH, N_META),
                        subln_g[l].reshape(-1, 1).astype(F32), bsz, seq, tq, tk)

    z = _conv(u, um, conv_w[l].astype(F32), row(conv_b[l]), row(conv_ln_g[l]),
              row(conv_ln_b[l]), bsz, seq)

    assert d == SUBLANES * LANES, "the row-tile layout assumes one (8, 128) tile per token row"
    h1, h1_tiles, top_e, gw_t, rank, counts = _merge_route(
        x2, lg, lb, o_attn, z, ga, gc,
        w_attn_o[l].astype(BF16), w_conv_o[l].astype(BF16), row(b_conv_o[l]),
        w_out[l].astype(BF16), row(ln1_g[l]), row(ln1_b[l]),
        w_router[l].T.astype(BF16), router_bias[l].reshape(-1, 1).astype(F32),
        _tile(t, 256))

    tm_e = 256
    counts = counts.reshape(-1).astype(jnp.int32)
    items, start = _work_items(counts, t * TOP_K, tm_e)
    dest = _dest(top_e, rank, start.astype(F32).reshape(-1, 1), _tile(t, 512)).reshape(-1)

    xs = _dispatch(dest, h1_tiles, _tile(t, 128))
    ys = _experts(items, xs, w_gate_e[l], w_up_e[l], w_down_e[l], tm_e)
    out = _combine(dest, h1, gw_t.T, ys,
                   w_gate_s[l].astype(BF16), w_up_s[l].astype(BF16), w_down_s[l].astype(BF16),
                   row(ln2_g[l]), row(ln2_b[l]), _tile(t, 128))
    return out.reshape(bsz, seq, d)
```

```python
import functools
import math

import jax
import jax.numpy as jnp
from jax import lax
from jax.experimental import pallas as pl
from jax.experimental.pallas import tpu as pltpu

N_META = 16
HEADS = 4
HEAD_DIM = 64
V_DIM = 2 * HEAD_DIM
QK_WIDTH = HEADS * 2 * HEAD_DIM
V_WIDTH = HEADS * V_DIM
ROPE_THETA = 10000.0
CONV_WIDTH = 31
N_EXPERTS = 256
TOP_K = 8
N_GROUPS = 8
TOPK_GROUPS = 4
GROUP_SIZE = N_EXPERTS // N_GROUPS
ROUTED_SCALE = 2.5
LN_EPS = 1e-5
DEPTH = 1
ALPHA = (2.0 * DEPTH) ** 0.25
LAM_INIT = 0.8 - 0.6 * math.exp(-0.3 * 0)

F32 = jnp.float32
BF16 = jnp.bfloat16
NEG_INF = float("-inf")

VMEM_LIMIT = 56 * 1024 * 1024


def _ln(x, g, b):
    mu = jnp.mean(x, axis=-1, keepdims=True)
    xc = x - mu
    var = jnp.mean(xc * xc, axis=-1, keepdims=True)
    return xc * lax.rsqrt(var + LN_EPS) * g + b


def _dot(a, b):
    return jnp.dot(a, b, preferred_element_type=F32)


def _dot_nt(a, b):
    return lax.dot_general(a, b, (((1,), (1,)), ((), ())), preferred_element_type=F32)


def _silu(x):
    return x * jax.nn.sigmoid(x)


SUBLANES = 8
LANES = 128


def _rows_from_tiles(ref, n_rows, first=0):
    return jnp.concatenate(
        [ref[pl.ds(first * SUBLANES + c, n_rows, stride=SUBLANES), :] for c in range(SUBLANES)],
        axis=1)


def _rows_to_tiles(ref, x):
    for c in range(SUBLANES):
        ref[pl.ds(c, x.shape[0], stride=SUBLANES), :] = x[:, c * LANES:(c + 1) * LANES]


LOG2E = math.log2(math.e)
HALF_HEAD = HEAD_DIM // 2


def _inproj_kernel(x_ref, g_ref, b_ref, wqt_ref, wk_ref, wvt_ref, wr_ref,
                   cos_ref, sin_ref, cost_ref, sint_ref,
                   qt_ref, k_ref, vt_ref, u_ref, ga_ref, gc_ref, *, d_model):
    h = _ln(x_ref[...], g_ref[...], b_ref[...]).astype(BF16)
    tm = x_ref.shape[0]
    c = d_model // 2

    qt = _dot_nt(wqt_ref[...], h)
    pieces = []
    for g in range(QK_WIDTH // HEAD_DIM):
        lo = g * HEAD_DIM
        pieces.append(qt[lo + HALF_HEAD:lo + HEAD_DIM, :])
        pieces.append(qt[lo:lo + HALF_HEAD, :])
    partner = jnp.concatenate(pieces, axis=0)
    reps_t = QK_WIDTH // cost_ref.shape[0]
    qt = qt * jnp.tile(cost_ref[...], (reps_t, 1)) + partner * jnp.tile(sint_ref[...], (reps_t, 1))
    qt_ref[...] = (qt * (HEAD_DIM ** -0.5 * LOG2E)).astype(BF16)

    kk = _dot(h, wk_ref[...])
    reps = QK_WIDTH // cos_ref.shape[1]
    lane = lax.broadcasted_iota(jnp.int32, (tm, QK_WIDTH), 1)
    partner = jnp.where((lane % HEAD_DIM) < HALF_HEAD,
                        pltpu.roll(kk, QK_WIDTH - HALF_HEAD, 1),
                        pltpu.roll(kk, HALF_HEAD, 1))
    kk = kk * jnp.tile(cos_ref[...], (1, reps)) + partner * jnp.tile(sin_ref[...], (1, reps))
    k_ref[...] = kk.astype(BF16)

    vt = _dot_nt(wvt_ref[...], h).astype(BF16)
    chunk = vt_ref.shape[2]
    for j in range(vt_ref.shape[0]):
        vt_ref[j] = vt[:, j * chunk:(j + 1) * chunk]

    def proj(lo, width):
        return _dot(h, wr_ref[:, lo:lo + width])

    u_ref[...] = proj(0, c) * jax.nn.sigmoid(proj(c, c))
    ga_ref[...] = jax.nn.sigmoid(proj(2 * c, d_model)).astype(BF16)
    gc_ref[...] = jax.nn.sigmoid(proj(2 * c + d_model, d_model)).astype(BF16)


def _inproj(x2, g, b, wqt, wk, wvt, wr, tables, tm, chunk):
    t, d = x2.shape
    c = d // 2
    cos, sin, cost, sint = tables
    n_pos_tiles = cos.shape[0] // tm
    row = lambda i: (i, 0)
    col = lambda i: (0, i)
    const = lambda i: (0, 0)
    pos = lambda i: (i % n_pos_tiles, 0)
    post = lambda i: (0, i % n_pos_tiles)
    full = lambda a: pl.BlockSpec(a.shape, const)
    return pl.pallas_call(
        functools.partial(_inproj_kernel, d_model=d),
        grid=(t // tm,),
        in_specs=[
            pl.BlockSpec((tm, d), row), full(g), full(b),
            full(wqt), full(wk), full(wvt), full(wr),
            pl.BlockSpec((tm, cos.shape[1]), pos),
            pl.BlockSpec((tm, cos.shape[1]), pos),
            pl.BlockSpec((cost.shape[0], tm), post),
            pl.BlockSpec((cost.shape[0], tm), post),
        ],
        out_specs=[
            pl.BlockSpec((QK_WIDTH, tm), col),
            pl.BlockSpec((tm, QK_WIDTH), row),
            pl.BlockSpec((tm // chunk, V_WIDTH, chunk), lambda i: (i, 0, 0)),
            pl.BlockSpec((tm, c), row),
            pl.BlockSpec((tm, d), row),
            pl.BlockSpec((tm, d), row),
        ],
        out_shape=[
            jax.ShapeDtypeStruct((QK_WIDTH, t), BF16),
            jax.ShapeDtypeStruct((t, QK_WIDTH), BF16),
            jax.ShapeDtypeStruct((t // chunk, V_WIDTH, chunk), BF16),
            jax.ShapeDtypeStruct((t, c), F32),
            jax.ShapeDtypeStruct((t, d), BF16),
            jax.ShapeDtypeStruct((t, d), BF16),
        ],
        compiler_params=pltpu.CompilerParams(
            dimension_semantics=("arbitrary",), vmem_limit_bytes=VMEM_LIMIT),
        name="inproj",
    )(x2, g, b, wqt, wk, wvt, wr, cos, sin, cost, sint)


def _attn_kernel(lam_ref, qt_ref, k_ref, vt_ref, km_ref, vmt_ref, g_ref, o_ref, acc_sc, *, tq, tk):
    i = pl.program_id(2)
    qt = qt_ref[...]
    feat = lax.broadcasted_iota(jnp.int32, qt.shape, 0)
    zero = jnp.zeros_like(qt)
    q2 = jnp.concatenate([jnp.where(feat < HEAD_DIM, qt, zero),
                          jnp.where(feat >= HEAD_DIM, qt, zero)], axis=1)

    def scores(j):
        r0 = pl.multiple_of(j * tk, tk)
        return _dot(k_ref[pl.ds(r0, tk), :], q2)

    def absorb(m_old, l_old, s, vtc):
        m_new = jnp.maximum(m_old, jnp.max(s, axis=0, keepdims=True))
        alpha = jnp.exp2(m_old - m_new)
        p = jnp.exp2(s - m_new)
        acc_sc[...] = alpha * acc_sc[...] + _dot(vtc, p.astype(BF16))
        return m_new, alpha * l_old + jnp.sum(p, axis=0, keepdims=True)

    s = _dot(km_ref[...], q2)
    m = jnp.max(s, axis=0, keepdims=True)
    p = jnp.exp2(s - m)
    acc_sc[...] = _dot(vmt_ref[...], p.astype(BF16))
    l = jnp.sum(p, axis=0, keepdims=True)

    per_tile = tq // tk
    n_full = i * per_tile

    def body(j, carry):
        m_old, l_old, s_cur = carry
        s_next = scores(j + 1)
        m_new, l_new = absorb(m_old, l_old, s_cur, vt_ref[j])
        return m_new, l_new, s_next

    m, l, s = lax.fori_loop(0, n_full, body, (m, l, scores(0)))

    key = lax.broadcasted_iota(jnp.int32, (tk, 2 * tq), 0)
    qry = lax.broadcasted_iota(jnp.int32, (tk, 2 * tq), 1) % tq
    for d in range(per_tile):
        s_cur = s
        if d + 1 < per_tile:
            s = scores(n_full + d + 1)
        m, l = absorb(m, l, jnp.where(key + d * tk <= qry, s_cur, NEG_INF), vt_ref[n_full + d])

    lam = lam_ref[0, 0]
    o = acc_sc[:, :tq] / l[:, :tq] - lam * (acc_sc[:, tq:] / l[:, tq:])
    o = o * lax.rsqrt(jnp.mean(o * o, axis=0, keepdims=True) + LN_EPS) * g_ref[...]
    o_ref[...] = (o * (1.0 - LAM_INIT)).T.astype(o_ref.dtype)


def _attention(lam, qt, k, vt, km, vmt, subln_col, bsz, seq, tq, tk):
    t = k.shape[0]
    nq = seq // tq
    nk = seq // tk
    return pl.pallas_call(
        functools.partial(_attn_kernel, tq=tq, tk=tk),
        grid=(bsz, HEADS, nq),
        in_specs=[
            pl.BlockSpec(memory_space=pltpu.SMEM),
            pl.BlockSpec((V_DIM, tq), lambda b, h, i: (h, b * nq + i)),
            pl.BlockSpec((seq, V_DIM), lambda b, h, i: (b, h)),
            pl.BlockSpec((nk, V_DIM, tk), lambda b, h, i: (b, h, 0)),
            pl.BlockSpec((N_META, V_DIM), lambda b, h, i: (0, h)),
            pl.BlockSpec((V_DIM, N_META), lambda b, h, i: (h, 0)),
            pl.BlockSpec((V_DIM, 1), lambda b, h, i: (0, 0)),
        ],
        out_specs=pl.BlockSpec((tq, V_DIM), lambda b, h, i: (b * nq + i, h)),
        out_shape=jax.ShapeDtypeStruct((t, V_WIDTH), BF16),
        scratch_shapes=[pltpu.VMEM((V_DIM, 2 * tq), F32)],
        compiler_params=pltpu.CompilerParams(
            dimension_semantics=("arbitrary", "arbitrary", "arbitrary"),
            vmem_limit_bytes=VMEM_LIMIT),
        name="diff_attention",
    )(lam, qt, k, vt, km, vmt, subln_col)


CONV_ROWS = 32
CONV_WINDOW = 64
CONV_PAD = 32


def _conv_kernel(u_ref, um_ref, w_ref, cb_ref, g_ref, b_ref, z_ref, ucat, grp, *, seq):
    c = u_ref.shape[1]
    ucat[0:CONV_PAD - N_META, :] = jnp.zeros((CONV_PAD - N_META, c), F32)
    ucat[CONV_PAD - N_META:CONV_PAD, :] = um_ref[...]
    ucat[CONV_PAD:CONV_PAD + seq, :] = u_ref[...]
    shift = CONV_PAD - (CONV_WIDTH - 1)

    def body(t, carry):
        r0 = pl.multiple_of(t * CONV_ROWS, CONV_ROWS)
        acc = jnp.zeros((CONV_ROWS, c), F32)
        for r in range(SUBLANES):
            taps = [j for j in range(CONV_WIDTH) if (shift + j) % SUBLANES == r]
            if not taps:
                continue
            lo = shift + taps[0]
            rows = taps[-1] - taps[0] + CONV_ROWS
            start = pl.multiple_of(r0 + lo - r, SUBLANES)
            span = rows + (SUBLANES if r else 0)
            grp[r, 0:rows, :] = ucat[pl.ds(start, span), :][r:r + rows, :]
            for j in taps:
                off = shift + j - lo
                acc = acc + grp[r, off:off + CONV_ROWS, :] * w_ref[j:j + 1, :]
        y = _ln(acc + cb_ref[...], g_ref[...], b_ref[...])
        z_ref[pl.ds(r0, CONV_ROWS), :] = _silu(y).astype(z_ref.dtype)
        return carry

    lax.fori_loop(0, seq // CONV_ROWS, body, 0)


def _conv(u, um, conv_w, conv_b, ln_g, ln_b, bsz, seq):
    t, c = u.shape
    const = lambda b: (0, 0)
    return pl.pallas_call(
        functools.partial(_conv_kernel, seq=seq),
        grid=(bsz,),
        in_specs=[
            pl.BlockSpec((seq, c), lambda b: (b, 0)),
            pl.BlockSpec((N_META, c), const),
            pl.BlockSpec((CONV_WIDTH, c), const),
            pl.BlockSpec((1, c), const),
            pl.BlockSpec((1, c), const),
            pl.BlockSpec((1, c), const),
        ],
        out_specs=pl.BlockSpec((seq, c), lambda b: (b, 0)),
        out_shape=jax.ShapeDtypeStruct((t, c), BF16),
        scratch_shapes=[pltpu.VMEM((CONV_PAD + seq, c), F32),
                        pltpu.VMEM((SUBLANES, CONV_WINDOW, c), F32)],
        compiler_params=pltpu.CompilerParams(
            dimension_semantics=("arbitrary",), vmem_limit_bytes=VMEM_LIMIT),
        name="conformer_conv",
    )(u, um, conv_w, conv_b, ln_g, ln_b)


def _merge_route_kernel(x_ref, lg_ref, lb_ref, oa_ref, z_ref, ga_ref, gc_ref,
                        wao_ref, wco_ref, bco_ref, wout_ref, g1_ref, b1_ref,
                        wrt_ref, rb_ref,
                        h1_ref, h1t_ref, te_ref, gw_ref, rk_ref, cnt_ref, carry_sc):
    tm = x_ref.shape[0]

    @pl.when(pl.program_id(0) == 0)
    def _():
        carry_sc[...] = jnp.zeros(carry_sc.shape, F32)

    h0 = _ln(x_ref[...], lg_ref[...], lb_ref[...])
    y_attn = _dot(oa_ref[...], wao_ref[...])
    y_conv = _dot(z_ref[...], wco_ref[...]) + bco_ref[...]
    merged = ga_ref[...].astype(F32) * y_attn + gc_ref[...].astype(F32) * y_conv
    h1 = _ln(ALPHA * h0 + _dot(merged.astype(BF16), wout_ref[...]), g1_ref[...], b1_ref[...])
    h1_ref[...] = h1
    _rows_to_tiles(h1t_ref, h1)
    h1b = h1.astype(BF16)

    scores = jax.nn.sigmoid(_dot_nt(wrt_ref[...], h1b))
    sel = scores + rb_ref[...]

    sub = lax.broadcasted_iota(jnp.int32, (GROUP_SIZE, tm), 0)
    gscore = []
    for g in range(N_GROUPS):
        slab = sel[g * GROUP_SIZE:(g + 1) * GROUP_SIZE, :]
        top1 = jnp.max(slab, axis=0, keepdims=True)
        arg1 = jnp.min(jnp.where(slab == top1, sub, GROUP_SIZE), axis=0, keepdims=True)
        top2 = jnp.max(jnp.where(sub == arg1, NEG_INF, slab), axis=0, keepdims=True)
        gscore.append(top1 + top2)

    eiota = lax.broadcasted_iota(jnp.int32, (N_EXPERTS, tm), 0)
    egroup = eiota // GROUP_SIZE
    allowed = jnp.zeros((N_EXPERTS, tm), F32)
    for _ in range(TOPK_GROUPS):
        best = functools.reduce(jnp.maximum, gscore)
        pick = jnp.full((1, tm), N_GROUPS, jnp.int32)
        for g in reversed(range(N_GROUPS)):
            pick = jnp.where(gscore[g] == best, g, pick)
        gscore = [jnp.where(pick == g, NEG_INF, gscore[g]) for g in range(N_GROUPS)]
        allowed = jnp.where(egroup == pick, 1.0, allowed)

    masked = jnp.where(allowed > 0.5, sel, NEG_INF)

    picks = []
    weights = []
    chosen = jnp.zeros((N_EXPERTS, tm), F32)
    for _ in range(TOP_K):
        best = jnp.max(masked, axis=0, keepdims=True)
        pick = jnp.min(jnp.where(masked == best, eiota, N_EXPERTS), axis=0, keepdims=True)
        hit = eiota == pick
        picks.append(pick)
        weights.append(jnp.sum(jnp.where(hit, scores, 0.0), axis=0, keepdims=True))
        chosen = jnp.where(hit, 1.0, chosen)
        masked = jnp.where(hit, NEG_INF, masked)

    gw = jnp.concatenate(weights, axis=0)
    gw_ref[...] = gw / jnp.sum(gw, axis=0, keepdims=True) * ROUTED_SCALE
    te_ref[...] = jnp.concatenate(picks, axis=0)

    before = (lax.broadcasted_iota(jnp.int32, (tm, tm), 0)
              < lax.broadcasted_iota(jnp.int32, (tm, tm), 1))
    prior = _dot(chosen.astype(BF16), jnp.where(before, 1.0, 0.0).astype(BF16)) + carry_sc[...]
    ranks = [jnp.sum(jnp.where(eiota == p, prior, 0.0), axis=0, keepdims=True) for p in picks]
    rk_ref[...] = jnp.concatenate(ranks, axis=0).astype(jnp.int32)
    carry_sc[...] = carry_sc[...] + jnp.sum(chosen, axis=1, keepdims=True)
    cnt_ref[...] = carry_sc[...]


def _merge_route(x2, lg, lb, oa, z, ga, gc, wao, wco, bco, wout, g1, b1, wrt, rb, tm):
    t, d = x2.shape
    c = d // 2
    row = lambda i: (i, 0)
    col = lambda i: (0, i)
    const = lambda i: (0, 0)
    full = lambda a: pl.BlockSpec(a.shape, const)
    return pl.pallas_call(
        _merge_route_kernel,
        grid=(t // tm,),
        in_specs=[
            pl.BlockSpec((tm, d), row), full(lg), full(lb),
            pl.BlockSpec((tm, V_WIDTH), row), pl.BlockSpec((tm, c), row),
            pl.BlockSpec((tm, d), row), pl.BlockSpec((tm, d), row),
            full(wao), full(wco), full(bco), full(wout), full(g1), full(b1),
            full(wrt), full(rb),
        ],
        out_specs=[
            pl.BlockSpec((tm, d), row),
            pl.BlockSpec((tm * SUBLANES, LANES), row),
            pl.BlockSpec((TOP_K, tm), col),
            pl.BlockSpec((TOP_K, tm), col),
            pl.BlockSpec((TOP_K, tm), col),
            pl.BlockSpec((N_EXPERTS, 1), const),
        ],
        out_shape=[
            jax.ShapeDtypeStruct((t, d), F32),
            jax.ShapeDtypeStruct((t * SUBLANES, LANES), F32),
            jax.ShapeDtypeStruct((TOP_K, t), jnp.int32),
            jax.ShapeDtypeStruct((TOP_K, t), F32),
            jax.ShapeDtypeStruct((TOP_K, t), jnp.int32),
            jax.ShapeDtypeStruct((N_EXPERTS, 1), F32),
        ],
        scratch_shapes=[pltpu.VMEM((N_EXPERTS, 1), F32)],
        compiler_params=pltpu.CompilerParams(
            dimension_semantics=("arbitrary",), vmem_limit_bytes=VMEM_LIMIT),
        name="merge_route",
    )(x2, lg, lb, oa, z, ga, gc, wao, wco, bco, wout, g1, b1, wrt, rb)


def _dest_kernel(te_ref, rk_ref, start_ref, d_ref):
    tm = te_ref.shape[1]
    eiota = lax.broadcasted_iota(jnp.int32, (N_EXPERTS, tm), 0)
    start = start_ref[...]
    rows = [jnp.sum(jnp.where(eiota == te_ref[k:k + 1, :], start, 0.0), axis=0, keepdims=True)
            for k in range(TOP_K)]
    d_ref[...] = (jnp.concatenate(rows, axis=0).astype(jnp.int32) + rk_ref[...]) * SUBLANES


def _dest(top_e, rank, start_col, tm):
    t = top_e.shape[1]
    col = lambda i: (0, i)
    return pl.pallas_call(
        _dest_kernel,
        grid=(t // tm,),
        in_specs=[
            pl.BlockSpec((TOP_K, tm), col),
            pl.BlockSpec((TOP_K, tm), col),
            pl.BlockSpec((N_EXPERTS, 1), lambda i: (0, 0)),
        ],
        out_specs=pl.BlockSpec((TOP_K, tm), col),
        out_shape=jax.ShapeDtypeStruct((TOP_K, t), jnp.int32),
        compiler_params=pltpu.CompilerParams(dimension_semantics=("arbitrary",)),
        name="moe_dest",
    )(top_e, rank, start_col)


def _dispatch_kernel(dest_ref, h_ref, xs_ref, zeros, sem, pad_sem, *, n_tok):
    tm = h_ref.shape[0] // SUBLANES
    base = pl.program_id(0) * tm

    @pl.when(pl.program_id(0) == 0)
    def _():
        zeros[...] = jnp.zeros(zeros.shape, zeros.dtype)
        pad = pltpu.make_async_copy(
            zeros, xs_ref.at[pl.ds(n_tok * TOP_K * SUBLANES, zeros.shape[0])], pad_sem)
        pad.start()
        pad.wait()

    for r in range(tm):
        for k in range(TOP_K):
            slot = pl.multiple_of(dest_ref[k * n_tok + base + r], SUBLANES)
            pltpu.make_async_copy(h_ref.at[pl.ds(r * SUBLANES, SUBLANES)],
                                  xs_ref.at[pl.ds(slot, SUBLANES)], sem).start(priority=k % 2)
    for k in range(TOP_K):
        pltpu.make_async_copy(h_ref, xs_ref.at[pl.ds(0, tm * SUBLANES)], sem).wait()


def _dispatch(dest_flat, h_tiles, tm):
    t = h_tiles.shape[0] // SUBLANES
    return pl.pallas_call(
        functools.partial(_dispatch_kernel, n_tok=t),
        grid_spec=pltpu.PrefetchScalarGridSpec(
            num_scalar_prefetch=1,
            grid=(t // tm,),
            in_specs=[pl.BlockSpec((tm * SUBLANES, LANES), lambda i, dest: (i, 0))],
            out_specs=pl.BlockSpec(memory_space=pl.ANY),
            scratch_shapes=[pltpu.VMEM((EXPERT_CHUNK * SUBLANES, LANES), h_tiles.dtype),
                            pltpu.SemaphoreType.DMA, pltpu.SemaphoreType.DMA],
        ),
        out_shape=jax.ShapeDtypeStruct(((t * TOP_K + EXPERT_CHUNK) * SUBLANES, LANES),
                                       h_tiles.dtype),
        compiler_params=pltpu.CompilerParams(
            dimension_semantics=("arbitrary",), has_side_effects=True),
        name="moe_dispatch",
    )(dest_flat, h_tiles)


EXPERT_CHUNK = 256
CHUNK_PIECES = tuple(EXPERT_CHUNK >> i for i in range(EXPERT_CHUNK.bit_length()))


def _experts_kernel(start_ref, count_ref, next_ref, first_ref,
                    xs_ref, wg_ref, wu_ref, wd_ref, ys_ref,
                    xbuf, ybuf, sem_x, sem_y, wg_bf, wu_bf, wd_bf, state):
    e = pl.program_id(0)
    ch = EXPERT_CHUNK
    count = count_ref[e]
    start = start_ref[e]

    @pl.when(e == 0)
    def _():
        state[0] = 0
        state[1] = 0
        state[2] = 0

    def fetch(row, slot):
        first = pl.multiple_of(row * SUBLANES, SUBLANES)
        return pltpu.make_async_copy(xs_ref.at[pl.ds(first, ch * SUBLANES)], xbuf.at[slot],
                                     sem_x.at[slot])

    def write_back(row, slot, rows, wait_only):
        for piece in CHUNK_PIECES:
            off = (rows // (2 * piece)) * (2 * piece)

            @pl.when((rows & piece) != 0)
            def _():
                src = pl.multiple_of(off * SUBLANES, SUBLANES)
                dst = pl.multiple_of((row + off) * SUBLANES, SUBLANES)
                copy = pltpu.make_async_copy(
                    ybuf.at[slot, pl.ds(src, piece * SUBLANES)],
                    ys_ref.at[pl.ds(dst, piece * SUBLANES)], sem_y.at[slot])
                if wait_only:
                    copy.wait()
                else:
                    copy.start()

    @pl.when(count > 0)
    def _():
        wg_bf[...] = wg_ref[...].astype(BF16)
        wu_bf[...] = wu_ref[...].astype(BF16)
        wd_bf[...] = wd_ref[...].astype(BF16)
        done = state[0]
        n_chunks = (count + ch - 1) // ch

        @pl.when(e == first_ref[0])
        def _():
            fetch(start, done % 2).start()

        def body(j, carry):
            slot = (done + j) % 2
            row = start + j * ch
            more = j + 1 < n_chunks
            following = next_ref[e]

            @pl.when(more)
            def _():
                fetch(row + ch, 1 - slot).start()

            @pl.when(jnp.logical_and(jnp.logical_not(more), following >= 0))
            def _():
                fetch(start_ref[jnp.maximum(following, 0)], 1 - slot).start()

            fetch(row, slot).wait()
            x = _rows_from_tiles(xbuf.at[slot], ch).astype(BF16)
            a = _silu(_dot(x, wg_bf[...])) * _dot(x, wu_bf[...])
            y = _dot(a.astype(BF16), wd_bf[...])

            write_back(0, slot, state[1 + slot], wait_only=True)
            _rows_to_tiles(ybuf.at[slot], y)
            rows = jnp.minimum(count - j * ch, ch)
            write_back(row, slot, rows, wait_only=False)
            state[1 + slot] = rows
            return carry

        lax.fori_loop(0, n_chunks, body, 0)
        state[0] = done + n_chunks

    @pl.when(e == pl.num_programs(0) - 1)
    def _():
        for slot in range(2):
            write_back(0, slot, state[1 + slot], wait_only=True)
            state[1 + slot] = 0


def _experts(tables, xs, wg, wu, wd, n_rows):
    d = SUBLANES * LANES
    n_exp, _, hid = wg.shape
    wmap = lambda e, start, count, following, first: (e, 0, 0)
    chunk_tiles = EXPERT_CHUNK * SUBLANES
    return pl.pallas_call(
        _experts_kernel,
        grid_spec=pltpu.PrefetchScalarGridSpec(
            num_scalar_prefetch=len(tables),
            grid=(n_exp,),
            in_specs=[
                pl.BlockSpec(memory_space=pl.ANY),
                pl.BlockSpec((None, d, hid), wmap),
                pl.BlockSpec((None, d, hid), wmap),
                pl.BlockSpec((None, hid, d), wmap),
            ],
            out_specs=pl.BlockSpec(memory_space=pl.ANY),
            scratch_shapes=[
                pltpu.VMEM((2, chunk_tiles, LANES), F32),
                pltpu.VMEM((2, chunk_tiles, LANES), F32),
                pltpu.SemaphoreType.DMA((2,)),
                pltpu.SemaphoreType.DMA((2,)),
                pltpu.VMEM((d, hid), BF16), pltpu.VMEM((d, hid), BF16),
                pltpu.VMEM((hid, d), BF16),
                pltpu.SMEM((3,), jnp.int32),
            ],
        ),
        out_shape=jax.ShapeDtypeStruct((n_rows * SUBLANES, LANES), xs.dtype),
        compiler_params=pltpu.CompilerParams(
            dimension_semantics=("arbitrary",), vmem_limit_bytes=VMEM_LIMIT,
            has_side_effects=True),
        name="moe_experts",
    )(*tables, xs, wg, wu, wd)


def _combine_kernel(dest_ref, h_ref, gw_ref, ys_ref, wgs_ref, wus_ref, wds_ref, g2_ref, b2_ref,
                    o_ref, buf, sem, *, n_tok):
    tm = h_ref.shape[0] // 2
    step = pl.program_id(0)

    def gather(tile, into):
        base = tile * tm
        for r in range(tm):
            for k in range(TOP_K):
                slot = pl.multiple_of(dest_ref[k * n_tok + base + r], SUBLANES)
                pltpu.make_async_copy(ys_ref.at[pl.ds(slot, SUBLANES)],
                                      buf.at[into, k, pl.ds(r * SUBLANES, SUBLANES)],
                                      sem.at[into]).start(priority=k % 2)

    def finish(half):
        rows = slice(half * tm, (half + 1) * tm)
        h = h_ref[rows, :]
        hb = h.astype(BF16)
        shared = _dot((_silu(_dot(hb, wgs_ref[...])) * _dot(hb, wus_ref[...])).astype(BF16),
                      wds_ref[...])
        acc = ALPHA * h + shared
        for k in range(TOP_K):
            pltpu.make_async_copy(ys_ref.at[pl.ds(0, tm * SUBLANES)], buf.at[half, k],
                                  sem.at[half]).wait()
        gw = gw_ref[rows, :]
        for k in range(TOP_K):
            acc = acc + _rows_from_tiles(buf.at[half, k], tm) * gw[:, k:k + 1]
        o_ref[rows, :] = _ln(acc, g2_ref[...], b2_ref[...])

    @pl.when(step == 0)
    def _():
        gather(0, 0)

    gather(2 * step + 1, 1)
    finish(0)

    @pl.when(step + 1 < pl.num_programs(0))
    def _():
        gather(2 * step + 2, 0)

    finish(1)


def _combine(dest_flat, h1, gw, ys, wgs, wus, wds, g2, b2, tm):
    t, d = h1.shape
    row = lambda i, dest: (i, 0)
    const = lambda i, dest: (0, 0)
    full = lambda a: pl.BlockSpec(a.shape, const)
    return pl.pallas_call(
        functools.partial(_combine_kernel, n_tok=t),
        grid_spec=pltpu.PrefetchScalarGridSpec(
            num_scalar_prefetch=1,
            grid=(t // (2 * tm),),
            in_specs=[
                pl.BlockSpec((2 * tm, d), row),
                pl.BlockSpec((2 * tm, TOP_K), row),
                pl.BlockSpec(memory_space=pl.ANY),
                full(wgs), full(wus), full(wds), full(g2), full(b2),
            ],
            out_specs=pl.BlockSpec((2 * tm, d), row),
            scratch_shapes=[pltpu.VMEM((2, TOP_K, tm * SUBLANES, LANES), ys.dtype),
                            pltpu.SemaphoreType.DMA((2,))],
        ),
        out_shape=jax.ShapeDtypeStruct((t, d), F32),
        compiler_params=pltpu.CompilerParams(
            dimension_semantics=("arbitrary",), vmem_limit_bytes=VMEM_LIMIT),
        name="moe_combine",
    )(dest_flat, h1, gw, ys, wgs, wus, wds, g2, b2)


def _expert_tables(counts):
    n = counts.shape[0]
    start = (jnp.cumsum(counts) - counts).astype(jnp.int32)
    ids = jnp.arange(n, dtype=jnp.int32)
    later_busy = jnp.logical_and(ids[None, :] > ids[:, None], counts[None, :] > 0)
    following = jnp.min(jnp.where(later_busy, ids[None, :], n), axis=1)
    following = jnp.where(following >= n, -1, following).astype(jnp.int32)
    first = jnp.min(jnp.where(counts > 0, ids, n)).reshape(1).astype(jnp.int32)
    return start, counts.astype(jnp.int32), following, first


def _rope_tables(positions):
    inv = 1.0 / (ROPE_THETA ** (jnp.arange(0, HEAD_DIM, 2, dtype=F32) / HEAD_DIM))
    ang = positions.astype(F32)[:, None] * inv[None, :]
    cos = jnp.cos(ang)
    sin = jnp.sin(ang)
    cos = jnp.concatenate([cos, cos], axis=-1)
    sin = jnp.concatenate([-sin, sin], axis=-1)
    reps = V_DIM // HEAD_DIM
    return jnp.tile(cos, (1, reps)), jnp.tile(sin, (1, reps)), cos.T, sin.T


def _tile(n, pref):
    return pref if n % pref == 0 else n


def kernel(x, meta_tokens, ln_in_g, ln_in_b, w_in, lambda_q1, lambda_k1, lambda_q2, lambda_k2, subln_g, w_attn_o, conv_w, conv_b, conv_ln_g, conv_ln_b, w_conv_o, b_conv_o, w_out, ln1_g, ln1_b, w_router, router_bias, w_gate_e, w_up_e, w_down_e, w_gate_s, w_up_s, w_down_s, ln2_g, ln2_b):
    bsz, seq, d = x.shape
    t = bsz * seq
    l = 0
    row = lambda a: a.reshape(1, -1).astype(F32)
    x2 = x.reshape(t, d)
    lg, lb = row(ln_in_g), row(ln_in_b)
    w_in_bf = w_in[l].astype(BF16)
    o_v = 2 * QK_WIDTH
    wqt = w_in_bf[:, :QK_WIDTH].T
    wk = w_in_bf[:, QK_WIDTH:o_v]
    wvt = w_in_bf[:, o_v:o_v + V_WIDTH].T
    wr = w_in_bf[:, o_v + V_WIDTH:]

    tables_x = _rope_tables(jnp.arange(N_META, N_META + seq))
    tables_m = _rope_tables(jnp.arange(N_META))

    tq = _tile(seq, 512)
    tk = _tile(seq, 256)
    qt, k, vt, u, ga, gc = _inproj(x2, lg, lb, wqt, wk, wvt, wr, tables_x, tq, tk)
    _, km, vmt, um, _, _ = _inproj(meta_tokens.astype(F32), lg, lb, wqt, wk, wvt, wr, tables_m,
                                   N_META, N_META)

    lam = (jnp.exp(jnp.sum(lambda_q1[l].astype(F32) * lambda_k1[l].astype(F32)))
           - jnp.exp(jnp.sum(lambda_q2[l].astype(F32) * lambda_k2[l].astype(F32)))
           + LAM_INIT).reshape(1, 1)
    o_attn = _attention(lam, qt, k, vt, km, vmt.reshape(V_WIDTH, N_META),
                        subln_g[l].reshape(-1, 1).astype(F32), bsz, seq, tq, tk)

    z = _conv(u, um, conv_w[l].astype(F32), row(conv_b[l]), row(conv_ln_g[l]),
              row(conv_ln_b[l]), bsz, seq)

    assert d == SUBLANES * LANES, "the row-tile layout assumes one (8, 128) tile per token row"
    h1, h1_tiles, top_e, gw_t, rank, counts = _merge_route(
        x2, lg, lb, o_attn, z, ga, gc,
        w_attn_o[l].astype(BF16), w_conv_o[l].astype(BF16), row(b_conv_o[l]),
        w_out[l].astype(BF16), row(ln1_g[l]), row(ln1_b[l]),
        w_router[l].T.astype(BF16), router_bias[l].reshape(-1, 1).astype(F32),
        _tile(t, 256))

    tables = _expert_tables(counts.reshape(-1).astype(jnp.int32))
    start = tables[0]
    dest = _dest(top_e, rank, start.astype(F32).reshape(-1, 1), _tile(t, 512)).reshape(-1)

    xs = _dispatch(dest, h1_tiles, _tile(t, 128))
    ys = _experts(tables, xs, w_gate_e[l], w_up_e[l], w_down_e[l], t * TOP_K)
    out = _combine(dest, h1, gw_t.T, ys,
                   w_gate_s[l].astype(BF16), w_up_s[l].astype(BF16), w_down_s[l].astype(BF16),
                   row(ln2_g[l]), row(ln2_b[l]), _tile(t, 128))
    return out.reshape(bsz, seq, d)
```

```python
import functools
import math

import jax
import jax.numpy as jnp
from jax import lax
from jax.experimental import pallas as pl
from jax.experimental.pallas import tpu as pltpu

N_META = 16
HEADS = 4
HEAD_DIM = 64
V_DIM = 2 * HEAD_DIM
QK_WIDTH = HEADS * 2 * HEAD_DIM
V_WIDTH = HEADS * V_DIM
ROPE_THETA = 10000.0
CONV_WIDTH = 31
N_EXPERTS = 256
TOP_K = 8
N_GROUPS = 8
TOPK_GROUPS = 4
GROUP_SIZE = N_EXPERTS // N_GROUPS
ROUTED_SCALE = 2.5
LN_EPS = 1e-5
DEPTH = 1
ALPHA = (2.0 * DEPTH) ** 0.25
LAM_INIT = 0.8 - 0.6 * math.exp(-0.3 * 0)

F32 = jnp.float32
BF16 = jnp.bfloat16
NEG_INF = float("-inf")

VMEM_LIMIT = 56 * 1024 * 1024


def _ln(x, g, b):
    mu = jnp.mean(x, axis=-1, keepdims=True)
    xc = x - mu
    var = jnp.mean(xc * xc, axis=-1, keepdims=True)
    return xc * lax.rsqrt(var + LN_EPS) * g + b


def _dot(a, b):
    return jnp.dot(a, b, preferred_element_type=F32)


def _dot_nt(a, b):
    return lax.dot_general(a, b, (((1,), (1,)), ((), ())), preferred_element_type=F32)


def _silu(x):
    return x * jax.nn.sigmoid(x)


SUBLANES = 8
LANES = 128


def _rows_from_tiles(ref, n_rows, first=0):
    return jnp.concatenate(
        [ref[pl.ds(first * SUBLANES + c, n_rows, stride=SUBLANES), :] for c in range(SUBLANES)],
        axis=1)


def _rows_to_tiles(ref, x):
    for c in range(SUBLANES):
        ref[pl.ds(c, x.shape[0], stride=SUBLANES), :] = x[:, c * LANES:(c + 1) * LANES]


LOG2E = math.log2(math.e)
HALF_HEAD = HEAD_DIM // 2


def _inproj_kernel(x_ref, g_ref, b_ref, wqt_ref, wk_ref, wvt_ref, wr_ref,
                   cos_ref, sin_ref, cost_ref, sint_ref,
                   qt_ref, k_ref, vt_ref, u_ref, ga_ref, gc_ref, *, d_model):
    h = _ln(x_ref[...], g_ref[...], b_ref[...]).astype(BF16)
    tm = x_ref.shape[0]
    c = d_model // 2

    qt = _dot_nt(wqt_ref[...], h)
    pieces = []
    for g in range(QK_WIDTH // HEAD_DIM):
        lo = g * HEAD_DIM
        pieces.append(qt[lo + HALF_HEAD:lo + HEAD_DIM, :])
        pieces.append(qt[lo:lo + HALF_HEAD, :])
    partner = jnp.concatenate(pieces, axis=0)
    reps_t = QK_WIDTH // cost_ref.shape[0]
    qt = qt * jnp.tile(cost_ref[...], (reps_t, 1)) + partner * jnp.tile(sint_ref[...], (reps_t, 1))
    qt_ref[...] = (qt * (HEAD_DIM ** -0.5 * LOG2E)).astype(BF16)

    kk = _dot(h, wk_ref[...])
    reps = QK_WIDTH // cos_ref.shape[1]
    lane = lax.broadcasted_iota(jnp.int32, (tm, QK_WIDTH), 1)
    partner = jnp.where((lane % HEAD_DIM) < HALF_HEAD,
                        pltpu.roll(kk, QK_WIDTH - HALF_HEAD, 1),
                        pltpu.roll(kk, HALF_HEAD, 1))
    kk = kk * jnp.tile(cos_ref[...], (1, reps)) + partner * jnp.tile(sin_ref[...], (1, reps))
    k_ref[...] = kk.astype(BF16)

    vt = _dot_nt(wvt_ref[...], h).astype(BF16)
    chunk = vt_ref.shape[2]
    for j in range(vt_ref.shape[0]):
        vt_ref[j] = vt[:, j * chunk:(j + 1) * chunk]

    def proj(lo, width):
        return _dot(h, wr_ref[:, lo:lo + width])

    u_ref[...] = proj(0, c) * jax.nn.sigmoid(proj(c, c))
    ga_ref[...] = jax.nn.sigmoid(proj(2 * c, d_model)).astype(BF16)
    gc_ref[...] = jax.nn.sigmoid(proj(2 * c + d_model, d_model)).astype(BF16)


def _inproj(x2, g, b, wqt, wk, wvt, wr, tables, tm, chunk):
    t, d = x2.shape
    c = d // 2
    cos, sin, cost, sint = tables
    n_pos_tiles = cos.shape[0] // tm
    row = lambda i: (i, 0)
    col = lambda i: (0, i)
    const = lambda i: (0, 0)
    pos = lambda i: (i % n_pos_tiles, 0)
    post = lambda i: (0, i % n_pos_tiles)
    full = lambda a: pl.BlockSpec(a.shape, const)
    return pl.pallas_call(
        functools.partial(_inproj_kernel, d_model=d),
        grid=(t // tm,),
        in_specs=[
            pl.BlockSpec((tm, d), row), full(g), full(b),
            full(wqt), full(wk), full(wvt), full(wr),
            pl.BlockSpec((tm, cos.shape[1]), pos),
            pl.BlockSpec((tm, cos.shape[1]), pos),
            pl.BlockSpec((cost.shape[0], tm), post),
            pl.BlockSpec((cost.shape[0], tm), post),
        ],
        out_specs=[
            pl.BlockSpec((QK_WIDTH, tm), col),
            pl.BlockSpec((tm, QK_WIDTH), row),
            pl.BlockSpec((tm // chunk, V_WIDTH, chunk), lambda i: (i, 0, 0)),
            pl.BlockSpec((tm, c), row),
            pl.BlockSpec((tm, d), row),
            pl.BlockSpec((tm, d), row),
        ],
        out_shape=[
            jax.ShapeDtypeStruct((QK_WIDTH, t), BF16),
            jax.ShapeDtypeStruct((t, QK_WIDTH), BF16),
            jax.ShapeDtypeStruct((t // chunk, V_WIDTH, chunk), BF16),
            jax.ShapeDtypeStruct((t, c), F32),
            jax.ShapeDtypeStruct((t, d), BF16),
            jax.ShapeDtypeStruct((t, d), BF16),
        ],
        compiler_params=pltpu.CompilerParams(
            dimension_semantics=("arbitrary",), vmem_limit_bytes=VMEM_LIMIT),
        name="inproj",
    )(x2, g, b, wqt, wk, wvt, wr, cos, sin, cost, sint)


def _attn_kernel(lam_ref, qt_ref, k_ref, vt_ref, km_ref, vmt_ref, g_ref, o_ref, acc_sc, *, tq, tk):
    i = pl.program_id(2)
    qt = qt_ref[...]
    feat = lax.broadcasted_iota(jnp.int32, qt.shape, 0)
    zero = jnp.zeros_like(qt)
    q2 = jnp.concatenate([jnp.where(feat < HEAD_DIM, qt, zero),
                          jnp.where(feat >= HEAD_DIM, qt, zero)], axis=1)

    def scores(j):
        r0 = pl.multiple_of(j * tk, tk)
        return _dot(k_ref[pl.ds(r0, tk), :], q2)

    def absorb(m_old, l_old, s, vtc):
        m_new = jnp.maximum(m_old, jnp.max(s, axis=0, keepdims=True))
        alpha = jnp.exp2(m_old - m_new)
        p = jnp.exp2(s - m_new)
        acc_sc[...] = alpha * acc_sc[...] + _dot(vtc, p.astype(BF16))
        return m_new, alpha * l_old + jnp.sum(p, axis=0, keepdims=True)

    s = _dot(km_ref[...], q2)
    m = jnp.max(s, axis=0, keepdims=True)
    p = jnp.exp2(s - m)
    acc_sc[...] = _dot(vmt_ref[...], p.astype(BF16))
    l = jnp.sum(p, axis=0, keepdims=True)

    per_tile = tq // tk
    n_full = i * per_tile

    def body(j, carry):
        m_old, l_old, s_cur = carry
        s_next = scores(j + 1)
        m_new, l_new = absorb(m_old, l_old, s_cur, vt_ref[j])
        return m_new, l_new, s_next

    m, l, s = lax.fori_loop(0, n_full, body, (m, l, scores(0)))

    key = lax.broadcasted_iota(jnp.int32, (tk, 2 * tq), 0)
    qry = lax.broadcasted_iota(jnp.int32, (tk, 2 * tq), 1) % tq
    for d in range(per_tile):
        s_cur = s
        if d + 1 < per_tile:
            s = scores(n_full + d + 1)
        m, l = absorb(m, l, jnp.where(key + d * tk <= qry, s_cur, NEG_INF), vt_ref[n_full + d])

    lam = lam_ref[0, 0]
    o = acc_sc[:, :tq] / l[:, :tq] - lam * (acc_sc[:, tq:] / l[:, tq:])
    o = o * lax.rsqrt(jnp.mean(o * o, axis=0, keepdims=True) + LN_EPS) * g_ref[...]
    o_ref[...] = (o * (1.0 - LAM_INIT)).T.astype(o_ref.dtype)


def _attention(lam, qt, k, vt, km, vmt, subln_col, bsz, seq, tq, tk):
    t = k.shape[0]
    nq = seq // tq
    nk = seq // tk
    return pl.pallas_call(
        functools.partial(_attn_kernel, tq=tq, tk=tk),
        grid=(bsz, HEADS, nq),
        in_specs=[
            pl.BlockSpec(memory_space=pltpu.SMEM),
            pl.BlockSpec((V_DIM, tq), lambda b, h, i: (h, b * nq + i)),
            pl.BlockSpec((seq, V_DIM), lambda b, h, i: (b, h)),
            pl.BlockSpec((nk, V_DIM, tk), lambda b, h, i: (b, h, 0)),
            pl.BlockSpec((N_META, V_DIM), lambda b, h, i: (0, h)),
            pl.BlockSpec((V_DIM, N_META), lambda b, h, i: (h, 0)),
            pl.BlockSpec((V_DIM, 1), lambda b, h, i: (0, 0)),
        ],
        out_specs=pl.BlockSpec((tq, V_DIM), lambda b, h, i: (b * nq + i, h)),
        out_shape=jax.ShapeDtypeStruct((t, V_WIDTH), BF16),
        scratch_shapes=[pltpu.VMEM((V_DIM, 2 * tq), F32)],
        compiler_params=pltpu.CompilerParams(
            dimension_semantics=("arbitrary", "arbitrary", "arbitrary"),
            vmem_limit_bytes=VMEM_LIMIT),
        name="diff_attention",
    )(lam, qt, k, vt, km, vmt, subln_col)


CONV_ROWS = 32
CONV_WINDOW = 64
CONV_PAD = 32


def _conv_kernel(u_ref, um_ref, w_ref, cb_ref, g_ref, b_ref, z_ref, ucat, grp, *, seq):
    c = u_ref.shape[1]
    ucat[0:CONV_PAD - N_META, :] = jnp.zeros((CONV_PAD - N_META, c), F32)
    ucat[CONV_PAD - N_META:CONV_PAD, :] = um_ref[...]
    ucat[CONV_PAD:CONV_PAD + seq, :] = u_ref[...]
    shift = CONV_PAD - (CONV_WIDTH - 1)

    def body(t, carry):
        r0 = pl.multiple_of(t * CONV_ROWS, CONV_ROWS)
        acc = jnp.zeros((CONV_ROWS, c), F32)
        for r in range(SUBLANES):
            taps = [j for j in range(CONV_WIDTH) if (shift + j) % SUBLANES == r]
            if not taps:
                continue
            lo = shift + taps[0]
            rows = taps[-1] - taps[0] + CONV_ROWS
            start = pl.multiple_of(r0 + lo - r, SUBLANES)
            span = rows + (SUBLANES if r else 0)
            grp[r, 0:rows, :] = ucat[pl.ds(start, span), :][r:r + rows, :]
            for j in taps:
                off = shift + j - lo
                acc = acc + grp[r, off:off + CONV_ROWS, :] * w_ref[j:j + 1, :]
        y = _ln(acc + cb_ref[...], g_ref[...], b_ref[...])
        z_ref[pl.ds(r0, CONV_ROWS), :] = _silu(y).astype(z_ref.dtype)
        return carry

    lax.fori_loop(0, seq // CONV_ROWS, body, 0)


def _conv(u, um, conv_w, conv_b, ln_g, ln_b, bsz, seq):
    t, c = u.shape
    const = lambda b: (0, 0)
    return pl.pallas_call(
        functools.partial(_conv_kernel, seq=seq),
        grid=(bsz,),
        in_specs=[
            pl.BlockSpec((seq, c), lambda b: (b, 0)),
            pl.BlockSpec((N_META, c), const),
            pl.BlockSpec((CONV_WIDTH, c), const),
            pl.BlockSpec((1, c), const),
            pl.BlockSpec((1, c), const),
            pl.BlockSpec((1, c), const),
        ],
        out_specs=pl.BlockSpec((seq, c), lambda b: (b, 0)),
        out_shape=jax.ShapeDtypeStruct((t, c), BF16),
        scratch_shapes=[pltpu.VMEM((CONV_PAD + seq, c), F32),
                        pltpu.VMEM((SUBLANES, CONV_WINDOW, c), F32)],
        compiler_params=pltpu.CompilerParams(
            dimension_semantics=("arbitrary",), vmem_limit_bytes=VMEM_LIMIT),
        name="conformer_conv",
    )(u, um, conv_w, conv_b, ln_g, ln_b)


def _merge_route_kernel(x_ref, lg_ref, lb_ref, oa_ref, z_ref, ga_ref, gc_ref,
                        wao_ref, wco_ref, bco_ref, wout_ref, g1_ref, b1_ref,
                        wrt_ref, rb_ref,
                        h1_ref, h1t_ref, te_ref, gw_ref, rk_ref, cnt_ref, carry_sc):
    tm = x_ref.shape[0]

    @pl.when(pl.program_id(0) == 0)
    def _():
        carry_sc[...] = jnp.zeros(carry_sc.shape, F32)

    h0 = _ln(x_ref[...], lg_ref[...], lb_ref[...])
    y_attn = _dot(oa_ref[...], wao_ref[...])
    y_conv = _dot(z_ref[...], wco_ref[...]) + bco_ref[...]
    merged = ga_ref[...].astype(F32) * y_attn + gc_ref[...].astype(F32) * y_conv
    h1 = _ln(ALPHA * h0 + _dot(merged.astype(BF16), wout_ref[...]), g1_ref[...], b1_ref[...])
    h1_ref[...] = h1
    _rows_to_tiles(h1t_ref, h1)
    h1b = h1.astype(BF16)

    scores = jax.nn.sigmoid(_dot_nt(wrt_ref[...], h1b))
    sel = scores + rb_ref[...]

    sub = lax.broadcasted_iota(jnp.int32, (GROUP_SIZE, tm), 0)
    gscore = []
    for g in range(N_GROUPS):
        slab = sel[g * GROUP_SIZE:(g + 1) * GROUP_SIZE, :]
        top1 = jnp.max(slab, axis=0, keepdims=True)
        arg1 = jnp.min(jnp.where(slab == top1, sub, GROUP_SIZE), axis=0, keepdims=True)
        top2 = jnp.max(jnp.where(sub == arg1, NEG_INF, slab), axis=0, keepdims=True)
        gscore.append(top1 + top2)

    eiota = lax.broadcasted_iota(jnp.int32, (N_EXPERTS, tm), 0)
    egroup = eiota // GROUP_SIZE
    allowed = jnp.zeros((N_EXPERTS, tm), F32)
    for _ in range(TOPK_GROUPS):
        best = functools.reduce(jnp.maximum, gscore)
        pick = jnp.full((1, tm), N_GROUPS, jnp.int32)
        for g in reversed(range(N_GROUPS)):
            pick = jnp.where(gscore[g] == best, g, pick)
        gscore = [jnp.where(pick == g, NEG_INF, gscore[g]) for g in range(N_GROUPS)]
        allowed = jnp.where(egroup == pick, 1.0, allowed)

    masked = jnp.where(allowed > 0.5, sel, NEG_INF)

    picks = []
    weights = []
    chosen = jnp.zeros((N_EXPERTS, tm), F32)
    for _ in range(TOP_K):
        best = jnp.max(masked, axis=0, keepdims=True)
        pick = jnp.min(jnp.where(masked == best, eiota, N_EXPERTS), axis=0, keepdims=True)
        hit = eiota == pick
        picks.append(pick)
        weights.append(jnp.sum(jnp.where(hit, scores, 0.0), axis=0, keepdims=True))
        chosen = jnp.where(hit, 1.0, chosen)
        masked = jnp.where(hit, NEG_INF, masked)

    gw = jnp.concatenate(weights, axis=0)
    gw_ref[...] = gw / jnp.sum(gw, axis=0, keepdims=True) * ROUTED_SCALE
    te_ref[...] = jnp.concatenate(picks, axis=0)

    before = (lax.broadcasted_iota(jnp.int32, (tm, tm), 0)
              < lax.broadcasted_iota(jnp.int32, (tm, tm), 1))
    prior = _dot(chosen.astype(BF16), jnp.where(before, 1.0, 0.0).astype(BF16)) + carry_sc[...]
    ranks = [jnp.sum(jnp.where(eiota == p, prior, 0.0), axis=0, keepdims=True) for p in picks]
    rk_ref[...] = jnp.concatenate(ranks, axis=0).astype(jnp.int32)
    carry_sc[...] = carry_sc[...] + jnp.sum(chosen, axis=1, keepdims=True)
    cnt_ref[...] = carry_sc[...]


def _merge_route(x2, lg, lb, oa, z, ga, gc, wao, wco, bco, wout, g1, b1, wrt, rb, tm):
    t, d = x2.shape
    c = d // 2
    row = lambda i: (i, 0)
    col = lambda i: (0, i)
    const = lambda i: (0, 0)
    full = lambda a: pl.BlockSpec(a.shape, const)
    return pl.pallas_call(
        _merge_route_kernel,
        grid=(t // tm,),
        in_specs=[
            pl.BlockSpec((tm, d), row), full(lg), full(lb),
            pl.BlockSpec((tm, V_WIDTH), row), pl.BlockSpec((tm, c), row),
            pl.BlockSpec((tm, d), row), pl.BlockSpec((tm, d), row),
            full(wao), full(wco), full(bco), full(wout), full(g1), full(b1),
            full(wrt), full(rb),
        ],
        out_specs=[
            pl.BlockSpec((tm, d), row),
            pl.BlockSpec((tm * SUBLANES, LANES), row),
            pl.BlockSpec((TOP_K, tm), col),
            pl.BlockSpec((TOP_K, tm), col),
            pl.BlockSpec((TOP_K, tm), col),
            pl.BlockSpec((N_EXPERTS, 1), const),
        ],
        out_shape=[
            jax.ShapeDtypeStruct((t, d), F32),
            jax.ShapeDtypeStruct((t * SUBLANES, LANES), F32),
            jax.ShapeDtypeStruct((TOP_K, t), jnp.int32),
            jax.ShapeDtypeStruct((TOP_K, t), F32),
            jax.ShapeDtypeStruct((TOP_K, t), jnp.int32),
            jax.ShapeDtypeStruct((N_EXPERTS, 1), F32),
        ],
        scratch_shapes=[pltpu.VMEM((N_EXPERTS, 1), F32)],
        compiler_params=pltpu.CompilerParams(
            dimension_semantics=("arbitrary",), vmem_limit_bytes=VMEM_LIMIT),
        name="merge_route",
    )(x2, lg, lb, oa, z, ga, gc, wao, wco, bco, wout, g1, b1, wrt, rb)


def _dest_kernel(te_ref, rk_ref, start_ref, d_ref):
    tm = te_ref.shape[1]
    eiota = lax.broadcasted_iota(jnp.int32, (N_EXPERTS, tm), 0)
    start = start_ref[...]
    rows = [jnp.sum(jnp.where(eiota == te_ref[k:k + 1, :], start, 0.0), axis=0, keepdims=True)
            for k in range(TOP_K)]
    d_ref[...] = (jnp.concatenate(rows, axis=0).astype(jnp.int32) + rk_ref[...]) * SUBLANES


def _dest(top_e, rank, start_col, tm):
    t = top_e.shape[1]
    col = lambda i: (0, i)
    return pl.pallas_call(
        _dest_kernel,
        grid=(t // tm,),
        in_specs=[
            pl.BlockSpec((TOP_K, tm), col),
            pl.BlockSpec((TOP_K, tm), col),
            pl.BlockSpec((N_EXPERTS, 1), lambda i: (0, 0)),
        ],
        out_specs=pl.BlockSpec((TOP_K, tm), col),
        out_shape=jax.ShapeDtypeStruct((TOP_K, t), jnp.int32),
        compiler_params=pltpu.CompilerParams(dimension_semantics=("arbitrary",)),
        name="moe_dest",
    )(top_e, rank, start_col)


def _dispatch_kernel(dest_ref, h_ref, xs_ref, zeros, sem, pad_sem, *, n_tok):
    tm = h_ref.shape[0] // SUBLANES
    base = pl.program_id(0) * tm

    @pl.when(pl.program_id(0) == 0)
    def _():
        zeros[...] = jnp.zeros(zeros.shape, zeros.dtype)
        pad = pltpu.make_async_copy(
            zeros, xs_ref.at[pl.ds(n_tok * TOP_K * SUBLANES, zeros.shape[0])], pad_sem)
        pad.start()
        pad.wait()

    for r in range(tm):
        for k in range(TOP_K):
            slot = pl.multiple_of(dest_ref[k * n_tok + base + r], SUBLANES)
            pltpu.make_async_copy(h_ref.at[pl.ds(r * SUBLANES, SUBLANES)],
                                  xs_ref.at[pl.ds(slot, SUBLANES)], sem).start(priority=k % 2)
    for k in range(TOP_K):
        pltpu.make_async_copy(h_ref, xs_ref.at[pl.ds(0, tm * SUBLANES)], sem).wait()


def _dispatch(dest_flat, h_tiles, tm):
    t = h_tiles.shape[0] // SUBLANES
    return pl.pallas_call(
        functools.partial(_dispatch_kernel, n_tok=t),
        grid_spec=pltpu.PrefetchScalarGridSpec(
            num_scalar_prefetch=1,
            grid=(t // tm,),
            in_specs=[pl.BlockSpec((tm * SUBLANES, LANES), lambda i, dest: (i, 0))],
            out_specs=pl.BlockSpec(memory_space=pl.ANY),
            scratch_shapes=[pltpu.VMEM((EXPERT_CHUNK * SUBLANES, LANES), h_tiles.dtype),
                            pltpu.SemaphoreType.DMA, pltpu.SemaphoreType.DMA],
        ),
        out_shape=jax.ShapeDtypeStruct(((t * TOP_K + EXPERT_CHUNK) * SUBLANES, LANES),
                                       h_tiles.dtype),
        compiler_params=pltpu.CompilerParams(
            dimension_semantics=("arbitrary",), has_side_effects=True),
        name="moe_dispatch",
    )(dest_flat, h_tiles)


EXPERT_CHUNK = 256
CHUNK_PIECES = tuple(EXPERT_CHUNK >> i for i in range(EXPERT_CHUNK.bit_length()))


def _experts_kernel(start_ref, count_ref, next_ref, first_ref,
                    xs_ref, wg_ref, wu_ref, wd_ref, ys_ref,
                    xbuf, ybuf, sem_x, sem_y, wg_bf, wu_bf, wd_bf, state):
    e = pl.program_id(0)
    ch = EXPERT_CHUNK
    count = count_ref[e]
    start = start_ref[e]

    @pl.when(e == 0)
    def _():
        state[0] = 0
        state[1] = 0
        state[2] = 0

    class fetch:
        def __init__(self, row, slot):
            self.copies = [
                pltpu.make_async_copy(xs_ref.at[pl.ds(row, ch), c],
                                      xbuf.at[slot, :, pl.ds(c * LANES, LANES)], sem_x.at[slot])
                for c in range(SUBLANES)]

        def start(self):
            for copy in self.copies:
                copy.start()

        def wait(self):
            for copy in self.copies:
                copy.wait()

    def write_back(row, slot, rows, wait_only):
        for piece in CHUNK_PIECES:
            off = pl.multiple_of((rows // (2 * piece)) * (2 * piece), piece)

            @pl.when((rows & piece) != 0)
            def _():
                for c in range(SUBLANES):
                    copy = pltpu.make_async_copy(
                        ybuf.at[slot, pl.ds(off, piece), pl.ds(c * LANES, LANES)],
                        ys_ref.at[pl.ds(row + off, piece), c], sem_y.at[slot])
                    if wait_only:
                        copy.wait()
                    else:
                        copy.start()

    @pl.when(count > 0)
    def _():
        wg_bf[...] = wg_ref[...].astype(BF16)
        wu_bf[...] = wu_ref[...].astype(BF16)
        wd_bf[...] = wd_ref[...].astype(BF16)
        done = state[0]
        n_chunks = (count + ch - 1) // ch

        @pl.when(e == first_ref[0])
        def _():
            fetch(start, done % 2).start()

        def body(j, carry):
            slot = (done + j) % 2
            row = start + j * ch
            more = j + 1 < n_chunks
            following = next_ref[e]

            @pl.when(more)
            def _():
                fetch(row + ch, 1 - slot).start()

            @pl.when(jnp.logical_and(jnp.logical_not(more), following >= 0))
            def _():
                fetch(start_ref[jnp.maximum(following, 0)], 1 - slot).start()

            fetch(row, slot).wait()
            x = xbuf[slot].astype(BF16)
            a = _silu(_dot(x, wg_bf[...])) * _dot(x, wu_bf[...])
            y = _dot(a.astype(BF16), wd_bf[...])

            write_back(0, slot, state[1 + slot], wait_only=True)
            ybuf[slot] = y
            rows = jnp.minimum(count - j * ch, ch)
            write_back(row, slot, rows, wait_only=False)
            state[1 + slot] = rows
            return carry

        lax.fori_loop(0, n_chunks, body, 0)
        state[0] = done + n_chunks

    @pl.when(e == pl.num_programs(0) - 1)
    def _():
        for slot in range(2):
            write_back(0, slot, state[1 + slot], wait_only=True)
            state[1 + slot] = 0


def _experts(tables, xs, wg, wu, wd, n_rows):
    d = SUBLANES * LANES
    n_exp, _, hid = wg.shape
    wmap = lambda e, start, count, following, first: (e, 0, 0)
    return pl.pallas_call(
        _experts_kernel,
        grid_spec=pltpu.PrefetchScalarGridSpec(
            num_scalar_prefetch=len(tables),
            grid=(n_exp,),
            in_specs=[
                pl.BlockSpec(memory_space=pl.ANY),
                pl.BlockSpec((None, d, hid), wmap),
                pl.BlockSpec((None, d, hid), wmap),
                pl.BlockSpec((None, hid, d), wmap),
            ],
            out_specs=pl.BlockSpec(memory_space=pl.ANY),
            scratch_shapes=[
                pltpu.VMEM((2, EXPERT_CHUNK, d), F32),
                pltpu.VMEM((2, EXPERT_CHUNK, d), F32),
                pltpu.SemaphoreType.DMA((2,)),
                pltpu.SemaphoreType.DMA((2,)),
                pltpu.VMEM((d, hid), BF16), pltpu.VMEM((d, hid), BF16),
                pltpu.VMEM((hid, d), BF16),
                pltpu.SMEM((3,), jnp.int32),
            ],
        ),
        out_shape=jax.ShapeDtypeStruct((n_rows, SUBLANES, LANES), xs.dtype),
        compiler_params=pltpu.CompilerParams(
            dimension_semantics=("arbitrary",), vmem_limit_bytes=VMEM_LIMIT,
            has_side_effects=True),
        name="moe_experts",
    )(*tables, xs.reshape(-1, SUBLANES, LANES), wg, wu, wd)


def _combine_kernel(dest_ref, h_ref, gw_ref, ys_ref, wgs_ref, wus_ref, wds_ref, g2_ref, b2_ref,
                    o_ref, buf, sem, *, n_tok):
    tm = h_ref.shape[0] // 2
    step = pl.program_id(0)

    def gather(tile, into):
        base = tile * tm
        for r in range(tm):
            for k in range(TOP_K):
                slot = pl.multiple_of(dest_ref[k * n_tok + base + r], SUBLANES)
                pltpu.make_async_copy(ys_ref.at[pl.ds(slot, SUBLANES)],
                                      buf.at[into, k, pl.ds(r * SUBLANES, SUBLANES)],
                                      sem.at[into]).start(priority=k % 2)

    def finish(half):
        rows = slice(half * tm, (half + 1) * tm)
        h = h_ref[rows, :]
        hb = h.astype(BF16)
        shared = _dot((_silu(_dot(hb, wgs_ref[...])) * _dot(hb, wus_ref[...])).astype(BF16),
                      wds_ref[...])
        acc = ALPHA * h + shared
        for k in range(TOP_K):
            pltpu.make_async_copy(ys_ref.at[pl.ds(0, tm * SUBLANES)], buf.at[half, k],
                                  sem.at[half]).wait()
        gw = gw_ref[rows, :]
        for k in range(TOP_K):
            acc = acc + _rows_from_tiles(buf.at[half, k], tm) * gw[:, k:k + 1]
        o_ref[rows, :] = _ln(acc, g2_ref[...], b2_ref[...])

    @pl.when(step == 0)
    def _():
        gather(0, 0)

    gather(2 * step + 1, 1)
    finish(0)

    @pl.when(step + 1 < pl.num_programs(0))
    def _():
        gather(2 * step + 2, 0)

    finish(1)


def _combine(dest_flat, h1, gw, ys, wgs, wus, wds, g2, b2, tm):
    t, d = h1.shape
    row = lambda i, dest: (i, 0)
    const = lambda i, dest: (0, 0)
    full = lambda a: pl.BlockSpec(a.shape, const)
    return pl.pallas_call(
        functools.partial(_combine_kernel, n_tok=t),
        grid_spec=pltpu.PrefetchScalarGridSpec(
            num_scalar_prefetch=1,
            grid=(t // (2 * tm),),
            in_specs=[
                pl.BlockSpec((2 * tm, d), row),
                pl.BlockSpec((2 * tm, TOP_K), row),
                pl.BlockSpec(memory_space=pl.ANY),
                full(wgs), full(wus), full(wds), full(g2), full(b2),
            ],
            out_specs=pl.BlockSpec((2 * tm, d), row),
            scratch_shapes=[pltpu.VMEM((2, TOP_K, tm * SUBLANES, LANES), ys.dtype),
                            pltpu.SemaphoreType.DMA((2,))],
        ),
        out_shape=jax.ShapeDtypeStruct((t, d), F32),
        compiler_params=pltpu.CompilerParams(
            dimension_semantics=("arbitrary",), vmem_limit_bytes=VMEM_LIMIT),
        name="moe_combine",
    )(dest_flat, h1, gw, ys, wgs, wus, wds, g2, b2)


def _expert_tables(counts):
    n = counts.shape[0]
    start = (jnp.cumsum(counts) - counts).astype(jnp.int32)
    ids = jnp.arange(n, dtype=jnp.int32)
    later_busy = jnp.logical_and(ids[None, :] > ids[:, None], counts[None, :] > 0)
    following = jnp.min(jnp.where(later_busy, ids[None, :], n), axis=1)
    following = jnp.where(following >= n, -1, following).astype(jnp.int32)
    first = jnp.min(jnp.where(counts > 0, ids, n)).reshape(1).astype(jnp.int32)
    return start, counts.astype(jnp.int32), following, first


def _rope_tables(positions):
    inv = 1.0 / (ROPE_THETA ** (jnp.arange(0, HEAD_DIM, 2, dtype=F32) / HEAD_DIM))
    ang = positions.astype(F32)[:, None] * inv[None, :]
    cos = jnp.cos(ang)
    sin = jnp.sin(ang)
    cos = jnp.concatenate([cos, cos], axis=-1)
    sin = jnp.concatenate([-sin, sin], axis=-1)
    reps = V_DIM // HEAD_DIM
    return jnp.tile(cos, (1, reps)), jnp.tile(sin, (1, reps)), cos.T, sin.T


def _tile(n, pref):
    return pref if n % pref == 0 else n


def kernel(x, meta_tokens, ln_in_g, ln_in_b, w_in, lambda_q1, lambda_k1, lambda_q2, lambda_k2, subln_g, w_attn_o, conv_w, conv_b, conv_ln_g, conv_ln_b, w_conv_o, b_conv_o, w_out, ln1_g, ln1_b, w_router, router_bias, w_gate_e, w_up_e, w_down_e, w_gate_s, w_up_s, w_down_s, ln2_g, ln2_b):
    bsz, seq, d = x.shape
    t = bsz * seq
    l = 0
    row = lambda a: a.reshape(1, -1).astype(F32)
    x2 = x.reshape(t, d)
    lg, lb = row(ln_in_g), row(ln_in_b)
    w_in_bf = w_in[l].astype(BF16)
    o_v = 2 * QK_WIDTH
    wqt = w_in_bf[:, :QK_WIDTH].T
    wk = w_in_bf[:, QK_WIDTH:o_v]
    wvt = w_in_bf[:, o_v:o_v + V_WIDTH].T
    wr = w_in_bf[:, o_v + V_WIDTH:]

    tables_x = _rope_tables(jnp.arange(N_META, N_META + seq))
    tables_m = _rope_tables(jnp.arange(N_META))

    tq = _tile(seq, 512)
    tk = _tile(seq, 256)
    qt, k, vt, u, ga, gc = _inproj(x2, lg, lb, wqt, wk, wvt, wr, tables_x, tq, tk)
    _, km, vmt, um, _, _ = _inproj(meta_tokens.astype(F32), lg, lb, wqt, wk, wvt, wr, tables_m,
                                   N_META, N_META)

    lam = (jnp.exp(jnp.sum(lambda_q1[l].astype(F32) * lambda_k1[l].astype(F32)))
           - jnp.exp(jnp.sum(lambda_q2[l].astype(F32) * lambda_k2[l].astype(F32)))
           + LAM_INIT).reshape(1, 1)
    o_attn = _attention(lam, qt, k, vt, km, vmt.reshape(V_WIDTH, N_META),
                        subln_g[l].reshape(-1, 1).astype(F32), bsz, seq, tq, tk)

    z = _conv(u, um, conv_w[l].astype(F32), row(conv_b[l]), row(conv_ln_g[l]),
              row(conv_ln_b[l]), bsz, seq)

    assert d == SUBLANES * LANES, "the row-tile layout assumes one (8, 128) tile per token row"
    h1, h1_tiles, top_e, gw_t, rank, counts = _merge_route(
        x2, lg, lb, o_attn, z, ga, gc,
        w_attn_o[l].astype(BF16), w_conv_o[l].astype(BF16), row(b_conv_o[l]),
        w_out[l].astype(BF16), row(ln1_g[l]), row(ln1_b[l]),
        w_router[l].T.astype(BF16), router_bias[l].reshape(-1, 1).astype(F32),
        _tile(t, 256))

    tables = _expert_tables(counts.reshape(-1).astype(jnp.int32))
    start = tables[0]
    dest = _dest(top_e, rank, start.astype(F32).reshape(-1, 1), _tile(t, 512)).reshape(-1)

    xs = _dispatch(dest, h1_tiles, _tile(t, 128))
    ys = _experts(tables, xs, w_gate_e[l], w_up_e[l], w_down_e[l], t * TOP_K)
    out = _combine(dest, h1, gw_t.T, ys.reshape(-1, LANES),
                   w_gate_s[l].astype(BF16), w_up_s[l].astype(BF16), w_down_s[l].astype(BF16),
                   row(ln2_g[l]), row(ln2_b[l]), _tile(t, 128))
    return out.reshape(bsz, seq, d)
```

```python
import functools
import math

import jax
import jax.numpy as jnp
from jax import lax
from jax.experimental import pallas as pl
from jax.experimental.pallas import tpu as pltpu

N_META = 16
HEADS = 4
HEAD_DIM = 64
V_DIM = 2 * HEAD_DIM
QK_WIDTH = HEADS * 2 * HEAD_DIM
V_WIDTH = HEADS * V_DIM
ROPE_THETA = 10000.0
CONV_WIDTH = 31
N_EXPERTS = 256
TOP_K = 8
N_GROUPS = 8
TOPK_GROUPS = 4
GROUP_SIZE = N_EXPERTS // N_GROUPS
ROUTED_SCALE = 2.5
LN_EPS = 1e-5
DEPTH = 1
ALPHA = (2.0 * DEPTH) ** 0.25
LAM_INIT = 0.8 - 0.6 * math.exp(-0.3 * 0)

F32 = jnp.float32
BF16 = jnp.bfloat16
NEG_INF = float("-inf")

VMEM_LIMIT = 56 * 1024 * 1024


def _ln(x, g, b):
    mu = jnp.mean(x, axis=-1, keepdims=True)
    xc = x - mu
    var = jnp.mean(xc * xc, axis=-1, keepdims=True)
    return xc * lax.rsqrt(var + LN_EPS) * g + b


def _dot(a, b):
    return jnp.dot(a, b, preferred_element_type=F32)


def _dot_nt(a, b):
    return lax.dot_general(a, b, (((1,), (1,)), ((), ())), preferred_element_type=F32)


def _silu(x):
    return x * jax.nn.sigmoid(x)


SUBLANES = 8
LANES = 128


def _rows_from_tiles(ref, n_rows, first=0):
    return jnp.concatenate(
        [ref[pl.ds(first * SUBLANES + c, n_rows, stride=SUBLANES), :] for c in range(SUBLANES)],
        axis=1)


def _rows_to_tiles(ref, x):
    for c in range(SUBLANES):
        ref[pl.ds(c, x.shape[0], stride=SUBLANES), :] = x[:, c * LANES:(c + 1) * LANES]


LOG2E = math.log2(math.e)
HALF_HEAD = HEAD_DIM // 2


def _inproj_kernel(x_ref, g_ref, b_ref, wqt_ref, wk_ref, wvt_ref, wr_ref,
                   cos_ref, sin_ref, cost_ref, sint_ref,
                   qt_ref, k_ref, vt_ref, u_ref, ga_ref, gc_ref, *, d_model):
    h = _ln(x_ref[...], g_ref[...], b_ref[...]).astype(BF16)
    tm = x_ref.shape[0]
    c = d_model // 2

    qt = _dot_nt(wqt_ref[...], h)
    pieces = []
    for g in range(QK_WIDTH // HEAD_DIM):
        lo = g * HEAD_DIM
        pieces.append(qt[lo + HALF_HEAD:lo + HEAD_DIM, :])
        pieces.append(qt[lo:lo + HALF_HEAD, :])
    partner = jnp.concatenate(pieces, axis=0)
    reps_t = QK_WIDTH // cost_ref.shape[0]
    qt = qt * jnp.tile(cost_ref[...], (reps_t, 1)) + partner * jnp.tile(sint_ref[...], (reps_t, 1))
    qt_ref[...] = (qt * (HEAD_DIM ** -0.5 * LOG2E)).astype(BF16)

    kk = _dot(h, wk_ref[...])
    reps = QK_WIDTH // cos_ref.shape[1]
    lane = lax.broadcasted_iota(jnp.int32, (tm, QK_WIDTH), 1)
    partner = jnp.where((lane % HEAD_DIM) < HALF_HEAD,
                        pltpu.roll(kk, QK_WIDTH - HALF_HEAD, 1),
                        pltpu.roll(kk, HALF_HEAD, 1))
    kk = kk * jnp.tile(cos_ref[...], (1, reps)) + partner * jnp.tile(sin_ref[...], (1, reps))
    k_ref[...] = kk.astype(BF16)

    vt = _dot_nt(wvt_ref[...], h).astype(BF16)
    chunk = vt_ref.shape[2]
    for j in range(vt_ref.shape[0]):
        vt_ref[j] = vt[:, j * chunk:(j + 1) * chunk]

    def proj(lo, width):
        return _dot(h, wr_ref[:, lo:lo + width])

    u_ref[...] = proj(0, c) * jax.nn.sigmoid(proj(c, c))
    ga_ref[...] = jax.nn.sigmoid(proj(2 * c, d_model)).astype(BF16)
    gc_ref[...] = jax.nn.sigmoid(proj(2 * c + d_model, d_model)).astype(BF16)


def _inproj(x2, g, b, wqt, wk, wvt, wr, tables, tm, chunk):
    t, d = x2.shape
    c = d // 2
    cos, sin, cost, sint = tables
    n_pos_tiles = cos.shape[0] // tm
    row = lambda i: (i, 0)
    col = lambda i: (0, i)
    const = lambda i: (0, 0)
    pos = lambda i: (i % n_pos_tiles, 0)
    post = lambda i: (0, i % n_pos_tiles)
    full = lambda a: pl.BlockSpec(a.shape, const)
    return pl.pallas_call(
        functools.partial(_inproj_kernel, d_model=d),
        grid=(t // tm,),
        in_specs=[
            pl.BlockSpec((tm, d), row), full(g), full(b),
            full(wqt), full(wk), full(wvt), full(wr),
            pl.BlockSpec((tm, cos.shape[1]), pos),
            pl.BlockSpec((tm, cos.shape[1]), pos),
            pl.BlockSpec((cost.shape[0], tm), post),
            pl.BlockSpec((cost.shape[0], tm), post),
        ],
        out_specs=[
            pl.BlockSpec((QK_WIDTH, tm), col),
            pl.BlockSpec((tm, QK_WIDTH), row),
            pl.BlockSpec((tm // chunk, V_WIDTH, chunk), lambda i: (i, 0, 0)),
            pl.BlockSpec((tm, c), row),
            pl.BlockSpec((tm, d), row),
            pl.BlockSpec((tm, d), row),
        ],
        out_shape=[
            jax.ShapeDtypeStruct((QK_WIDTH, t), BF16),
            jax.ShapeDtypeStruct((t, QK_WIDTH), BF16),
            jax.ShapeDtypeStruct((t // chunk, V_WIDTH, chunk), BF16),
            jax.ShapeDtypeStruct((t, c), F32),
            jax.ShapeDtypeStruct((t, d), BF16),
            jax.ShapeDtypeStruct((t, d), BF16),
        ],
        compiler_params=pltpu.CompilerParams(
            dimension_semantics=("arbitrary",), vmem_limit_bytes=VMEM_LIMIT),
        name="inproj",
    )(x2, g, b, wqt, wk, wvt, wr, cos, sin, cost, sint)


def _attn_kernel(lam_ref, qt_ref, k_ref, vt_ref, km_ref, vmt_ref, g_ref, o_ref, acc_sc, *, tq, tk):
    i = pl.program_id(2)
    qt = qt_ref[...]
    feat = lax.broadcasted_iota(jnp.int32, qt.shape, 0)
    zero = jnp.zeros_like(qt)
    q2 = jnp.concatenate([jnp.where(feat < HEAD_DIM, qt, zero),
                          jnp.where(feat >= HEAD_DIM, qt, zero)], axis=1)

    def scores(j):
        r0 = pl.multiple_of(j * tk, tk)
        return _dot(k_ref[pl.ds(r0, tk), :], q2)

    def absorb(m_old, l_old, s, vtc):
        m_new = jnp.maximum(m_old, jnp.max(s, axis=0, keepdims=True))
        alpha = jnp.exp2(m_old - m_new)
        p = jnp.exp2(s - m_new)
        acc_sc[...] = alpha * acc_sc[...] + _dot(vtc, p.astype(BF16))
        return m_new, alpha * l_old + jnp.sum(p, axis=0, keepdims=True)

    s = _dot(km_ref[...], q2)
    m = jnp.max(s, axis=0, keepdims=True)
    p = jnp.exp2(s - m)
    acc_sc[...] = _dot(vmt_ref[...], p.astype(BF16))
    l = jnp.sum(p, axis=0, keepdims=True)

    per_tile = tq // tk
    n_full = i * per_tile

    def body(j, carry):
        m_old, l_old, s_cur = carry
        s_next = scores(j + 1)
        m_new, l_new = absorb(m_old, l_old, s_cur, vt_ref[j])
        return m_new, l_new, s_next

    m, l, s = lax.fori_loop(0, n_full, body, (m, l, scores(0)))

    key = lax.broadcasted_iota(jnp.int32, (tk, 2 * tq), 0)
    qry = lax.broadcasted_iota(jnp.int32, (tk, 2 * tq), 1) % tq
    for d in range(per_tile):
        s_cur = s
        if d + 1 < per_tile:
            s = scores(n_full + d + 1)
        m, l = absorb(m, l, jnp.where(key + d * tk <= qry, s_cur, NEG_INF), vt_ref[n_full + d])

    lam = lam_ref[0, 0]
    o = acc_sc[:, :tq] / l[:, :tq] - lam * (acc_sc[:, tq:] / l[:, tq:])
    o = o * lax.rsqrt(jnp.mean(o * o, axis=0, keepdims=True) + LN_EPS) * g_ref[...]
    o_ref[...] = (o * (1.0 - LAM_INIT)).T.astype(o_ref.dtype)


def _attention(lam, qt, k, vt, km, vmt, subln_col, bsz, seq, tq, tk):
    t = k.shape[0]
    nq = seq // tq
    nk = seq // tk
    return pl.pallas_call(
        functools.partial(_attn_kernel, tq=tq, tk=tk),
        grid=(bsz, HEADS, nq),
        in_specs=[
            pl.BlockSpec(memory_space=pltpu.SMEM),
            pl.BlockSpec((V_DIM, tq), lambda b, h, i: (h, b * nq + i)),
            pl.BlockSpec((seq, V_DIM), lambda b, h, i: (b, h)),
            pl.BlockSpec((nk, V_DIM, tk), lambda b, h, i: (b, h, 0)),
            pl.BlockSpec((N_META, V_DIM), lambda b, h, i: (0, h)),
            pl.BlockSpec((V_DIM, N_META), lambda b, h, i: (h, 0)),
            pl.BlockSpec((V_DIM, 1), lambda b, h, i: (0, 0)),
        ],
        out_specs=pl.BlockSpec((tq, V_DIM), lambda b, h, i: (b * nq + i, h)),
        out_shape=jax.ShapeDtypeStruct((t, V_WIDTH), BF16),
        scratch_shapes=[pltpu.VMEM((V_DIM, 2 * tq), F32)],
        compiler_params=pltpu.CompilerParams(
            dimension_semantics=("arbitrary", "arbitrary", "arbitrary"),
            vmem_limit_bytes=VMEM_LIMIT),
        name="diff_attention",
    )(lam, qt, k, vt, km, vmt, subln_col)


CONV_ROWS = 32
CONV_WINDOW = 64
CONV_PAD = 32


def _conv_kernel(u_ref, um_ref, w_ref, cb_ref, g_ref, b_ref, z_ref, ucat, grp, *, seq):
    c = u_ref.shape[1]
    ucat[0:CONV_PAD - N_META, :] = jnp.zeros((CONV_PAD - N_META, c), F32)
    ucat[CONV_PAD - N_META:CONV_PAD, :] = um_ref[...]
    ucat[CONV_PAD:CONV_PAD + seq, :] = u_ref[...]
    shift = CONV_PAD - (CONV_WIDTH - 1)

    def body(t, carry):
        r0 = pl.multiple_of(t * CONV_ROWS, CONV_ROWS)
        acc = jnp.zeros((CONV_ROWS, c), F32)
        for r in range(SUBLANES):
            taps = [j for j in range(CONV_WIDTH) if (shift + j) % SUBLANES == r]
            if not taps:
                continue
            lo = shift + taps[0]
            rows = taps[-1] - taps[0] + CONV_ROWS
            start = pl.multiple_of(r0 + lo - r, SUBLANES)
            span = rows + (SUBLANES if r else 0)
            grp[r, 0:rows, :] = ucat[pl.ds(start, span), :][r:r + rows, :]
            for j in taps:
                off = shift + j - lo
                acc = acc + grp[r, off:off + CONV_ROWS, :] * w_ref[j:j + 1, :]
        y = _ln(acc + cb_ref[...], g_ref[...], b_ref[...])
        z_ref[pl.ds(r0, CONV_ROWS), :] = _silu(y).astype(z_ref.dtype)
        return carry

    lax.fori_loop(0, seq // CONV_ROWS, body, 0)


def _conv(u, um, conv_w, conv_b, ln_g, ln_b, bsz, seq):
    t, c = u.shape
    const = lambda b: (0, 0)
    return pl.pallas_call(
        functools.partial(_conv_kernel, seq=seq),
        grid=(bsz,),
        in_specs=[
            pl.BlockSpec((seq, c), lambda b: (b, 0)),
            pl.BlockSpec((N_META, c), const),
            pl.BlockSpec((CONV_WIDTH, c), const),
            pl.BlockSpec((1, c), const),
            pl.BlockSpec((1, c), const),
            pl.BlockSpec((1, c), const),
        ],
        out_specs=pl.BlockSpec((seq, c), lambda b: (b, 0)),
        out_shape=jax.ShapeDtypeStruct((t, c), BF16),
        scratch_shapes=[pltpu.VMEM((CONV_PAD + seq, c), F32),
                        pltpu.VMEM((SUBLANES, CONV_WINDOW, c), F32)],
        compiler_params=pltpu.CompilerParams(
            dimension_semantics=("arbitrary",), vmem_limit_bytes=VMEM_LIMIT),
        name="conformer_conv",
    )(u, um, conv_w, conv_b, ln_g, ln_b)


def _merge_route_kernel(x_ref, lg_ref, lb_ref, oa_ref, z_ref, ga_ref, gc_ref,
                        wao_ref, wco_ref, bco_ref, wout_ref, g1_ref, b1_ref,
                        wrt_ref, rb_ref,
                        h1_ref, h1t_ref, te_ref, gw_ref, rk_ref, cnt_ref, carry_sc):
    tm = x_ref.shape[0]

    @pl.when(pl.program_id(0) == 0)
    def _():
        carry_sc[...] = jnp.zeros(carry_sc.shape, F32)

    h0 = _ln(x_ref[...], lg_ref[...], lb_ref[...])
    y_attn = _dot(oa_ref[...], wao_ref[...])
    y_conv = _dot(z_ref[...], wco_ref[...]) + bco_ref[...]
    merged = ga_ref[...].astype(F32) * y_attn + gc_ref[...].astype(F32) * y_conv
    h1 = _ln(ALPHA * h0 + _dot(merged.astype(BF16), wout_ref[...]), g1_ref[...], b1_ref[...])
    h1_ref[...] = h1
    _rows_to_tiles(h1t_ref, h1)
    h1b = h1.astype(BF16)

    scores = jax.nn.sigmoid(_dot_nt(wrt_ref[...], h1b))
    sel = scores + rb_ref[...]

    sub = lax.broadcasted_iota(jnp.int32, (GROUP_SIZE, tm), 0)
    gscore = []
    for g in range(N_GROUPS):
        slab = sel[g * GROUP_SIZE:(g + 1) * GROUP_SIZE, :]
        top1 = jnp.max(slab, axis=0, keepdims=True)
        arg1 = jnp.min(jnp.where(slab == top1, sub, GROUP_SIZE), axis=0, keepdims=True)
        top2 = jnp.max(jnp.where(sub == arg1, NEG_INF, slab), axis=0, keepdims=True)
        gscore.append(top1 + top2)

    eiota = lax.broadcasted_iota(jnp.int32, (N_EXPERTS, tm), 0)
    egroup = eiota // GROUP_SIZE
    allowed = jnp.zeros((N_EXPERTS, tm), F32)
    for _ in range(TOPK_GROUPS):
        best = functools.reduce(jnp.maximum, gscore)
        pick = jnp.full((1, tm), N_GROUPS, jnp.int32)
        for g in reversed(range(N_GROUPS)):
            pick = jnp.where(gscore[g] == best, g, pick)
        gscore = [jnp.where(pick == g, NEG_INF, gscore[g]) for g in range(N_GROUPS)]
        allowed = jnp.where(egroup == pick, 1.0, allowed)

    masked = jnp.where(allowed > 0.5, sel, NEG_INF)

    picks = []
    weights = []
    chosen = jnp.zeros((N_EXPERTS, tm), F32)
    for _ in range(TOP_K):
        best = jnp.max(masked, axis=0, keepdims=True)
        pick = jnp.min(jnp.where(masked == best, eiota, N_EXPERTS), axis=0, keepdims=True)
        hit = eiota == pick
        picks.append(pick)
        weights.append(jnp.sum(jnp.where(hit, scores, 0.0), axis=0, keepdims=True))
        chosen = jnp.where(hit, 1.0, chosen)
        masked = jnp.where(hit, NEG_INF, masked)

    gw = jnp.concatenate(weights, axis=0)
    gw_ref[...] = gw / jnp.sum(gw, axis=0, keepdims=True) * ROUTED_SCALE
    te_ref[...] = jnp.concatenate(picks, axis=0)

    before = (lax.broadcasted_iota(jnp.int32, (tm, tm), 0)
              < lax.broadcasted_iota(jnp.int32, (tm, tm), 1))
    prior = _dot(chosen.astype(BF16), jnp.where(before, 1.0, 0.0).astype(BF16)) + carry_sc[...]
    ranks = [jnp.sum(jnp.where(eiota == p, prior, 0.0), axis=0, keepdims=True) for p in picks]
    rk_ref[...] = jnp.concatenate(ranks, axis=0).astype(jnp.int32)
    carry_sc[...] = carry_sc[...] + jnp.sum(chosen, axis=1, keepdims=True)
    cnt_ref[...] = carry_sc[...]


def _merge_route(x2, lg, lb, oa, z, ga, gc, wao, wco, bco, wout, g1, b1, wrt, rb, tm):
    t, d = x2.shape
    c = d // 2
    row = lambda i: (i, 0)
    col = lambda i: (0, i)
    const = lambda i: (0, 0)
    full = lambda a: pl.BlockSpec(a.shape, const)
    return pl.pallas_call(
        _merge_route_kernel,
        grid=(t // tm,),
        in_specs=[
            pl.BlockSpec((tm, d), row), full(lg), full(lb),
            pl.BlockSpec((tm, V_WIDTH), row), pl.BlockSpec((tm, c), row),
            pl.BlockSpec((tm, d), row), pl.BlockSpec((tm, d), row),
            full(wao), full(wco), full(bco), full(wout), full(g1), full(b1),
            full(wrt), full(rb),
        ],
        out_specs=[
            pl.BlockSpec((tm, d), row),
            pl.BlockSpec((tm * SUBLANES, LANES), row),
            pl.BlockSpec((TOP_K, tm), col),
            pl.BlockSpec((TOP_K, tm), col),
            pl.BlockSpec((TOP_K, tm), col),
            pl.BlockSpec((N_EXPERTS, 1), const),
        ],
        out_shape=[
            jax.ShapeDtypeStruct((t, d), F32),
            jax.ShapeDtypeStruct((t * SUBLANES, LANES), F32),
            jax.ShapeDtypeStruct((TOP_K, t), jnp.int32),
            jax.ShapeDtypeStruct((TOP_K, t), F32),
            jax.ShapeDtypeStruct((TOP_K, t), jnp.int32),
            jax.ShapeDtypeStruct((N_EXPERTS, 1), F32),
        ],
        scratch_shapes=[pltpu.VMEM((N_EXPERTS, 1), F32)],
        compiler_params=pltpu.CompilerParams(
            dimension_semantics=("arbitrary",), vmem_limit_bytes=VMEM_LIMIT),
        name="merge_route",
    )(x2, lg, lb, oa, z, ga, gc, wao, wco, bco, wout, g1, b1, wrt, rb)


def _dest_kernel(te_ref, rk_ref, start_ref, d_ref):
    tm = te_ref.shape[1]
    eiota = lax.broadcasted_iota(jnp.int32, (N_EXPERTS, tm), 0)
    start = start_ref[...]
    rows = [jnp.sum(jnp.where(eiota == te_ref[k:k + 1, :], start, 0.0), axis=0, keepdims=True)
            for k in range(TOP_K)]
    d_ref[...] = (jnp.concatenate(rows, axis=0).astype(jnp.int32) + rk_ref[...]) * SUBLANES


def _dest(top_e, rank, start_col, tm):
    t = top_e.shape[1]
    col = lambda i: (0, i)
    return pl.pallas_call(
        _dest_kernel,
        grid=(t // tm,),
        in_specs=[
            pl.BlockSpec((TOP_K, tm), col),
            pl.BlockSpec((TOP_K, tm), col),
            pl.BlockSpec((N_EXPERTS, 1), lambda i: (0, 0)),
        ],
        out_specs=pl.BlockSpec((TOP_K, tm), col),
        out_shape=jax.ShapeDtypeStruct((TOP_K, t), jnp.int32),
        compiler_params=pltpu.CompilerParams(dimension_semantics=("arbitrary",)),
        name="moe_dest",
    )(top_e, rank, start_col)


def _dispatch_kernel(dest_ref, h_ref, xs_ref, zeros, sem, pad_sem, *, n_tok):
    tm = h_ref.shape[0] // SUBLANES
    base = pl.program_id(0) * tm

    @pl.when(pl.program_id(0) == 0)
    def _():
        zeros[...] = jnp.zeros(zeros.shape, zeros.dtype)
        pad = pltpu.make_async_copy(
            zeros, xs_ref.at[pl.ds(n_tok * TOP_K * SUBLANES, zeros.shape[0])], pad_sem)
        pad.start()
        pad.wait()

    for r in range(tm):
        for k in range(TOP_K):
            slot = pl.multiple_of(dest_ref[k * n_tok + base + r], SUBLANES)
            pltpu.make_async_copy(h_ref.at[pl.ds(r * SUBLANES, SUBLANES)],
                                  xs_ref.at[pl.ds(slot, SUBLANES)], sem).start(priority=k % 2)
    for k in range(TOP_K):
        pltpu.make_async_copy(h_ref, xs_ref.at[pl.ds(0, tm * SUBLANES)], sem).wait()


def _dispatch(dest_flat, h_tiles, tm):
    t = h_tiles.shape[0] // SUBLANES
    return pl.pallas_call(
        functools.partial(_dispatch_kernel, n_tok=t),
        grid_spec=pltpu.PrefetchScalarGridSpec(
            num_scalar_prefetch=1,
            grid=(t // tm,),
            in_specs=[pl.BlockSpec((tm * SUBLANES, LANES), lambda i, dest: (i, 0))],
            out_specs=pl.BlockSpec(memory_space=pl.ANY),
            scratch_shapes=[pltpu.VMEM((EXPERT_CHUNK * SUBLANES, LANES), h_tiles.dtype),
                            pltpu.SemaphoreType.DMA, pltpu.SemaphoreType.DMA],
        ),
        out_shape=jax.ShapeDtypeStruct(((t * TOP_K + EXPERT_CHUNK) * SUBLANES, LANES),
                                       h_tiles.dtype),
        compiler_params=pltpu.CompilerParams(
            dimension_semantics=("arbitrary",), has_side_effects=True),
        name="moe_dispatch",
    )(dest_flat, h_tiles)


EXPERT_CHUNK = 256
CHUNK_PIECES = tuple(EXPERT_CHUNK >> i for i in range(EXPERT_CHUNK.bit_length()))


def _experts_kernel(start_ref, count_ref, next_ref, first_ref,
                    xs_ref, wg_ref, wu_ref, wd_ref, ys_ref,
                    xbuf, ybuf, sem_x, sem_y, wg_bf, wu_bf, wd_bf, state):
    e = pl.program_id(0)
    ch = EXPERT_CHUNK
    count = count_ref[e]
    start = start_ref[e]

    @pl.when(e == 0)
    def _():
        state[0] = 0
        state[1] = 0
        state[2] = 0

    class fetch:
        def __init__(self, row, slot):
            self.copies = [
                pltpu.make_async_copy(xs_ref.at[pl.ds(row, ch), c],
                                      xbuf.at[slot, :, pl.ds(c * LANES, LANES)], sem_x.at[slot])
                for c in range(SUBLANES)]

        def start(self):
            for copy in self.copies:
                copy.start(priority=1)

        def wait(self):
            for copy in self.copies:
                copy.wait()

    def write_back(row, slot, rows, wait_only):
        for piece in CHUNK_PIECES:
            off = pl.multiple_of((rows // (2 * piece)) * (2 * piece), piece)

            @pl.when((rows & piece) != 0)
            def _():
                for c in range(SUBLANES):
                    copy = pltpu.make_async_copy(
                        ybuf.at[slot, pl.ds(off, piece), pl.ds(c * LANES, LANES)],
                        ys_ref.at[pl.ds(row + off, piece), c], sem_y.at[slot])
                    if wait_only:
                        copy.wait()
                    else:
                        copy.start()

    @pl.when(count > 0)
    def _():
        wg_bf[...] = wg_ref[...].astype(BF16)
        wu_bf[...] = wu_ref[...].astype(BF16)
        wd_bf[...] = wd_ref[...].astype(BF16)
        done = state[0]
        n_chunks = (count + ch - 1) // ch

        @pl.when(e == first_ref[0])
        def _():
            fetch(start, done % 2).start()

        def body(j, carry):
            slot = (done + j) % 2
            row = start + j * ch
            more = j + 1 < n_chunks
            following = next_ref[e]

            @pl.when(more)
            def _():
                fetch(row + ch, 1 - slot).start()

            @pl.when(jnp.logical_and(jnp.logical_not(more), following >= 0))
            def _():
                fetch(start_ref[jnp.maximum(following, 0)], 1 - slot).start()

            fetch(row, slot).wait()
            x = xbuf[slot].astype(BF16)
            a = _silu(_dot(x, wg_bf[...])) * _dot(x, wu_bf[...])
            y = _dot(a.astype(BF16), wd_bf[...])

            write_back(0, slot, state[1 + slot], wait_only=True)
            ybuf[slot] = y
            rows = jnp.minimum(count - j * ch, ch)
            write_back(row, slot, rows, wait_only=False)
            state[1 + slot] = rows
            return carry

        lax.fori_loop(0, n_chunks, body, 0)
        state[0] = done + n_chunks

    @pl.when(e == pl.num_programs(0) - 1)
    def _():
        for slot in range(2):
            write_back(0, slot, state[1 + slot], wait_only=True)
            state[1 + slot] = 0


def _experts(tables, xs, wg, wu, wd, n_rows):
    d = SUBLANES * LANES
    n_exp, _, hid = wg.shape
    wmap = lambda e, start, count, following, first: (e, 0, 0)
    return pl.pallas_call(
        _experts_kernel,
        grid_spec=pltpu.PrefetchScalarGridSpec(
            num_scalar_prefetch=len(tables),
            grid=(n_exp,),
            in_specs=[
                pl.BlockSpec(memory_space=pl.ANY),
                pl.BlockSpec((None, d, hid), wmap),
                pl.BlockSpec((None, d, hid), wmap),
                pl.BlockSpec((None, hid, d), wmap),
            ],
            out_specs=pl.BlockSpec(memory_space=pl.ANY),
            scratch_shapes=[
                pltpu.VMEM((2, EXPERT_CHUNK, d), F32),
                pltpu.VMEM((2, EXPERT_CHUNK, d), F32),
                pltpu.SemaphoreType.DMA((2,)),
                pltpu.SemaphoreType.DMA((2,)),
                pltpu.VMEM((d, hid), BF16), pltpu.VMEM((d, hid), BF16),
                pltpu.VMEM((hid, d), BF16),
                pltpu.SMEM((3,), jnp.int32),
            ],
        ),
        out_shape=jax.ShapeDtypeStruct((n_rows, SUBLANES, LANES), xs.dtype),
        compiler_params=pltpu.CompilerParams(
            dimension_semantics=("arbitrary",), vmem_limit_bytes=VMEM_LIMIT,
            has_side_effects=True),
        name="moe_experts",
    )(*tables, xs.reshape(-1, SUBLANES, LANES), wg, wu, wd)


def _combine_kernel(dest_ref, h_ref, gw_ref, ys_ref, wgs_ref, wus_ref, wds_ref, g2_ref, b2_ref,
                    o_ref, buf, sem, *, n_tok):
    tm = h_ref.shape[0] // 2
    step = pl.program_id(0)

    def gather(tile, into):
        base = tile * tm
        for r in range(tm):
            for k in range(TOP_K):
                slot = pl.multiple_of(dest_ref[k * n_tok + base + r], SUBLANES)
                pltpu.make_async_copy(ys_ref.at[pl.ds(slot, SUBLANES)],
                                      buf.at[into, k, pl.ds(r * SUBLANES, SUBLANES)],
                                      sem.at[into]).start(priority=k % 2)

    def finish(half):
        rows = slice(half * tm, (half + 1) * tm)
        h = h_ref[rows, :]
        hb = h.astype(BF16)
        shared = _dot((_silu(_dot(hb, wgs_ref[...])) * _dot(hb, wus_ref[...])).astype(BF16),
                      wds_ref[...])
        acc = ALPHA * h + shared
        for k in range(TOP_K):
            pltpu.make_async_copy(ys_ref.at[pl.ds(0, tm * SUBLANES)], buf.at[half, k],
                                  sem.at[half]).wait()
        gw = gw_ref[rows, :]
        for k in range(TOP_K):
            acc = acc + _rows_from_tiles(buf.at[half, k], tm) * gw[:, k:k + 1]
        o_ref[rows, :] = _ln(acc, g2_ref[...], b2_ref[...])

    @pl.when(step == 0)
    def _():
        gather(0, 0)

    gather(2 * step + 1, 1)
    finish(0)

    @pl.when(step + 1 < pl.num_programs(0))
    def _():
        gather(2 * step + 2, 0)

    finish(1)


def _combine(dest_flat, h1, gw, ys, wgs, wus, wds, g2, b2, tm):
    t, d = h1.shape
    row = lambda i, dest: (i, 0)
    const = lambda i, dest: (0, 0)
    full = lambda a: pl.BlockSpec(a.shape, const)
    return pl.pallas_call(
        functools.partial(_combine_kernel, n_tok=t),
        grid_spec=pltpu.PrefetchScalarGridSpec(
            num_scalar_prefetch=1,
            grid=(t // (2 * tm),),
            in_specs=[
                pl.BlockSpec((2 * tm, d), row),
                pl.BlockSpec((2 * tm, TOP_K), row),
                pl.BlockSpec(memory_space=pl.ANY),
                full(wgs), full(wus), full(wds), full(g2), full(b2),
            ],
            out_specs=pl.BlockSpec((2 * tm, d), row),
            scratch_shapes=[pltpu.VMEM((2, TOP_K, tm * SUBLANES, LANES), ys.dtype),
                            pltpu.SemaphoreType.DMA((2,))],
        ),
        out_shape=jax.ShapeDtypeStruct((t, d), F32),
        compiler_params=pltpu.CompilerParams(
            dimension_semantics=("arbitrary",), vmem_limit_bytes=VMEM_LIMIT),
        name="moe_combine",
    )(dest_flat, h1, gw, ys, wgs, wus, wds, g2, b2)


def _expert_tables(counts):
    n = counts.shape[0]
    start = (jnp.cumsum(counts) - counts).astype(jnp.int32)
    ids = jnp.arange(n, dtype=jnp.int32)
    later_busy = jnp.logical_and(ids[None, :] > ids[:, None], counts[None, :] > 0)
    following = jnp.min(jnp.where(later_busy, ids[None, :], n), axis=1)
    following = jnp.where(following >= n, -1, following).astype(jnp.int32)
    first = jnp.min(jnp.where(counts > 0, ids, n)).reshape(1).astype(jnp.int32)
    return start, counts.astype(jnp.int32), following, first


def _rope_tables(positions):
    inv = 1.0 / (ROPE_THETA ** (jnp.arange(0, HEAD_DIM, 2, dtype=F32) / HEAD_DIM))
    ang = positions.astype(F32)[:, None] * inv[None, :]
    cos = jnp.cos(ang)
    sin = jnp.sin(ang)
    cos = jnp.concatenate([cos, cos], axis=-1)
    sin = jnp.concatenate([-sin, sin], axis=-1)
    reps = V_DIM // HEAD_DIM
    return jnp.tile(cos, (1, reps)), jnp.tile(sin, (1, reps)), cos.T, sin.T


def _tile(n, pref):
    return pref if n % pref == 0 else n


def _tiles(seq, n_tok):
    return dict(q_rows=_tile(seq, 512), key_rows=_tile(seq, 256),
                merge_rows=_tile(n_tok, 256), dest_rows=_tile(n_tok, 512),
                move_rows=_tile(n_tok, 128))


def kernel(x, meta_tokens, ln_in_g, ln_in_b, w_in, lambda_q1, lambda_k1, lambda_q2, lambda_k2, subln_g, w_attn_o, conv_w, conv_b, conv_ln_g, conv_ln_b, w_conv_o, b_conv_o, w_out, ln1_g, ln1_b, w_router, router_bias, w_gate_e, w_up_e, w_down_e, w_gate_s, w_up_s, w_down_s, ln2_g, ln2_b):
    bsz, seq, d = x.shape
    t = bsz * seq
    l = 0
    row = lambda a: a.reshape(1, -1).astype(F32)
    x2 = x.reshape(t, d)
    lg, lb = row(ln_in_g), row(ln_in_b)
    w_in_bf = w_in[l].astype(BF16)
    o_v = 2 * QK_WIDTH
    wqt = w_in_bf[:, :QK_WIDTH].T
    wk = w_in_bf[:, QK_WIDTH:o_v]
    wvt = w_in_bf[:, o_v:o_v + V_WIDTH].T
    wr = w_in_bf[:, o_v + V_WIDTH:]

    tables_x = _rope_tables(jnp.arange(N_META, N_META + seq))
    tables_m = _rope_tables(jnp.arange(N_META))

    tiles = _tiles(seq, t)
    tq, tk = tiles["q_rows"], tiles["key_rows"]
    qt, k, vt, u, ga, gc = _inproj(x2, lg, lb, wqt, wk, wvt, wr, tables_x, tq, tk)
    _, km, vmt, um, _, _ = _inproj(meta_tokens.astype(F32), lg, lb, wqt, wk, wvt, wr, tables_m,
                                   N_META, N_META)

    lam = (jnp.exp(jnp.sum(lambda_q1[l].astype(F32) * lambda_k1[l].astype(F32)))
           - jnp.exp(jnp.sum(lambda_q2[l].astype(F32) * lambda_k2[l].astype(F32)))
           + LAM_INIT).reshape(1, 1)
    o_attn = _attention(lam, qt, k, vt, km, vmt.reshape(V_WIDTH, N_META),
                        subln_g[l].reshape(-1, 1).astype(F32), bsz, seq, tq, tk)

    z = _conv(u, um, conv_w[l].astype(F32), row(conv_b[l]), row(conv_ln_g[l]),
              row(conv_ln_b[l]), bsz, seq)

    assert d == SUBLANES * LANES, "the row-tile layout assumes one (8, 128) tile per token row"
    h1, h1_tiles, top_e, gw_t, rank, counts = _merge_route(
        x2, lg, lb, o_attn, z, ga, gc,
        w_attn_o[l].astype(BF16), w_conv_o[l].astype(BF16), row(b_conv_o[l]),
        w_out[l].astype(BF16), row(ln1_g[l]), row(ln1_b[l]),
        w_router[l].T.astype(BF16), router_bias[l].reshape(-1, 1).astype(F32),
        tiles["merge_rows"])

    tables = _expert_tables(counts.reshape(-1).astype(jnp.int32))
    start = tables[0]
    dest = _dest(top_e, rank, start.astype(F32).reshape(-1, 1), tiles["dest_rows"]).reshape(-1)

    xs = _dispatch(dest, h1_tiles, tiles["move_rows"])
    ys = _experts(tables, xs, w_gate_e[l], w_up_e[l], w_down_e[l], t * TOP_K)
    out = _combine(dest, h1, gw_t.T, ys.reshape(-1, LANES),
                   w_gate_s[l].astype(BF16), w_up_s[l].astype(BF16), w_down_s[l].astype(BF16),
                   row(ln2_g[l]), row(ln2_b[l]), tiles["move_rows"])
    return out.reshape(bsz, seq, d)
```

```python
import functools
import math

import jax
import jax.numpy as jnp
from jax import lax
from jax.experimental import pallas as pl
from jax.experimental.pallas import tpu as pltpu

N_META = 16
HEADS = 4
HEAD_DIM = 64
V_DIM = 2 * HEAD_DIM
QK_WIDTH = HEADS * 2 * HEAD_DIM
V_WIDTH = HEADS * V_DIM
ROPE_THETA = 10000.0
CONV_WIDTH = 31
N_EXPERTS = 256
TOP_K = 8
N_GROUPS = 8
TOPK_GROUPS = 4
GROUP_SIZE = N_EXPERTS // N_GROUPS
ROUTED_SCALE = 2.5
LN_EPS = 1e-5
DEPTH = 1
ALPHA = (2.0 * DEPTH) ** 0.25
LAM_INIT = 0.8 - 0.6 * math.exp(-0.3 * 0)

F32 = jnp.float32
BF16 = jnp.bfloat16
NEG_INF = float("-inf")

VMEM_LIMIT = 56 * 1024 * 1024


def _ln(x, g, b):
    mu = jnp.mean(x, axis=-1, keepdims=True)
    xc = x - mu
    var = jnp.mean(xc * xc, axis=-1, keepdims=True)
    return xc * lax.rsqrt(var + LN_EPS) * g + b


def _dot(a, b):
    return jnp.dot(a, b, preferred_element_type=F32)


def _dot_nt(a, b):
    return lax.dot_general(a, b, (((1,), (1,)), ((), ())), preferred_element_type=F32)


def _silu(x):
    return x * jax.nn.sigmoid(x)


SUBLANES = 8
LANES = 128


def _rows_from_tiles(ref, n_rows, first=0):
    return jnp.concatenate(
        [ref[pl.ds(first * SUBLANES + c, n_rows, stride=SUBLANES), :] for c in range(SUBLANES)],
        axis=1)


def _rows_to_tiles(ref, x):
    for c in range(SUBLANES):
        ref[pl.ds(c, x.shape[0], stride=SUBLANES), :] = x[:, c * LANES:(c + 1) * LANES]


LOG2E = math.log2(math.e)
HALF_HEAD = HEAD_DIM // 2


def _inproj_kernel(x_ref, g_ref, b_ref, wqt_ref, wk_ref, wvt_ref, wr_ref,
                   cos_ref, sin_ref, cost_ref, sint_ref,
                   qt_ref, k_ref, vt_ref, u_ref, ga_ref, gc_ref, *, d_model):
    h = _ln(x_ref[...], g_ref[...], b_ref[...]).astype(BF16)
    tm = x_ref.shape[0]
    c = d_model // 2

    qt = _dot_nt(wqt_ref[...], h)
    pieces = []
    for g in range(QK_WIDTH // HEAD_DIM):
        lo = g * HEAD_DIM
        pieces.append(qt[lo + HALF_HEAD:lo + HEAD_DIM, :])
        pieces.append(qt[lo:lo + HALF_HEAD, :])
    partner = jnp.concatenate(pieces, axis=0)
    reps_t = QK_WIDTH // cost_ref.shape[0]
    qt = qt * jnp.tile(cost_ref[...], (reps_t, 1)) + partner * jnp.tile(sint_ref[...], (reps_t, 1))
    qt_ref[...] = (qt * (HEAD_DIM ** -0.5 * LOG2E)).astype(BF16)

    kk = _dot(h, wk_ref[...])
    reps = QK_WIDTH // cos_ref.shape[1]
    lane = lax.broadcasted_iota(jnp.int32, (tm, QK_WIDTH), 1)
    partner = jnp.where((lane % HEAD_DIM) < HALF_HEAD,
                        pltpu.roll(kk, QK_WIDTH - HALF_HEAD, 1),
                        pltpu.roll(kk, HALF_HEAD, 1))
    kk = kk * jnp.tile(cos_ref[...], (1, reps)) + partner * jnp.tile(sin_ref[...], (1, reps))
    k_ref[...] = kk.astype(BF16)

    vt = _dot_nt(wvt_ref[...], h).astype(BF16)
    chunk = vt_ref.shape[2]
    for j in range(vt_ref.shape[0]):
        vt_ref[j] = vt[:, j * chunk:(j + 1) * chunk]

    def proj(lo, width):
        return _dot(h, wr_ref[:, lo:lo + width])

    u_ref[...] = proj(0, c) * jax.nn.sigmoid(proj(c, c))
    ga_ref[...] = jax.nn.sigmoid(proj(2 * c, d_model)).astype(BF16)
    gc_ref[...] = jax.nn.sigmoid(proj(2 * c + d_model, d_model)).astype(BF16)


def _inproj(x2, g, b, wqt, wk, wvt, wr, tables, tm, chunk):
    t, d = x2.shape
    c = d // 2
    cos, sin, cost, sint = tables
    n_pos_tiles = cos.shape[0] // tm
    row = lambda i: (i, 0)
    col = lambda i: (0, i)
    const = lambda i: (0, 0)
    pos = lambda i: (i % n_pos_tiles, 0)
    post = lambda i: (0, i % n_pos_tiles)
    full = lambda a: pl.BlockSpec(a.shape, const)
    return pl.pallas_call(
        functools.partial(_inproj_kernel, d_model=d),
        grid=(t // tm,),
        in_specs=[
            pl.BlockSpec((tm, d), row), full(g), full(b),
            full(wqt), full(wk), full(wvt), full(wr),
            pl.BlockSpec((tm, cos.shape[1]), pos),
            pl.BlockSpec((tm, cos.shape[1]), pos),
            pl.BlockSpec((cost.shape[0], tm), post),
            pl.BlockSpec((cost.shape[0], tm), post),
        ],
        out_specs=[
            pl.BlockSpec((QK_WIDTH, tm), col),
            pl.BlockSpec((tm, QK_WIDTH), row),
            pl.BlockSpec((tm // chunk, V_WIDTH, chunk), lambda i: (i, 0, 0)),
            pl.BlockSpec((tm, c), row),
            pl.BlockSpec((tm, d), row),
            pl.BlockSpec((tm, d), row),
        ],
        out_shape=[
            jax.ShapeDtypeStruct((QK_WIDTH, t), BF16),
            jax.ShapeDtypeStruct((t, QK_WIDTH), BF16),
            jax.ShapeDtypeStruct((t // chunk, V_WIDTH, chunk), BF16),
            jax.ShapeDtypeStruct((t, c), F32),
            jax.ShapeDtypeStruct((t, d), BF16),
            jax.ShapeDtypeStruct((t, d), BF16),
        ],
        compiler_params=pltpu.CompilerParams(
            dimension_semantics=("arbitrary",), vmem_limit_bytes=VMEM_LIMIT),
        name="inproj",
    )(x2, g, b, wqt, wk, wvt, wr, cos, sin, cost, sint)


def _attn_kernel(lam_ref, qt_ref, k_ref, vt_ref, km_ref, vmt_ref, g_ref, o_ref, acc_sc, *, tq, tk):
    i = pl.program_id(2)
    qt = qt_ref[...]
    feat = lax.broadcasted_iota(jnp.int32, qt.shape, 0)
    zero = jnp.zeros_like(qt)
    q2 = jnp.concatenate([jnp.where(feat < HEAD_DIM, qt, zero),
                          jnp.where(feat >= HEAD_DIM, qt, zero)], axis=1)

    def scores(j):
        r0 = pl.multiple_of(j * tk, tk)
        return _dot(k_ref[pl.ds(r0, tk), :], q2)

    def absorb(m_old, l_old, s, vtc):
        m_new = jnp.maximum(m_old, jnp.max(s, axis=0, keepdims=True))
        alpha = jnp.exp2(m_old - m_new)
        p = jnp.exp2(s - m_new)
        acc_sc[...] = alpha * acc_sc[...] + _dot(vtc, p.astype(BF16))
        return m_new, alpha * l_old + jnp.sum(p, axis=0, keepdims=True)

    s = _dot(km_ref[...], q2)
    m = jnp.max(s, axis=0, keepdims=True)
    p = jnp.exp2(s - m)
    acc_sc[...] = _dot(vmt_ref[...], p.astype(BF16))
    l = jnp.sum(p, axis=0, keepdims=True)

    per_tile = tq // tk
    n_full = i * per_tile

    def body(j, carry):
        m_old, l_old, s_cur = carry
        s_next = scores(j + 1)
        m_new, l_new = absorb(m_old, l_old, s_cur, vt_ref[j])
        return m_new, l_new, s_next

    m, l, s = lax.fori_loop(0, n_full, body, (m, l, scores(0)))

    key = lax.broadcasted_iota(jnp.int32, (tk, 2 * tq), 0)
    qry = lax.broadcasted_iota(jnp.int32, (tk, 2 * tq), 1) % tq
    for d in range(per_tile):
        s_cur = s
        if d + 1 < per_tile:
            s = scores(n_full + d + 1)
        m, l = absorb(m, l, jnp.where(key + d * tk <= qry, s_cur, NEG_INF), vt_ref[n_full + d])

    lam = lam_ref[0, 0]
    o = acc_sc[:, :tq] / l[:, :tq] - lam * (acc_sc[:, tq:] / l[:, tq:])
    o = o * lax.rsqrt(jnp.mean(o * o, axis=0, keepdims=True) + LN_EPS) * g_ref[...]
    o_ref[...] = (o * (1.0 - LAM_INIT)).T.astype(o_ref.dtype)


def _attention(lam, qt, k, vt, km, vmt, subln_col, bsz, seq, tq, tk):
    t = k.shape[0]
    nq = seq // tq
    nk = seq // tk
    return pl.pallas_call(
        functools.partial(_attn_kernel, tq=tq, tk=tk),
        grid=(bsz, HEADS, nq),
        in_specs=[
            pl.BlockSpec(memory_space=pltpu.SMEM),
            pl.BlockSpec((V_DIM, tq), lambda b, h, i: (h, b * nq + i)),
            pl.BlockSpec((seq, V_DIM), lambda b, h, i: (b, h)),
            pl.BlockSpec((nk, V_DIM, tk), lambda b, h, i: (b, h, 0)),
            pl.BlockSpec((N_META, V_DIM), lambda b, h, i: (0, h)),
            pl.BlockSpec((V_DIM, N_META), lambda b, h, i: (h, 0)),
            pl.BlockSpec((V_DIM, 1), lambda b, h, i: (0, 0)),
        ],
        out_specs=pl.BlockSpec((tq, V_DIM), lambda b, h, i: (b * nq + i, h)),
        out_shape=jax.ShapeDtypeStruct((t, V_WIDTH), BF16),
        scratch_shapes=[pltpu.VMEM((V_DIM, 2 * tq), F32)],
        compiler_params=pltpu.CompilerParams(
            dimension_semantics=("arbitrary", "arbitrary", "arbitrary"),
            vmem_limit_bytes=VMEM_LIMIT),
        name="diff_attention",
    )(lam, qt, k, vt, km, vmt, subln_col)


CONV_ROWS = 32
CONV_WINDOW = 64
CONV_PAD = 32


def _conv_kernel(u_ref, um_ref, w_ref, cb_ref, g_ref, b_ref, z_ref, ucat, grp, *, seq):
    c = u_ref.shape[1]
    ucat[0:CONV_PAD - N_META, :] = jnp.zeros((CONV_PAD - N_META, c), F32)
    ucat[CONV_PAD - N_META:CONV_PAD, :] = um_ref[...]
    ucat[CONV_PAD:CONV_PAD + seq, :] = u_ref[...]
    shift = CONV_PAD - (CONV_WIDTH - 1)

    def body(t, carry):
        r0 = pl.multiple_of(t * CONV_ROWS, CONV_ROWS)
        acc = jnp.zeros((CONV_ROWS, c), F32)
        for r in range(SUBLANES):
            taps = [j for j in range(CONV_WIDTH) if (shift + j) % SUBLANES == r]
            if not taps:
                continue
            lo = shift + taps[0]
            rows = taps[-1] - taps[0] + CONV_ROWS
            start = pl.multiple_of(r0 + lo - r, SUBLANES)
            span = rows + (SUBLANES if r else 0)
            grp[r, 0:rows, :] = ucat[pl.ds(start, span), :][r:r + rows, :]
            for j in taps:
                off = shift + j - lo
                acc = acc + grp[r, off:off + CONV_ROWS, :] * w_ref[j:j + 1, :]
        y = _ln(acc + cb_ref[...], g_ref[...], b_ref[...])
        z_ref[pl.ds(r0, CONV_ROWS), :] = _silu(y).astype(z_ref.dtype)
        return carry

    lax.fori_loop(0, seq // CONV_ROWS, body, 0)


def _conv(u, um, conv_w, conv_b, ln_g, ln_b, bsz, seq):
    t, c = u.shape
    const = lambda b: (0, 0)
    return pl.pallas_call(
        functools.partial(_conv_kernel, seq=seq),
        grid=(bsz,),
        in_specs=[
            pl.BlockSpec((seq, c), lambda b: (b, 0)),
            pl.BlockSpec((N_META, c), const),
            pl.BlockSpec((CONV_WIDTH, c), const),
            pl.BlockSpec((1, c), const),
            pl.BlockSpec((1, c), const),
            pl.BlockSpec((1, c), const),
        ],
        out_specs=pl.BlockSpec((seq, c), lambda b: (b, 0)),
        out_shape=jax.ShapeDtypeStruct((t, c), BF16),
        scratch_shapes=[pltpu.VMEM((CONV_PAD + seq, c), F32),
                        pltpu.VMEM((SUBLANES, CONV_WINDOW, c), F32)],
        compiler_params=pltpu.CompilerParams(
            dimension_semantics=("arbitrary",), vmem_limit_bytes=VMEM_LIMIT),
        name="conformer_conv",
    )(u, um, conv_w, conv_b, ln_g, ln_b)


def _merge_route_kernel(x_ref, lg_ref, lb_ref, oa_ref, z_ref, ga_ref, gc_ref,
                        wao_ref, wco_ref, bco_ref, wout_ref, g1_ref, b1_ref,
                        wrt_ref, rb_ref,
                        h1_ref, h1t_ref, te_ref, gw_ref, rk_ref, cnt_ref, carry_sc):
    tm = x_ref.shape[0]

    @pl.when(pl.program_id(0) == 0)
    def _():
        carry_sc[...] = jnp.zeros(carry_sc.shape, F32)

    h0 = _ln(x_ref[...], lg_ref[...], lb_ref[...])
    y_attn = _dot(oa_ref[...], wao_ref[...])
    y_conv = _dot(z_ref[...], wco_ref[...]) + bco_ref[...]
    merged = ga_ref[...].astype(F32) * y_attn + gc_ref[...].astype(F32) * y_conv
    h1 = _ln(ALPHA * h0 + _dot(merged.astype(BF16), wout_ref[...]), g1_ref[...], b1_ref[...])
    h1_ref[...] = h1
    _rows_to_tiles(h1t_ref, h1)
    h1b = h1.astype(BF16)

    scores = jax.nn.sigmoid(_dot_nt(wrt_ref[...], h1b))
    sel = scores + rb_ref[...]

    sub = lax.broadcasted_iota(jnp.int32, (GROUP_SIZE, tm), 0)
    gscore = []
    for g in range(N_GROUPS):
        slab = sel[g * GROUP_SIZE:(g + 1) * GROUP_SIZE, :]
        top1 = jnp.max(slab, axis=0, keepdims=True)
        arg1 = jnp.min(jnp.where(slab == top1, sub, GROUP_SIZE), axis=0, keepdims=True)
        top2 = jnp.max(jnp.where(sub == arg1, NEG_INF, slab), axis=0, keepdims=True)
        gscore.append(top1 + top2)

    eiota = lax.broadcasted_iota(jnp.int32, (N_EXPERTS, tm), 0)
    egroup = eiota // GROUP_SIZE
    allowed = jnp.zeros((N_EXPERTS, tm), F32)
    for _ in range(TOPK_GROUPS):
        best = functools.reduce(jnp.maximum, gscore)
        pick = jnp.full((1, tm), N_GROUPS, jnp.int32)
        for g in reversed(range(N_GROUPS)):
            pick = jnp.where(gscore[g] == best, g, pick)
        gscore = [jnp.where(pick == g, NEG_INF, gscore[g]) for g in range(N_GROUPS)]
        allowed = jnp.where(egroup == pick, 1.0, allowed)

    masked = jnp.where(allowed > 0.5, sel, NEG_INF)

    picks = []
    weights = []
    chosen = jnp.zeros((N_EXPERTS, tm), F32)
    for _ in range(TOP_K):
        best = jnp.max(masked, axis=0, keepdims=True)
        pick = jnp.min(jnp.where(masked == best, eiota, N_EXPERTS), axis=0, keepdims=True)
        hit = eiota == pick
        picks.append(pick)
        weights.append(jnp.sum(jnp.where(hit, scores, 0.0), axis=0, keepdims=True))
        chosen = jnp.where(hit, 1.0, chosen)
        masked = jnp.where(hit, NEG_INF, masked)

    gw = jnp.concatenate(weights, axis=0)
    gw_ref[...] = gw / jnp.sum(gw, axis=0, keepdims=True) * ROUTED_SCALE
    te_ref[...] = jnp.concatenate(picks, axis=0)

    before = (lax.broadcasted_iota(jnp.int32, (tm, tm), 0)
              < lax.broadcasted_iota(jnp.int32, (tm, tm), 1))
    prior = _dot(chosen.astype(BF16), jnp.where(before, 1.0, 0.0).astype(BF16)) + carry_sc[...]
    ranks = [jnp.sum(jnp.where(eiota == p, prior, 0.0), axis=0, keepdims=True) for p in picks]
    rk_ref[...] = jnp.concatenate(ranks, axis=0).astype(jnp.int32)
    carry_sc[...] = carry_sc[...] + jnp.sum(chosen, axis=1, keepdims=True)
    cnt_ref[...] = carry_sc[...]


def _merge_route(x2, lg, lb, oa, z, ga, gc, wao, wco, bco, wout, g1, b1, wrt, rb, tm):
    t, d = x2.shape
    c = d // 2
    row = lambda i: (i, 0)
    col = lambda i: (0, i)
    const = lambda i: (0, 0)
    full = lambda a: pl.BlockSpec(a.shape, const)
    return pl.pallas_call(
        _merge_route_kernel,
        grid=(t // tm,),
        in_specs=[
            pl.BlockSpec((tm, d), row), full(lg), full(lb),
            pl.BlockSpec((tm, V_WIDTH), row), pl.BlockSpec((tm, c), row),
            pl.BlockSpec((tm, d), row), pl.BlockSpec((tm, d), row),
            full(wao), full(wco), full(bco), full(wout), full(g1), full(b1),
            full(wrt), full(rb),
        ],
        out_specs=[
            pl.BlockSpec((tm, d), row),
            pl.BlockSpec((tm * SUBLANES, LANES), row),
            pl.BlockSpec((TOP_K, tm), col),
            pl.BlockSpec((TOP_K, tm), col),
            pl.BlockSpec((TOP_K, tm), col),
            pl.BlockSpec((N_EXPERTS, 1), const),
        ],
        out_shape=[
            jax.ShapeDtypeStruct((t, d), F32),
            jax.ShapeDtypeStruct((t * SUBLANES, LANES), F32),
            jax.ShapeDtypeStruct((TOP_K, t), jnp.int32),
            jax.ShapeDtypeStruct((TOP_K, t), F32),
            jax.ShapeDtypeStruct((TOP_K, t), jnp.int32),
            jax.ShapeDtypeStruct((N_EXPERTS, 1), F32),
        ],
        scratch_shapes=[pltpu.VMEM((N_EXPERTS, 1), F32)],
        compiler_params=pltpu.CompilerParams(
            dimension_semantics=("arbitrary",), vmem_limit_bytes=VMEM_LIMIT),
        name="merge_route",
    )(x2, lg, lb, oa, z, ga, gc, wao, wco, bco, wout, g1, b1, wrt, rb)


def _dest_kernel(te_ref, rk_ref, start_ref, d_ref):
    tm = te_ref.shape[1]
    eiota = lax.broadcasted_iota(jnp.int32, (N_EXPERTS, tm), 0)
    start = start_ref[...]
    rows = [jnp.sum(jnp.where(eiota == te_ref[k:k + 1, :], start, 0.0), axis=0, keepdims=True)
            for k in range(TOP_K)]
    d_ref[...] = (jnp.concatenate(rows, axis=0).astype(jnp.int32) + rk_ref[...]) * SUBLANES


def _dest(top_e, rank, start_col, tm):
    t = top_e.shape[1]
    col = lambda i: (0, i)
    return pl.pallas_call(
        _dest_kernel,
        grid=(t // tm,),
        in_specs=[
            pl.BlockSpec((TOP_K, tm), col),
            pl.BlockSpec((TOP_K, tm), col),
            pl.BlockSpec((N_EXPERTS, 1), lambda i: (0, 0)),
        ],
        out_specs=pl.BlockSpec((TOP_K, tm), col),
        out_shape=jax.ShapeDtypeStruct((TOP_K, t), jnp.int32),
        compiler_params=pltpu.CompilerParams(dimension_semantics=("arbitrary",)),
        name="moe_dest",
    )(top_e, rank, start_col)


def _dispatch_kernel(dest_ref, h_ref, xs_ref, sem, *, n_tok):
    tm = h_ref.shape[0] // SUBLANES
    base = pl.program_id(0) * tm
    for r in range(tm):
        for k in range(TOP_K):
            slot = pl.multiple_of(dest_ref[k * n_tok + base + r], SUBLANES)
            pltpu.make_async_copy(h_ref.at[pl.ds(r * SUBLANES, SUBLANES)],
                                  xs_ref.at[pl.ds(slot, SUBLANES)], sem).start(priority=k % 2)
    for k in range(TOP_K):
        pltpu.make_async_copy(h_ref, xs_ref.at[pl.ds(0, tm * SUBLANES)], sem).wait()


def _dispatch(dest_flat, h_tiles, tm):
    t = h_tiles.shape[0] // SUBLANES
    return pl.pallas_call(
        functools.partial(_dispatch_kernel, n_tok=t),
        grid_spec=pltpu.PrefetchScalarGridSpec(
            num_scalar_prefetch=1,
            grid=(t // tm,),
            in_specs=[pl.BlockSpec((tm * SUBLANES, LANES), lambda i, dest: (i, 0))],
            out_specs=pl.BlockSpec(memory_space=pl.ANY),
            scratch_shapes=[pltpu.SemaphoreType.DMA],
        ),
        out_shape=jax.ShapeDtypeStruct((t * TOP_K * SUBLANES, LANES), h_tiles.dtype),
        compiler_params=pltpu.CompilerParams(
            dimension_semantics=("arbitrary",), has_side_effects=True),
        name="moe_dispatch",
    )(dest_flat, h_tiles)


EXPERT_CHUNK = 256
CHUNK_PIECES = tuple(EXPERT_CHUNK >> i for i in range(EXPERT_CHUNK.bit_length()))


def _experts_kernel(start_ref, count_ref, next_ref, first_ref,
                    xs_ref, wg_ref, wu_ref, wd_ref, ys_ref,
                    xbuf, ybuf, sem_x, sem_y, wg_bf, wu_bf, wd_bf, state):
    e = pl.program_id(0)
    ch = EXPERT_CHUNK
    count = count_ref[e]
    start = start_ref[e]

    @pl.when(e == 0)
    def _():
        state[0] = 0
        state[1] = 0
        state[2] = 0
        xbuf[...] = jnp.zeros(xbuf.shape, xbuf.dtype)

    def move(row, slot, rows, fetch, wait_only):
        for piece in CHUNK_PIECES:
            off = pl.multiple_of((rows // (2 * piece)) * (2 * piece), piece)

            @pl.when((rows & piece) != 0)
            def _():
                for c in range(SUBLANES):
                    block = (pl.ds(off, piece), pl.ds(c * LANES, LANES))
                    if fetch:
                        copy = pltpu.make_async_copy(xs_ref.at[pl.ds(row + off, piece), c],
                                                     xbuf.at[(slot, *block)], sem_x.at[slot])
                    else:
                        copy = pltpu.make_async_copy(ybuf.at[(slot, *block)],
                                                     ys_ref.at[pl.ds(row + off, piece), c],
                                                     sem_y.at[slot])
                    if wait_only:
                        copy.wait()
                    else:
                        copy.start(priority=1 if fetch else 0)

    def chunk_rows(n_rows, j):
        return jnp.minimum(n_rows - j * ch, ch)

    @pl.when(count > 0)
    def _():
        wg_bf[...] = wg_ref[...].astype(BF16)
        wu_bf[...] = wu_ref[...].astype(BF16)
        wd_bf[...] = wd_ref[...].astype(BF16)
        done = state[0]
        n_chunks = (count + ch - 1) // ch

        @pl.when(e == first_ref[0])
        def _():
            move(start, done % 2, chunk_rows(count, 0), fetch=True, wait_only=False)

        def body(j, carry):
            slot = (done + j) % 2
            row = start + j * ch
            rows = chunk_rows(count, j)
            more = j + 1 < n_chunks
            following = jnp.maximum(next_ref[e], 0)

            @pl.when(more)
            def _():
                move(row + ch, 1 - slot, chunk_rows(count, j + 1), fetch=True, wait_only=False)

            @pl.when(jnp.logical_and(jnp.logical_not(more), next_ref[e] >= 0))
            def _():
                move(start_ref[following], 1 - slot, chunk_rows(count_ref[following], 0),
                     fetch=True, wait_only=False)

            move(0, slot, rows, fetch=True, wait_only=True)
            x = xbuf[slot].astype(BF16)
            a = _silu(_dot(x, wg_bf[...])) * _dot(x, wu_bf[...])
            y = _dot(a.astype(BF16), wd_bf[...])

            move(0, slot, state[1 + slot], fetch=False, wait_only=True)
            ybuf[slot] = y
            move(row, slot, rows, fetch=False, wait_only=False)
            state[1 + slot] = rows
            return carry

        lax.fori_loop(0, n_chunks, body, 0)
        state[0] = done + n_chunks

    @pl.when(e == pl.num_programs(0) - 1)
    def _():
        for slot in range(2):
            move(0, slot, state[1 + slot], fetch=False, wait_only=True)
            state[1 + slot] = 0


def _experts(tables, xs, wg, wu, wd, n_rows):
    d = SUBLANES * LANES
    n_exp, _, hid = wg.shape
    wmap = lambda e, start, count, following, first: (e, 0, 0)
    return pl.pallas_call(
        _experts_kernel,
        grid_spec=pltpu.PrefetchScalarGridSpec(
            num_scalar_prefetch=len(tables),
            grid=(n_exp,),
            in_specs=[
                pl.BlockSpec(memory_space=pl.ANY),
                pl.BlockSpec((None, d, hid), wmap),
                pl.BlockSpec((None, d, hid), wmap),
                pl.BlockSpec((None, hid, d), wmap),
            ],
            out_specs=pl.BlockSpec(memory_space=pl.ANY),
            scratch_shapes=[
                pltpu.VMEM((2, EXPERT_CHUNK, d), F32),
                pltpu.VMEM((2, EXPERT_CHUNK, d), F32),
                pltpu.SemaphoreType.DMA((2,)),
                pltpu.SemaphoreType.DMA((2,)),
                pltpu.VMEM((d, hid), BF16), pltpu.VMEM((d, hid), BF16),
                pltpu.VMEM((hid, d), BF16),
                pltpu.SMEM((3,), jnp.int32),
            ],
        ),
        out_shape=jax.ShapeDtypeStruct((n_rows, SUBLANES, LANES), xs.dtype),
        compiler_params=pltpu.CompilerParams(
            dimension_semantics=("arbitrary",), vmem_limit_bytes=VMEM_LIMIT,
            has_side_effects=True),
        name="moe_experts",
    )(*tables, xs.reshape(-1, SUBLANES, LANES), wg, wu, wd)


def _combine_kernel(dest_ref, h_ref, gw_ref, ys_ref, wgs_ref, wus_ref, wds_ref, g2_ref, b2_ref,
                    o_ref, buf, sem, *, n_tok):
    tm = h_ref.shape[0] // 2
    step = pl.program_id(0)

    def gather(tile, into):
        base = tile * tm
        for r in range(tm):
            for k in range(TOP_K):
                slot = pl.multiple_of(dest_ref[k * n_tok + base + r], SUBLANES)
                pltpu.make_async_copy(ys_ref.at[pl.ds(slot, SUBLANES)],
                                      buf.at[into, k, pl.ds(r * SUBLANES, SUBLANES)],
                                      sem.at[into]).start(priority=k % 2)

    def finish(half):
        rows = slice(half * tm, (half + 1) * tm)
        h = h_ref[rows, :]
        hb = h.astype(BF16)
        shared = _dot((_silu(_dot(hb, wgs_ref[...])) * _dot(hb, wus_ref[...])).astype(BF16),
                      wds_ref[...])
        acc = ALPHA * h + shared
        for k in range(TOP_K):
            pltpu.make_async_copy(ys_ref.at[pl.ds(0, tm * SUBLANES)], buf.at[half, k],
                                  sem.at[half]).wait()
        gw = gw_ref[rows, :]
        for k in range(TOP_K):
            acc = acc + _rows_from_tiles(buf.at[half, k], tm) * gw[:, k:k + 1]
        o_ref[rows, :] = _ln(acc, g2_ref[...], b2_ref[...])

    @pl.when(step == 0)
    def _():
        gather(0, 0)

    gather(2 * step + 1, 1)
    finish(0)

    @pl.when(step + 1 < pl.num_programs(0))
    def _():
        gather(2 * step + 2, 0)

    finish(1)


def _combine(dest_flat, h1, gw, ys, wgs, wus, wds, g2, b2, tm):
    t, d = h1.shape
    row = lambda i, dest: (i, 0)
    const = lambda i, dest: (0, 0)
    full = lambda a: pl.BlockSpec(a.shape, const)
    return pl.pallas_call(
        functools.partial(_combine_kernel, n_tok=t),
        grid_spec=pltpu.PrefetchScalarGridSpec(
            num_scalar_prefetch=1,
            grid=(t // (2 * tm),),
            in_specs=[
                pl.BlockSpec((2 * tm, d), row),
                pl.BlockSpec((2 * tm, TOP_K), row),
                pl.BlockSpec(memory_space=pl.ANY),
                full(wgs), full(wus), full(wds), full(g2), full(b2),
            ],
            out_specs=pl.BlockSpec((2 * tm, d), row),
            scratch_shapes=[pltpu.VMEM((2, TOP_K, tm * SUBLANES, LANES), ys.dtype),
                            pltpu.SemaphoreType.DMA((2,))],
        ),
        out_shape=jax.ShapeDtypeStruct((t, d), F32),
        compiler_params=pltpu.CompilerParams(
            dimension_semantics=("arbitrary",), vmem_limit_bytes=VMEM_LIMIT),
        name="moe_combine",
    )(dest_flat, h1, gw, ys, wgs, wus, wds, g2, b2)


def _expert_tables(counts):
    n = counts.shape[0]
    start = (jnp.cumsum(counts) - counts).astype(jnp.int32)
    ids = jnp.arange(n, dtype=jnp.int32)
    later_busy = jnp.logical_and(ids[None, :] > ids[:, None], counts[None, :] > 0)
    following = jnp.min(jnp.where(later_busy, ids[None, :], n), axis=1)
    following = jnp.where(following >= n, -1, following).astype(jnp.int32)
    first = jnp.min(jnp.where(counts > 0, ids, n)).reshape(1).astype(jnp.int32)
    return start, counts.astype(jnp.int32), following, first


def _rope_tables(positions):
    inv = 1.0 / (ROPE_THETA ** (jnp.arange(0, HEAD_DIM, 2, dtype=F32) / HEAD_DIM))
    ang = positions.astype(F32)[:, None] * inv[None, :]
    cos = jnp.cos(ang)
    sin = jnp.sin(ang)
    cos = jnp.concatenate([cos, cos], axis=-1)
    sin = jnp.concatenate([-sin, sin], axis=-1)
    reps = V_DIM // HEAD_DIM
    return jnp.tile(cos, (1, reps)), jnp.tile(sin, (1, reps)), cos.T, sin.T


def _tile(n, pref):
    return pref if n % pref == 0 else n


def _tiles(seq, n_tok):
    return dict(q_rows=_tile(seq, 512), key_rows=_tile(seq, 256),
                merge_rows=_tile(n_tok, 256), dest_rows=_tile(n_tok, 512),
                move_rows=_tile(n_tok, 128))


def kernel(x, meta_tokens, ln_in_g, ln_in_b, w_in, lambda_q1, lambda_k1, lambda_q2, lambda_k2, subln_g, w_attn_o, conv_w, conv_b, conv_ln_g, conv_ln_b, w_conv_o, b_conv_o, w_out, ln1_g, ln1_b, w_router, router_bias, w_gate_e, w_up_e, w_down_e, w_gate_s, w_up_s, w_down_s, ln2_g, ln2_b):
    bsz, seq, d = x.shape
    t = bsz * seq
    l = 0
    row = lambda a: a.reshape(1, -1).astype(F32)
    x2 = x.reshape(t, d)
    lg, lb = row(ln_in_g), row(ln_in_b)
    w_in_bf = w_in[l].astype(BF16)
    o_v = 2 * QK_WIDTH
    wqt = w_in_bf[:, :QK_WIDTH].T
    wk = w_in_bf[:, QK_WIDTH:o_v]
    wvt = w_in_bf[:, o_v:o_v + V_WIDTH].T
    wr = w_in_bf[:, o_v + V_WIDTH:]

    tables_x = _rope_tables(jnp.arange(N_META, N_META + seq))
    tables_m = _rope_tables(jnp.arange(N_META))

    tiles = _tiles(seq, t)
    tq, tk = tiles["q_rows"], tiles["key_rows"]
    qt, k, vt, u, ga, gc = _inproj(x2, lg, lb, wqt, wk, wvt, wr, tables_x, tq, tk)
    _, km, vmt, um, _, _ = _inproj(meta_tokens.astype(F32), lg, lb, wqt, wk, wvt, wr, tables_m,
                                   N_META, N_META)

    lam = (jnp.exp(jnp.sum(lambda_q1[l].astype(F32) * lambda_k1[l].astype(F32)))
           - jnp.exp(jnp.sum(lambda_q2[l].astype(F32) * lambda_k2[l].astype(F32)))
           + LAM_INIT).reshape(1, 1)
    o_attn = _attention(lam, qt, k, vt, km, vmt.reshape(V_WIDTH, N_META),
                        subln_g[l].reshape(-1, 1).astype(F32), bsz, seq, tq, tk)

    z = _conv(u, um, conv_w[l].astype(F32), row(conv_b[l]), row(conv_ln_g[l]),
              row(conv_ln_b[l]), bsz, seq)

    assert d == SUBLANES * LANES, "the row-tile layout assumes one (8, 128) tile per token row"
    h1, h1_tiles, top_e, gw_t, rank, counts = _merge_route(
        x2, lg, lb, o_attn, z, ga, gc,
        w_attn_o[l].astype(BF16), w_conv_o[l].astype(BF16), row(b_conv_o[l]),
        w_out[l].astype(BF16), row(ln1_g[l]), row(ln1_b[l]),
        w_router[l].T.astype(BF16), router_bias[l].reshape(-1, 1).astype(F32),
        tiles["merge_rows"])

    tables = _expert_tables(counts.reshape(-1).astype(jnp.int32))
    start = tables[0]
    dest = _dest(top_e, rank, start.astype(F32).reshape(-1, 1), tiles["dest_rows"]).reshape(-1)

    xs = _dispatch(dest, h1_tiles, tiles["move_rows"])
    ys = _experts(tables, xs, w_gate_e[l], w_up_e[l], w_down_e[l], t * TOP_K)
    out = _combine(dest, h1, gw_t.T, ys.reshape(-1, LANES),
                   w_gate_s[l].astype(BF16), w_up_s[l].astype(BF16), w_down_s[l].astype(BF16),
                   row(ln2_g[l]), row(ln2_b[l]), tiles["move_rows"])
    return out.reshape(bsz, seq, d)
```

```python
import functools
import math

import jax
import jax.numpy as jnp
from jax import lax
from jax.experimental import pallas as pl
from jax.experimental.pallas import tpu as pltpu

N_META = 16
HEADS = 4
HEAD_DIM = 64
V_DIM = 2 * HEAD_DIM
QK_WIDTH = HEADS * 2 * HEAD_DIM
V_WIDTH = HEADS * V_DIM
ROPE_THETA = 10000.0
CONV_WIDTH = 31
N_EXPERTS = 256
TOP_K = 8
N_GROUPS = 8
TOPK_GROUPS = 4
GROUP_SIZE = N_EXPERTS // N_GROUPS
ROUTED_SCALE = 2.5
LN_EPS = 1e-5
DEPTH = 1
ALPHA = (2.0 * DEPTH) ** 0.25
LAM_INIT = 0.8 - 0.6 * math.exp(-0.3 * 0)

F32 = jnp.float32
BF16 = jnp.bfloat16
NEG_INF = float("-inf")

VMEM_LIMIT = 56 * 1024 * 1024


def _ln(x, g, b):
    mu = jnp.mean(x, axis=-1, keepdims=True)
    xc = x - mu
    var = jnp.mean(xc * xc, axis=-1, keepdims=True)
    return xc * lax.rsqrt(var + LN_EPS) * g + b


def _dot(a, b):
    return jnp.dot(a, b, preferred_element_type=F32)


def _dot_nt(a, b):
    return lax.dot_general(a, b, (((1,), (1,)), ((), ())), preferred_element_type=F32)


def _silu(x):
    return x * jax.nn.sigmoid(x)


SUBLANES = 8
LANES = 128


def _rows_from_tiles(ref, n_rows, first=0):
    return jnp.concatenate(
        [ref[pl.ds(first * SUBLANES + c, n_rows, stride=SUBLANES), :] for c in range(SUBLANES)],
        axis=1)


def _rows_to_tiles(ref, x):
    for c in range(SUBLANES):
        ref[pl.ds(c, x.shape[0], stride=SUBLANES), :] = x[:, c * LANES:(c + 1) * LANES]


LOG2E = math.log2(math.e)
HALF_HEAD = HEAD_DIM // 2


def _inproj_kernel(x_ref, g_ref, b_ref, wqt_ref, wk_ref, wvt_ref, wr_ref,
                   cos_ref, sin_ref, cost_ref, sint_ref,
                   qt_ref, k_ref, vt_ref, u_ref, ga_ref, gc_ref, *, d_model):
    h = _ln(x_ref[...], g_ref[...], b_ref[...]).astype(BF16)
    tm = x_ref.shape[0]
    c = d_model // 2

    qt = _dot_nt(wqt_ref[...], h)
    pieces = []
    for g in range(QK_WIDTH // HEAD_DIM):
        lo = g * HEAD_DIM
        pieces.append(qt[lo + HALF_HEAD:lo + HEAD_DIM, :])
        pieces.append(qt[lo:lo + HALF_HEAD, :])
    partner = jnp.concatenate(pieces, axis=0)
    reps_t = QK_WIDTH // cost_ref.shape[0]
    qt = qt * jnp.tile(cost_ref[...], (reps_t, 1)) + partner * jnp.tile(sint_ref[...], (reps_t, 1))
    qt_ref[...] = (qt * (HEAD_DIM ** -0.5 * LOG2E)).astype(BF16)

    kk = _dot(h, wk_ref[...])
    reps = QK_WIDTH // cos_ref.shape[1]
    lane = lax.broadcasted_iota(jnp.int32, (tm, QK_WIDTH), 1)
    partner = jnp.where((lane % HEAD_DIM) < HALF_HEAD,
                        pltpu.roll(kk, QK_WIDTH - HALF_HEAD, 1),
                        pltpu.roll(kk, HALF_HEAD, 1))
    kk = kk * jnp.tile(cos_ref[...], (1, reps)) + partner * jnp.tile(sin_ref[...], (1, reps))
    k_ref[...] = kk.astype(BF16)

    vt = _dot_nt(wvt_ref[...], h).astype(BF16)
    chunk = vt_ref.shape[2]
    for j in range(vt_ref.shape[0]):
        vt_ref[j] = vt[:, j * chunk:(j + 1) * chunk]

    def proj(lo, width):
        return _dot(h, wr_ref[:, lo:lo + width])

    u_ref[...] = proj(0, c) * jax.nn.sigmoid(proj(c, c))
    ga_ref[...] = jax.nn.sigmoid(proj(2 * c, d_model)).astype(BF16)
    gc_ref[...] = jax.nn.sigmoid(proj(2 * c + d_model, d_model)).astype(BF16)


def _inproj(x2, g, b, wqt, wk, wvt, wr, tables, tm, chunk):
    t, d = x2.shape
    c = d // 2
    cos, sin, cost, sint = tables
    n_pos_tiles = cos.shape[0] // tm
    row = lambda i: (i, 0)
    col = lambda i: (0, i)
    const = lambda i: (0, 0)
    pos = lambda i: (i % n_pos_tiles, 0)
    post = lambda i: (0, i % n_pos_tiles)
    full = lambda a: pl.BlockSpec(a.shape, const)
    return pl.pallas_call(
        functools.partial(_inproj_kernel, d_model=d),
        grid=(t // tm,),
        in_specs=[
            pl.BlockSpec((tm, d), row), full(g), full(b),
            full(wqt), full(wk), full(wvt), full(wr),
            pl.BlockSpec((tm, cos.shape[1]), pos),
            pl.BlockSpec((tm, cos.shape[1]), pos),
            pl.BlockSpec((cost.shape[0], tm), post),
            pl.BlockSpec((cost.shape[0], tm), post),
        ],
        out_specs=[
            pl.BlockSpec((QK_WIDTH, tm), col),
            pl.BlockSpec((tm, QK_WIDTH), row),
            pl.BlockSpec((tm // chunk, V_WIDTH, chunk), lambda i: (i, 0, 0)),
            pl.BlockSpec((tm, c), row),
            pl.BlockSpec((tm, d), row),
            pl.BlockSpec((tm, d), row),
        ],
        out_shape=[
            jax.ShapeDtypeStruct((QK_WIDTH, t), BF16),
            jax.ShapeDtypeStruct((t, QK_WIDTH), BF16),
            jax.ShapeDtypeStruct((t // chunk, V_WIDTH, chunk), BF16),
            jax.ShapeDtypeStruct((t, c), F32),
            jax.ShapeDtypeStruct((t, d), BF16),
            jax.ShapeDtypeStruct((t, d), BF16),
        ],
        compiler_params=pltpu.CompilerParams(
            dimension_semantics=("arbitrary",), vmem_limit_bytes=VMEM_LIMIT),
        name="inproj",
    )(x2, g, b, wqt, wk, wvt, wr, cos, sin, cost, sint)


def _attn_kernel(lam_ref, qt_ref, k_ref, vt_ref, km_ref, vmt_ref, g_ref, o_ref, acc_sc, *, tq, tk):
    i = pl.program_id(2)
    qt = qt_ref[...]
    feat = lax.broadcasted_iota(jnp.int32, qt.shape, 0)
    zero = jnp.zeros_like(qt)
    q2 = jnp.concatenate([jnp.where(feat < HEAD_DIM, qt, zero),
                          jnp.where(feat >= HEAD_DIM, qt, zero)], axis=1)

    def scores(j):
        r0 = pl.multiple_of(j * tk, tk)
        return _dot(k_ref[pl.ds(r0, tk), :], q2)

    def absorb(m_old, l_old, s, vtc):
        m_new = jnp.maximum(m_old, jnp.max(s, axis=0, keepdims=True))
        alpha = jnp.exp2(m_old - m_new)
        p = jnp.exp2(s - m_new)
        acc_sc[...] = alpha * acc_sc[...] + _dot(vtc, p.astype(BF16))
        return m_new, alpha * l_old + jnp.sum(p, axis=0, keepdims=True)

    s = _dot(km_ref[...], q2)
    m = jnp.max(s, axis=0, keepdims=True)
    p = jnp.exp2(s - m)
    acc_sc[...] = _dot(vmt_ref[...], p.astype(BF16))
    l = jnp.sum(p, axis=0, keepdims=True)

    per_tile = tq // tk
    n_full = i * per_tile

    def body(j, carry):
        m_old, l_old, s_cur = carry
        s_next = scores(j + 1)
        m_new, l_new = absorb(m_old, l_old, s_cur, vt_ref[j])
        return m_new, l_new, s_next

    m, l, s = lax.fori_loop(0, n_full, body, (m, l, scores(0)))

    key = lax.broadcasted_iota(jnp.int32, (tk, 2 * tq), 0)
    qry = lax.broadcasted_iota(jnp.int32, (tk, 2 * tq), 1) % tq
    for d in range(per_tile):
        s_cur = s
        if d + 1 < per_tile:
            s = scores(n_full + d + 1)
        m, l = absorb(m, l, jnp.where(key + d * tk <= qry, s_cur, NEG_INF), vt_ref[n_full + d])

    lam = lam_ref[0, 0]
    o = acc_sc[:, :tq] / l[:, :tq] - lam * (acc_sc[:, tq:] / l[:, tq:])
    o = o * lax.rsqrt(jnp.mean(o * o, axis=0, keepdims=True) + LN_EPS) * g_ref[...]
    o_ref[...] = (o * (1.0 - LAM_INIT)).T.astype(o_ref.dtype)


def _attention(lam, qt, k, vt, km, vmt, subln_col, bsz, seq, tq, tk):
    t = k.shape[0]
    nq = seq // tq
    nk = seq // tk
    return pl.pallas_call(
        functools.partial(_attn_kernel, tq=tq, tk=tk),
        grid=(bsz, HEADS, nq),
        in_specs=[
            pl.BlockSpec(memory_space=pltpu.SMEM),
            pl.BlockSpec((V_DIM, tq), lambda b, h, i: (h, b * nq + i)),
            pl.BlockSpec((seq, V_DIM), lambda b, h, i: (b, h)),
            pl.BlockSpec((nk, V_DIM, tk), lambda b, h, i: (b, h, 0)),
            pl.BlockSpec((N_META, V_DIM), lambda b, h, i: (0, h)),
            pl.BlockSpec((V_DIM, N_META), lambda b, h, i: (h, 0)),
            pl.BlockSpec((V_DIM, 1), lambda b, h, i: (0, 0)),
        ],
        out_specs=pl.BlockSpec((tq, V_DIM), lambda b, h, i: (b * nq + i, h)),
        out_shape=jax.ShapeDtypeStruct((t, V_WIDTH), BF16),
        scratch_shapes=[pltpu.VMEM((V_DIM, 2 * tq), F32)],
        compiler_params=pltpu.CompilerParams(
            dimension_semantics=("arbitrary", "arbitrary", "arbitrary"),
            vmem_limit_bytes=VMEM_LIMIT),
        name="diff_attention",
    )(lam, qt, k, vt, km, vmt, subln_col)


CONV_ROWS = 32
CONV_WINDOW = 64
CONV_PAD = 32


def _conv_kernel(u_ref, um_ref, w_ref, cb_ref, g_ref, b_ref, z_ref, ucat, grp, *, seq):
    c = u_ref.shape[1]
    ucat[0:CONV_PAD - N_META, :] = jnp.zeros((CONV_PAD - N_META, c), F32)
    ucat[CONV_PAD - N_META:CONV_PAD, :] = um_ref[...]
    ucat[CONV_PAD:CONV_PAD + seq, :] = u_ref[...]
    shift = CONV_PAD - (CONV_WIDTH - 1)

    def body(t, carry):
        r0 = pl.multiple_of(t * CONV_ROWS, CONV_ROWS)
        acc = jnp.zeros((CONV_ROWS, c), F32)
        for r in range(SUBLANES):
            taps = [j for j in range(CONV_WIDTH) if (shift + j) % SUBLANES == r]
            if not taps:
                continue
            lo = shift + taps[0]
            rows = taps[-1] - taps[0] + CONV_ROWS
            start = pl.multiple_of(r0 + lo - r, SUBLANES)
            span = rows + (SUBLANES if r else 0)
            grp[r, 0:rows, :] = ucat[pl.ds(start, span), :][r:r + rows, :]
            for j in taps:
                off = shift + j - lo
                acc = acc + grp[r, off:off + CONV_ROWS, :] * w_ref[j:j + 1, :]
        y = _ln(acc + cb_ref[...], g_ref[...], b_ref[...])
        z_ref[pl.ds(r0, CONV_ROWS), :] = _silu(y).astype(z_ref.dtype)
        return carry

    lax.fori_loop(0, seq // CONV_ROWS, body, 0)


def _conv(u, um, conv_w, conv_b, ln_g, ln_b, bsz, seq):
    t, c = u.shape
    const = lambda b: (0, 0)
    return pl.pallas_call(
        functools.partial(_conv_kernel, seq=seq),
        grid=(bsz,),
        in_specs=[
            pl.BlockSpec((seq, c), lambda b: (b, 0)),
            pl.BlockSpec((N_META, c), const),
            pl.BlockSpec((CONV_WIDTH, c), const),
            pl.BlockSpec((1, c), const),
            pl.BlockSpec((1, c), const),
            pl.BlockSpec((1, c), const),
        ],
        out_specs=pl.BlockSpec((seq, c), lambda b: (b, 0)),
        out_shape=jax.ShapeDtypeStruct((t, c), BF16),
        scratch_shapes=[pltpu.VMEM((CONV_PAD + seq, c), F32),
                        pltpu.VMEM((SUBLANES, CONV_WINDOW, c), F32)],
        compiler_params=pltpu.CompilerParams(
            dimension_semantics=("arbitrary",), vmem_limit_bytes=VMEM_LIMIT),
        name="conformer_conv",
    )(u, um, conv_w, conv_b, ln_g, ln_b)


def _merge_route_kernel(x_ref, lg_ref, lb_ref, oa_ref, z_ref, ga_ref, gc_ref,
                        wao_ref, wco_ref, bco_ref, wout_ref, g1_ref, b1_ref,
                        wrt_ref, rb_ref,
                        h1_ref, h1t_ref, te_ref, gw_ref, rk_ref, cnt_ref, carry_sc):
    tm = x_ref.shape[0]

    @pl.when(pl.program_id(0) == 0)
    def _():
        carry_sc[...] = jnp.zeros(carry_sc.shape, F32)

    h0 = _ln(x_ref[...], lg_ref[...], lb_ref[...])
    y_attn = _dot(oa_ref[...], wao_ref[...])
    y_conv = _dot(z_ref[...], wco_ref[...]) + bco_ref[...]
    merged = ga_ref[...].astype(F32) * y_attn + gc_ref[...].astype(F32) * y_conv
    h1 = _ln(ALPHA * h0 + _dot(merged.astype(BF16), wout_ref[...]), g1_ref[...], b1_ref[...])
    h1_ref[...] = h1
    _rows_to_tiles(h1t_ref, h1)
    h1b = h1.astype(BF16)

    scores = jax.nn.sigmoid(_dot_nt(wrt_ref[...], h1b))
    sel = scores + rb_ref[...]

    sub = lax.broadcasted_iota(jnp.int32, (GROUP_SIZE, tm), 0)
    gscore = []
    for g in range(N_GROUPS):
        slab = sel[g * GROUP_SIZE:(g + 1) * GROUP_SIZE, :]
        top1 = jnp.max(slab, axis=0, keepdims=True)
        arg1 = jnp.min(jnp.where(slab == top1, sub, GROUP_SIZE), axis=0, keepdims=True)
        top2 = jnp.max(jnp.where(sub == arg1, NEG_INF, slab), axis=0, keepdims=True)
        gscore.append(top1 + top2)

    eiota = lax.broadcasted_iota(jnp.int32, (N_EXPERTS, tm), 0)
    egroup = eiota // GROUP_SIZE
    allowed = jnp.zeros((N_EXPERTS, tm), F32)
    for _ in range(TOPK_GROUPS):
        best = functools.reduce(jnp.maximum, gscore)
        pick = jnp.full((1, tm), N_GROUPS, jnp.int32)
        for g in reversed(range(N_GROUPS)):
            pick = jnp.where(gscore[g] == best, g, pick)
        gscore = [jnp.where(pick == g, NEG_INF, gscore[g]) for g in range(N_GROUPS)]
        allowed = jnp.where(egroup == pick, 1.0, allowed)

    masked = jnp.where(allowed > 0.5, sel, NEG_INF)

    picks = []
    weights = []
    chosen = jnp.zeros((N_EXPERTS, tm), F32)
    for _ in range(TOP_K):
        best = jnp.max(masked, axis=0, keepdims=True)
        pick = jnp.min(jnp.where(masked == best, eiota, N_EXPERTS), axis=0, keepdims=True)
        hit = eiota == pick
        picks.append(pick)
        weights.append(jnp.sum(jnp.where(hit, scores, 0.0), axis=0, keepdims=True))
        chosen = jnp.where(hit, 1.0, chosen)
        masked = jnp.where(hit, NEG_INF, masked)

    gw = jnp.concatenate(weights, axis=0)
    gw_ref[...] = gw / jnp.sum(gw, axis=0, keepdims=True) * ROUTED_SCALE
    te_ref[...] = jnp.concatenate(picks, axis=0)

    before = (lax.broadcasted_iota(jnp.int32, (tm, tm), 0)
              < lax.broadcasted_iota(jnp.int32, (tm, tm), 1))
    prior = _dot(chosen.astype(BF16), jnp.where(before, 1.0, 0.0).astype(BF16)) + carry_sc[...]
    ranks = [jnp.sum(jnp.where(eiota == p, prior, 0.0), axis=0, keepdims=True) for p in picks]
    rk_ref[...] = jnp.concatenate(ranks, axis=0).astype(jnp.int32)
    carry_sc[...] = carry_sc[...] + jnp.sum(chosen, axis=1, keepdims=True)
    cnt_ref[...] = carry_sc[...]


def _merge_route(x2, lg, lb, oa, z, ga, gc, wao, wco, bco, wout, g1, b1, wrt, rb, tm):
    t, d = x2.shape
    c = d // 2
    row = lambda i: (i, 0)
    col = lambda i: (0, i)
    const = lambda i: (0, 0)
    full = lambda a: pl.BlockSpec(a.shape, const)
    return pl.pallas_call(
        _merge_route_kernel,
        grid=(t // tm,),
        in_specs=[
            pl.BlockSpec((tm, d), row), full(lg), full(lb),
            pl.BlockSpec((tm, V_WIDTH), row), pl.BlockSpec((tm, c), row),
            pl.BlockSpec((tm, d), row), pl.BlockSpec((tm, d), row),
            full(wao), full(wco), full(bco), full(wout), full(g1), full(b1),
            full(wrt), full(rb),
        ],
        out_specs=[
            pl.BlockSpec((tm, d), row),
            pl.BlockSpec((tm * SUBLANES, LANES), row),
            pl.BlockSpec((TOP_K, tm), col),
            pl.BlockSpec((TOP_K, tm), col),
            pl.BlockSpec((TOP_K, tm), col),
            pl.BlockSpec((N_EXPERTS, 1), const),
        ],
        out_shape=[
            jax.ShapeDtypeStruct((t, d), F32),
            jax.ShapeDtypeStruct((t * SUBLANES, LANES), F32),
            jax.ShapeDtypeStruct((TOP_K, t), jnp.int32),
            jax.ShapeDtypeStruct((TOP_K, t), F32),
            jax.ShapeDtypeStruct((TOP_K, t), jnp.int32),
            jax.ShapeDtypeStruct((N_EXPERTS, 1), F32),
        ],
        scratch_shapes=[pltpu.VMEM((N_EXPERTS, 1), F32)],
        compiler_params=pltpu.CompilerParams(
            dimension_semantics=("arbitrary",), vmem_limit_bytes=VMEM_LIMIT),
        name="merge_route",
    )(x2, lg, lb, oa, z, ga, gc, wao, wco, bco, wout, g1, b1, wrt, rb)


def _dest_kernel(te_ref, rk_ref, start_ref, d_ref):
    tm = te_ref.shape[1]
    eiota = lax.broadcasted_iota(jnp.int32, (N_EXPERTS, tm), 0)
    start = start_ref[...]
    rows = [jnp.sum(jnp.where(eiota == te_ref[k:k + 1, :], start, 0.0), axis=0, keepdims=True)
            for k in range(TOP_K)]
    d_ref[...] = (jnp.concatenate(rows, axis=0).astype(jnp.int32) + rk_ref[...]) * SUBLANES


def _dest(top_e, rank, start_col, tm):
    t = top_e.shape[1]
    col = lambda i: (0, i)
    return pl.pallas_call(
        _dest_kernel,
        grid=(t // tm,),
        in_specs=[
            pl.BlockSpec((TOP_K, tm), col),
            pl.BlockSpec((TOP_K, tm), col),
            pl.BlockSpec((N_EXPERTS, 1), lambda i: (0, 0)),
        ],
        out_specs=pl.BlockSpec((TOP_K, tm), col),
        out_shape=jax.ShapeDtypeStruct((TOP_K, t), jnp.int32),
        compiler_params=pltpu.CompilerParams(dimension_semantics=("arbitrary",)),
        name="moe_dest",
    )(top_e, rank, start_col)


def _dispatch_kernel(dest_ref, h_ref, xs_ref, zeros, sem, pad_sem, *, n_tok):
    tm = h_ref.shape[0] // SUBLANES
    base = pl.program_id(0) * tm

    @pl.when(pl.program_id(0) == 0)
    def _():
        zeros[...] = jnp.zeros(zeros.shape, zeros.dtype)
        pad = pltpu.make_async_copy(
            zeros, xs_ref.at[pl.ds(n_tok * TOP_K * SUBLANES, zeros.shape[0])], pad_sem)
        pad.start()
        pad.wait()

    for r in range(tm):
        for k in range(TOP_K):
            slot = pl.multiple_of(dest_ref[k * n_tok + base + r], SUBLANES)
            pltpu.make_async_copy(h_ref.at[pl.ds(r * SUBLANES, SUBLANES)],
                                  xs_ref.at[pl.ds(slot, SUBLANES)], sem).start(priority=k % 2)
    for k in range(TOP_K):
        pltpu.make_async_copy(h_ref, xs_ref.at[pl.ds(0, tm * SUBLANES)], sem).wait()


def _dispatch(dest_flat, h_tiles, tm):
    t = h_tiles.shape[0] // SUBLANES
    return pl.pallas_call(
        functools.partial(_dispatch_kernel, n_tok=t),
        grid_spec=pltpu.PrefetchScalarGridSpec(
            num_scalar_prefetch=1,
            grid=(t // tm,),
            in_specs=[pl.BlockSpec((tm * SUBLANES, LANES), lambda i, dest: (i, 0))],
            out_specs=pl.BlockSpec(memory_space=pl.ANY),
            scratch_shapes=[pltpu.VMEM((EXPERT_CHUNK * SUBLANES, LANES), h_tiles.dtype),
                            pltpu.SemaphoreType.DMA, pltpu.SemaphoreType.DMA],
        ),
        out_shape=jax.ShapeDtypeStruct(((t * TOP_K + EXPERT_CHUNK) * SUBLANES, LANES),
                                       h_tiles.dtype),
        compiler_params=pltpu.CompilerParams(
            dimension_semantics=("arbitrary",), has_side_effects=True),
        name="moe_dispatch",
    )(dest_flat, h_tiles)


EXPERT_CHUNK = 256
CHUNK_PIECES = tuple(EXPERT_CHUNK >> i for i in range(EXPERT_CHUNK.bit_length()))


def _experts_kernel(start_ref, count_ref, next_ref, first_ref,
                    xs_ref, wg_ref, wu_ref, wd_ref, ys_ref,
                    xbuf, ybuf, sem_x, sem_y, wg_bf, wu_bf, wd_bf, state):
    e = pl.program_id(0)
    ch = EXPERT_CHUNK
    count = count_ref[e]
    start = start_ref[e]

    @pl.when(e == 0)
    def _():
        state[0] = 0
        state[1] = 0
        state[2] = 0

    def fetch(row, slot):
        first = pl.multiple_of(row * SUBLANES, SUBLANES)
        return pltpu.make_async_copy(xs_ref.at[pl.ds(first, ch * SUBLANES)], xbuf.at[slot],
                                     sem_x.at[slot])

    def write_back(row, slot, rows, wait_only):
        for piece in CHUNK_PIECES:
            off = (rows // (2 * piece)) * (2 * piece)

            @pl.when((rows & piece) != 0)
            def _():
                src = pl.multiple_of(off * SUBLANES, SUBLANES)
                dst = pl.multiple_of((row + off) * SUBLANES, SUBLANES)
                copy = pltpu.make_async_copy(
                    ybuf.at[slot, pl.ds(src, piece * SUBLANES)],
                    ys_ref.at[pl.ds(dst, piece * SUBLANES)], sem_y.at[slot])
                if wait_only:
                    copy.wait()
                else:
                    copy.start()

    @pl.when(count > 0)
    def _():
        wg_bf[...] = wg_ref[...].astype(BF16)
        wu_bf[...] = wu_ref[...].astype(BF16)
        wd_bf[...] = wd_ref[...].astype(BF16)
        done = state[0]
        n_chunks = (count + ch - 1) // ch

        @pl.when(e == first_ref[0])
        def _():
            fetch(start, done % 2).start()

        def body(j, carry):
            slot = (done + j) % 2
            row = start + j * ch
            more = j + 1 < n_chunks
            following = next_ref[e]

            @pl.when(more)
            def _():
                fetch(row + ch, 1 - slot).start()

            @pl.when(jnp.logical_and(jnp.logical_not(more), following >= 0))
            def _():
                fetch(start_ref[jnp.maximum(following, 0)], 1 - slot).start()

            fetch(row, slot).wait()
            x = _rows_from_tiles(xbuf.at[slot], ch).astype(BF16)
            a = _silu(_dot(x, wg_bf[...])) * _dot(x, wu_bf[...])
            y = _dot(a.astype(BF16), wd_bf[...])

            write_back(0, slot, state[1 + slot], wait_only=True)
            _rows_to_tiles(ybuf.at[slot], y)
            rows = jnp.minimum(count - j * ch, ch)
            write_back(row, slot, rows, wait_only=False)
            state[1 + slot] = rows
            return carry

        lax.fori_loop(0, n_chunks, body, 0)
        state[0] = done + n_chunks

    @pl.when(e == pl.num_programs(0) - 1)
    def _():
        for slot in range(2):
            write_back(0, slot, state[1 + slot], wait_only=True)
            state[1 + slot] = 0


def _experts(tables, xs, wg, wu, wd, n_rows):
    d = SUBLANES * LANES
    n_exp, _, hid = wg.shape
    wmap = lambda e, start, count, following, first: (e, 0, 0)
    return pl.pallas_call(
        _experts_kernel,
        grid_spec=pltpu.PrefetchScalarGridSpec(
            num_scalar_prefetch=len(tables),
            grid=(n_exp,),
            in_specs=[
                pl.BlockSpec(memory_space=pl.ANY),
                pl.BlockSpec((None, d, hid), wmap),
                pl.BlockSpec((None, d, hid), wmap),
                pl.BlockSpec((None, hid, d), wmap),
            ],
            out_specs=pl.BlockSpec(memory_space=pl.ANY),
            scratch_shapes=[
                pltpu.VMEM((2, EXPERT_CHUNK * SUBLANES, LANES), F32),
                pltpu.VMEM((2, EXPERT_CHUNK * SUBLANES, LANES), F32),
                pltpu.SemaphoreType.DMA((2,)),
                pltpu.SemaphoreType.DMA((2,)),
                pltpu.VMEM((d, hid), BF16), pltpu.VMEM((d, hid), BF16),
                pltpu.VMEM((hid, d), BF16),
                pltpu.SMEM((3,), jnp.int32),
            ],
        ),
        out_shape=jax.ShapeDtypeStruct((n_rows * SUBLANES, LANES), xs.dtype),
        compiler_params=pltpu.CompilerParams(
            dimension_semantics=("arbitrary",), vmem_limit_bytes=VMEM_LIMIT,
            has_side_effects=True),
        name="moe_experts",
    )(*tables, xs, wg, wu, wd)


def _combine_kernel(dest_ref, h_ref, gw_ref, ys_ref, wgs_ref, wus_ref, wds_ref, g2_ref, b2_ref,
                    o_ref, buf, sem, *, n_tok):
    tm = h_ref.shape[0] // 2
    step = pl.program_id(0)

    def gather(tile, into):
        base = tile * tm
        for r in range(tm):
            for k in range(TOP_K):
                slot = pl.multiple_of(dest_ref[k * n_tok + base + r], SUBLANES)
                pltpu.make_async_copy(ys_ref.at[pl.ds(slot, SUBLANES)],
                                      buf.at[into, k, pl.ds(r * SUBLANES, SUBLANES)],
                                      sem.at[into]).start(priority=k % 2)

    def finish(half):
        rows = slice(half * tm, (half + 1) * tm)
        h = h_ref[rows, :]
        hb = h.astype(BF16)
        shared = _dot((_silu(_dot(hb, wgs_ref[...])) * _dot(hb, wus_ref[...])).astype(BF16),
                      wds_ref[...])
        acc = ALPHA * h + shared
        for k in range(TOP_K):
            pltpu.make_async_copy(ys_ref.at[pl.ds(0, tm * SUBLANES)], buf.at[half, k],
                                  sem.at[half]).wait()
        gw = gw_ref[rows, :]
        for k in range(TOP_K):
            acc = acc + _rows_from_tiles(buf.at[half, k], tm) * gw[:, k:k + 1]
        o_ref[rows, :] = _ln(acc, g2_ref[...], b2_ref[...])

    @pl.when(step == 0)
    def _():
        gather(0, 0)

    gather(2 * step + 1, 1)
    finish(0)

    @pl.when(step + 1 < pl.num_programs(0))
    def _():
        gather(2 * step + 2, 0)

    finish(1)


def _combine(dest_flat, h1, gw, ys, wgs, wus, wds, g2, b2, tm):
    t, d = h1.shape
    row = lambda i, dest: (i, 0)
    const = lambda i, dest: (0, 0)
    full = lambda a: pl.BlockSpec(a.shape, const)
    return pl.pallas_call(
        functools.partial(_combine_kernel, n_tok=t),
        grid_spec=pltpu.PrefetchScalarGridSpec(
            num_scalar_prefetch=1,
            grid=(t // (2 * tm),),
            in_specs=[
                pl.BlockSpec((2 * tm, d), row),
                pl.BlockSpec((2 * tm, TOP_K), row),
                pl.BlockSpec(memory_space=pl.ANY),
                full(wgs), full(wus), full(wds), full(g2), full(b2),
            ],
            out_specs=pl.BlockSpec((2 * tm, d), row),
            scratch_shapes=[pltpu.VMEM((2, TOP_K, tm * SUBLANES, LANES), ys.dtype),
                            pltpu.SemaphoreType.DMA((2,))],
        ),
        out_shape=jax.ShapeDtypeStruct((t, d), F32),
        compiler_params=pltpu.CompilerParams(
            dimension_semantics=("arbitrary",), vmem_limit_bytes=VMEM_LIMIT),
        name="moe_combine",
    )(dest_flat, h1, gw, ys, wgs, wus, wds, g2, b2)


def _expert_tables(counts):
    n = counts.shape[0]
    start = (jnp.cumsum(counts) - counts).astype(jnp.int32)
    ids = jnp.arange(n, dtype=jnp.int32)
    later_busy = jnp.logical_and(ids[None, :] > ids[:, None], counts[None, :] > 0)
    following = jnp.min(jnp.where(later_busy, ids[None, :], n), axis=1)
    following = jnp.where(following >= n, -1, following).astype(jnp.int32)
    first = jnp.min(jnp.where(counts > 0, ids, n)).reshape(1).astype(jnp.int32)
    return start, counts.astype(jnp.int32), following, first


def _rope_tables(positions):
    inv = 1.0 / (ROPE_THETA ** (jnp.arange(0, HEAD_DIM, 2, dtype=F32) / HEAD_DIM))
    ang = positions.astype(F32)[:, None] * inv[None, :]
    cos = jnp.cos(ang)
    sin = jnp.sin(ang)
    cos = jnp.concatenate([cos, cos], axis=-1)
    sin = jnp.concatenate([-sin, sin], axis=-1)
    reps = V_DIM // HEAD_DIM
    return jnp.tile(cos, (1, reps)), jnp.tile(sin, (1, reps)), cos.T, sin.T


def _tile(n, pref):
    return pref if n % pref == 0 else n


def _tiles(seq, n_tok):
    return dict(q_rows=_tile(seq, 512), key_rows=_tile(seq, 256),
                merge_rows=_tile(n_tok, 512), dest_rows=_tile(n_tok, 512),
                move_rows=_tile(n_tok, 256))


def kernel(x, meta_tokens, ln_in_g, ln_in_b, w_in, lambda_q1, lambda_k1, lambda_q2, lambda_k2, subln_g, w_attn_o, conv_w, conv_b, conv_ln_g, conv_ln_b, w_conv_o, b_conv_o, w_out, ln1_g, ln1_b, w_router, router_bias, w_gate_e, w_up_e, w_down_e, w_gate_s, w_up_s, w_down_s, ln2_g, ln2_b):
    bsz, seq, d = x.shape
    t = bsz * seq
    l = 0
    row = lambda a: a.reshape(1, -1).astype(F32)
    x2 = x.reshape(t, d)
    lg, lb = row(ln_in_g), row(ln_in_b)
    w_in_bf = w_in[l].astype(BF16)
    o_v = 2 * QK_WIDTH
    wqt = w_in_bf[:, :QK_WIDTH].T
    wk = w_in_bf[:, QK_WIDTH:o_v]
    wvt = w_in_bf[:, o_v:o_v + V_WIDTH].T
    wr = w_in_bf[:, o_v + V_WIDTH:]

    tables_x = _rope_tables(jnp.arange(N_META, N_META + seq))
    tables_m = _rope_tables(jnp.arange(N_META))

    tiles = _tiles(seq, t)
    tq, tk = tiles["q_rows"], tiles["key_rows"]
    qt, k, vt, u, ga, gc = _inproj(x2, lg, lb, wqt, wk, wvt, wr, tables_x, tq, tk)
    _, km, vmt, um, _, _ = _inproj(meta_tokens.astype(F32), lg, lb, wqt, wk, wvt, wr, tables_m,
                                   N_META, N_META)

    lam = (jnp.exp(jnp.sum(lambda_q1[l].astype(F32) * lambda_k1[l].astype(F32)))
           - jnp.exp(jnp.sum(lambda_q2[l].astype(F32) * lambda_k2[l].astype(F32)))
           + LAM_INIT).reshape(1, 1)
    o_attn = _attention(lam, qt, k, vt, km, vmt.reshape(V_WIDTH, N_META),
                        subln_g[l].reshape(-1, 1).astype(F32), bsz, seq, tq, tk)

    z = _conv(u, um, conv_w[l].astype(F32), row(conv_b[l]), row(conv_ln_g[l]),
              row(conv_ln_b[l]), bsz, seq)

    assert d == SUBLANES * LANES, "the row-tile layout assumes one (8, 128) tile per token row"
    h1, h1_tiles, top_e, gw_t, rank, counts = _merge_route(
        x2, lg, lb, o_attn, z, ga, gc,
        w_attn_o[l].astype(BF16), w_conv_o[l].astype(BF16), row(b_conv_o[l]),
        w_out[l].astype(BF16), row(ln1_g[l]), row(ln1_b[l]),
        w_router[l].T.astype(BF16), router_bias[l].reshape(-1, 1).astype(F32),
        tiles["merge_rows"])

    tables = _expert_tables(counts.reshape(-1).astype(jnp.int32))
    start = tables[0]
    dest = _dest(top_e, rank, start.astype(F32).reshape(-1, 1), tiles["dest_rows"]).reshape(-1)

    xs = _dispatch(dest, h1_tiles, tiles["move_rows"])
    ys = _experts(tables, xs, w_gate_e[l], w_up_e[l], w_down_e[l], t * TOP_K)
    out = _combine(dest, h1, gw_t.T, ys,
                   w_gate_s[l].astype(BF16), w_up_s[l].astype(BF16), w_down_s[l].astype(BF16),
                   row(ln2_g[l]), row(ln2_b[l]), tiles["move_rows"])
    return out.reshape(bsz, seq, d)
```

```python
import functools
import math

import jax
import jax.numpy as jnp
from jax import lax
from jax.experimental import pallas as pl
from jax.experimental.pallas import tpu as pltpu

N_META = 16
HEADS = 4
HEAD_DIM = 64
V_DIM = 2 * HEAD_DIM
QK_WIDTH = HEADS * 2 * HEAD_DIM
V_WIDTH = HEADS * V_DIM
ROPE_THETA = 10000.0
CONV_WIDTH = 31
N_EXPERTS = 256
TOP_K = 8
N_GROUPS = 8
TOPK_GROUPS = 4
GROUP_SIZE = N_EXPERTS // N_GROUPS
ROUTED_SCALE = 2.5
LN_EPS = 1e-5
DEPTH = 1
ALPHA = (2.0 * DEPTH) ** 0.25
LAM_INIT = 0.8 - 0.6 * math.exp(-0.3 * 0)

F32 = jnp.float32
BF16 = jnp.bfloat16
NEG_INF = float("-inf")

VMEM_LIMIT = 56 * 1024 * 1024


def _ln(x, g, b):
    mu = jnp.mean(x, axis=-1, keepdims=True)
    xc = x - mu
    var = jnp.mean(xc * xc, axis=-1, keepdims=True)
    return xc * lax.rsqrt(var + LN_EPS) * g + b


def _dot(a, b):
    return jnp.dot(a, b, preferred_element_type=F32)


def _dot_nt(a, b):
    return lax.dot_general(a, b, (((1,), (1,)), ((), ())), preferred_element_type=F32)


def _silu(x):
    return x * jax.nn.sigmoid(x)


SUBLANES = 8
LANES = 128


def _rows_from_tiles(ref, n_rows, first=0):
    return jnp.concatenate(
        [ref[pl.ds(first * SUBLANES + c, n_rows, stride=SUBLANES), :] for c in range(SUBLANES)],
        axis=1)


def _rows_to_tiles(ref, x):
    for c in range(SUBLANES):
        ref[pl.ds(c, x.shape[0], stride=SUBLANES), :] = x[:, c * LANES:(c + 1) * LANES]


LOG2E = math.log2(math.e)
HALF_HEAD = HEAD_DIM // 2


def _inproj_kernel(x_ref, g_ref, b_ref, wqt_ref, wk_ref, wvt_ref, wr_ref,
                   cos_ref, sin_ref, cost_ref, sint_ref,
                   qt_ref, k_ref, vt_ref, u_ref, ga_ref, gc_ref, *, d_model):
    h = _ln(x_ref[...], g_ref[...], b_ref[...]).astype(BF16)
    tm = x_ref.shape[0]
    c = d_model // 2

    qt = _dot_nt(wqt_ref[...], h)
    pieces = []
    for g in range(QK_WIDTH // HEAD_DIM):
        lo = g * HEAD_DIM
        pieces.append(qt[lo + HALF_HEAD:lo + HEAD_DIM, :])
        pieces.append(qt[lo:lo + HALF_HEAD, :])
    partner = jnp.concatenate(pieces, axis=0)
    reps_t = QK_WIDTH // cost_ref.shape[0]
    qt = qt * jnp.tile(cost_ref[...], (reps_t, 1)) + partner * jnp.tile(sint_ref[...], (reps_t, 1))
    qt_ref[...] = (qt * (HEAD_DIM ** -0.5 * LOG2E)).astype(BF16)

    kk = _dot(h, wk_ref[...])
    reps = QK_WIDTH // cos_ref.shape[1]
    lane = lax.broadcasted_iota(jnp.int32, (tm, QK_WIDTH), 1)
    partner = jnp.where((lane % HEAD_DIM) < HALF_HEAD,
                        pltpu.roll(kk, QK_WIDTH - HALF_HEAD, 1),
                        pltpu.roll(kk, HALF_HEAD, 1))
    kk = kk * jnp.tile(cos_ref[...], (1, reps)) + partner * jnp.tile(sin_ref[...], (1, reps))
    k_ref[...] = kk.astype(BF16)

    vt = _dot_nt(wvt_ref[...], h).astype(BF16)
    chunk = vt_ref.shape[2]
    for j in range(vt_ref.shape[0]):
        vt_ref[j] = vt[:, j * chunk:(j + 1) * chunk]

    def proj(lo, width):
        return _dot(h, wr_ref[:, lo:lo + width])

    u_ref[...] = proj(0, c) * jax.nn.sigmoid(proj(c, c))
    ga_ref[...] = jax.nn.sigmoid(proj(2 * c, d_model)).astype(BF16)
    gc_ref[...] = jax.nn.sigmoid(proj(2 * c + d_model, d_model)).astype(BF16)


def _inproj(x2, g, b, wqt, wk, wvt, wr, tables, tm, chunk):
    t, d = x2.shape
    c = d // 2
    cos, sin, cost, sint = tables
    n_pos_tiles = cos.shape[0] // tm
    row = lambda i: (i, 0)
    col = lambda i: (0, i)
    const = lambda i: (0, 0)
    pos = lambda i: (i % n_pos_tiles, 0)
    post = lambda i: (0, i % n_pos_tiles)
    full = lambda a: pl.BlockSpec(a.shape, const)
    return pl.pallas_call(
        functools.partial(_inproj_kernel, d_model=d),
        grid=(t // tm,),
        in_specs=[
            pl.BlockSpec((tm, d), row), full(g), full(b),
            full(wqt), full(wk), full(wvt), full(wr),
            pl.BlockSpec((tm, cos.shape[1]), pos),
            pl.BlockSpec((tm, cos.shape[1]), pos),
            pl.BlockSpec((cost.shape[0], tm), post),
            pl.BlockSpec((cost.shape[0], tm), post),
        ],
        out_specs=[
            pl.BlockSpec((QK_WIDTH, tm), col),
            pl.BlockSpec((tm, QK_WIDTH), row),
            pl.BlockSpec((tm // chunk, V_WIDTH, chunk), lambda i: (i, 0, 0)),
            pl.BlockSpec((tm, c), row),
            pl.BlockSpec((tm, d), row),
            pl.BlockSpec((tm, d), row),
        ],
        out_shape=[
            jax.ShapeDtypeStruct((QK_WIDTH, t), BF16),
            jax.ShapeDtypeStruct((t, QK_WIDTH), BF16),
            jax.ShapeDtypeStruct((t // chunk, V_WIDTH, chunk), BF16),
            jax.ShapeDtypeStruct((t, c), F32),
            jax.ShapeDtypeStruct((t, d), BF16),
            jax.ShapeDtypeStruct((t, d), BF16),
        ],
        compiler_params=pltpu.CompilerParams(
            dimension_semantics=("arbitrary",), vmem_limit_bytes=VMEM_LIMIT),
        name="inproj",
    )(x2, g, b, wqt, wk, wvt, wr, cos, sin, cost, sint)


def _attn_kernel(lam_ref, qt_ref, k_ref, vt_ref, km_ref, vmt_ref, g_ref, o_ref, acc_sc, *, tq, tk):
    i = pl.program_id(2)
    qt = qt_ref[...]
    feat = lax.broadcasted_iota(jnp.int32, qt.shape, 0)
    zero = jnp.zeros_like(qt)
    q2 = jnp.concatenate([jnp.where(feat < HEAD_DIM, qt, zero),
                          jnp.where(feat >= HEAD_DIM, qt, zero)], axis=1)

    def scores(j):
        r0 = pl.multiple_of(j * tk, tk)
        return _dot(k_ref[pl.ds(r0, tk), :], q2)

    def absorb(m_old, l_old, s, vtc):
        m_new = jnp.maximum(m_old, jnp.max(s, axis=0, keepdims=True))
        alpha = jnp.exp2(m_old - m_new)
        p = jnp.exp2(s - m_new)
        acc_sc[...] = alpha * acc_sc[...] + _dot(vtc, p.astype(BF16))
        return m_new, alpha * l_old + jnp.sum(p, axis=0, keepdims=True)

    s = _dot(km_ref[...], q2)
    m = jnp.max(s, axis=0, keepdims=True)
    p = jnp.exp2(s - m)
    acc_sc[...] = _dot(vmt_ref[...], p.astype(BF16))
    l = jnp.sum(p, axis=0, keepdims=True)

    per_tile = tq // tk
    n_full = i * per_tile

    def body(j, carry):
        m_old, l_old, s_cur = carry
        s_next = scores(j + 1)
        m_new, l_new = absorb(m_old, l_old, s_cur, vt_ref[j])
        return m_new, l_new, s_next

    m, l, s = lax.fori_loop(0, n_full, body, (m, l, scores(0)))

    key = lax.broadcasted_iota(jnp.int32, (tk, 2 * tq), 0)
    qry = lax.broadcasted_iota(jnp.int32, (tk, 2 * tq), 1) % tq
    for d in range(per_tile):
        s_cur = s
        if d + 1 < per_tile:
            s = scores(n_full + d + 1)
        m, l = absorb(m, l, jnp.where(key + d * tk <= qry, s_cur, NEG_INF), vt_ref[n_full + d])

    lam = lam_ref[0, 0]
    o = acc_sc[:, :tq] / l[:, :tq] - lam * (acc_sc[:, tq:] / l[:, tq:])
    o = o * lax.rsqrt(jnp.mean(o * o, axis=0, keepdims=True) + LN_EPS) * g_ref[...]
    o_ref[...] = (o * (1.0 - LAM_INIT)).T.astype(o_ref.dtype)


def _attention(lam, qt, k, vt, km, vmt, subln_col, bsz, seq, tq, tk):
    t = k.shape[0]
    nq = seq // tq
    nk = seq // tk
    return pl.pallas_call(
        functools.partial(_attn_kernel, tq=tq, tk=tk),
        grid=(bsz, HEADS, nq),
        in_specs=[
            pl.BlockSpec(memory_space=pltpu.SMEM),
            pl.BlockSpec((V_DIM, tq), lambda b, h, i: (h, b * nq + i)),
            pl.BlockSpec((seq, V_DIM), lambda b, h, i: (b, h)),
            pl.BlockSpec((nk, V_DIM, tk), lambda b, h, i: (b, h, 0)),
            pl.BlockSpec((N_META, V_DIM), lambda b, h, i: (0, h)),
            pl.BlockSpec((V_DIM, N_META), lambda b, h, i: (h, 0)),
            pl.BlockSpec((V_DIM, 1), lambda b, h, i: (0, 0)),
        ],
        out_specs=pl.BlockSpec((tq, V_DIM), lambda b, h, i: (b * nq + i, h)),
        out_shape=jax.ShapeDtypeStruct((t, V_WIDTH), BF16),
        scratch_shapes=[pltpu.VMEM((V_DIM, 2 * tq), F32)],
        compiler_params=pltpu.CompilerParams(
            dimension_semantics=("arbitrary", "arbitrary", "arbitrary"),
            vmem_limit_bytes=VMEM_LIMIT),
        name="diff_attention",
    )(lam, qt, k, vt, km, vmt, subln_col)


CONV_ROWS = 32
CONV_WINDOW = 64
CONV_PAD = 32


def _conv_kernel(u_ref, um_ref, w_ref, cb_ref, g_ref, b_ref, z_ref, ucat, grp, *, seq):
    c = u_ref.shape[1]
    ucat[0:CONV_PAD - N_META, :] = jnp.zeros((CONV_PAD - N_META, c), F32)
    ucat[CONV_PAD - N_META:CONV_PAD, :] = um_ref[...]
    ucat[CONV_PAD:CONV_PAD + seq, :] = u_ref[...]
    shift = CONV_PAD - (CONV_WIDTH - 1)

    def body(t, carry):
        r0 = pl.multiple_of(t * CONV_ROWS, CONV_ROWS)
        acc = jnp.zeros((CONV_ROWS, c), F32)
        for r in range(SUBLANES):
            taps = [j for j in range(CONV_WIDTH) if (shift + j) % SUBLANES == r]
            if not taps:
                continue
            lo = shift + taps[0]
            rows = taps[-1] - taps[0] + CONV_ROWS
            start = pl.multiple_of(r0 + lo - r, SUBLANES)
            span = rows + (SUBLANES if r else 0)
            grp[r, 0:rows, :] = ucat[pl.ds(start, span), :][r:r + rows, :]
            for j in taps:
                off = shift + j - lo
                acc = acc + grp[r, off:off + CONV_ROWS, :] * w_ref[j:j + 1, :]
        y = _ln(acc + cb_ref[...], g_ref[...], b_ref[...])
        z_ref[pl.ds(r0, CONV_ROWS), :] = _silu(y).astype(z_ref.dtype)
        return carry

    lax.fori_loop(0, seq // CONV_ROWS, body, 0)


def _conv(u, um, conv_w, conv_b, ln_g, ln_b, bsz, seq):
    t, c = u.shape
    const = lambda b: (0, 0)
    return pl.pallas_call(
        functools.partial(_conv_kernel, seq=seq),
        grid=(bsz,),
        in_specs=[
            pl.BlockSpec((seq, c), lambda b: (b, 0)),
            pl.BlockSpec((N_META, c), const),
            pl.BlockSpec((CONV_WIDTH, c), const),
            pl.BlockSpec((1, c), const),
            pl.BlockSpec((1, c), const),
            pl.BlockSpec((1, c), const),
        ],
        out_specs=pl.BlockSpec((seq, c), lambda b: (b, 0)),
        out_shape=jax.ShapeDtypeStruct((t, c), BF16),
        scratch_shapes=[pltpu.VMEM((CONV_PAD + seq, c), F32),
                        pltpu.VMEM((SUBLANES, CONV_WINDOW, c), F32)],
        compiler_params=pltpu.CompilerParams(
            dimension_semantics=("arbitrary",), vmem_limit_bytes=VMEM_LIMIT),
        name="conformer_conv",
    )(u, um, conv_w, conv_b, ln_g, ln_b)


def _merge_route_kernel(x_ref, lg_ref, lb_ref, oa_ref, z_ref, ga_ref, gc_ref,
                        wao_ref, wco_ref, bco_ref, wout_ref, g1_ref, b1_ref,
                        wrt_ref, rb_ref,
                        h1_ref, h1t_ref, te_ref, gw_ref, rk_ref, cnt_ref, carry_sc):
    tm = x_ref.shape[0]

    @pl.when(pl.program_id(0) == 0)
    def _():
        carry_sc[...] = jnp.zeros(carry_sc.shape, F32)

    h0 = _ln(x_ref[...], lg_ref[...], lb_ref[...])
    y_attn = _dot(oa_ref[...], wao_ref[...])
    y_conv = _dot(z_ref[...], wco_ref[...]) + bco_ref[...]
    merged = ga_ref[...].astype(F32) * y_attn + gc_ref[...].astype(F32) * y_conv
    h1 = _ln(ALPHA * h0 + _dot(merged.astype(BF16), wout_ref[...]), g1_ref[...], b1_ref[...])
    h1_ref[...] = h1
    _rows_to_tiles(h1t_ref, h1)
    h1b = h1.astype(BF16)

    scores = jax.nn.sigmoid(_dot_nt(wrt_ref[...], h1b))
    sel = scores + rb_ref[...]

    sub = lax.broadcasted_iota(jnp.int32, (GROUP_SIZE, tm), 0)
    gscore = []
    for g in range(N_GROUPS):
        slab = sel[g * GROUP_SIZE:(g + 1) * GROUP_SIZE, :]
        top1 = jnp.max(slab, axis=0, keepdims=True)
        arg1 = jnp.min(jnp.where(slab == top1, sub, GROUP_SIZE), axis=0, keepdims=True)
        top2 = jnp.max(jnp.where(sub == arg1, NEG_INF, slab), axis=0, keepdims=True)
        gscore.append(top1 + top2)

    eiota = lax.broadcasted_iota(jnp.int32, (N_EXPERTS, tm), 0)
    egroup = eiota // GROUP_SIZE
    allowed = jnp.zeros((N_EXPERTS, tm), F32)
    for _ in range(TOPK_GROUPS):
        best = functools.reduce(jnp.maximum, gscore)
        pick = jnp.full((1, tm), N_GROUPS, jnp.int32)
        for g in reversed(range(N_GROUPS)):
            pick = jnp.where(gscore[g] == best, g, pick)
        gscore = [jnp.where(pick == g, NEG_INF, gscore[g]) for g in range(N_GROUPS)]
        allowed = jnp.where(egroup == pick, 1.0, allowed)

    masked = jnp.where(allowed > 0.5, sel, NEG_INF)

    picks = []
    weights = []
    chosen = jnp.zeros((N_EXPERTS, tm), F32)
    for _ in range(TOP_K):
        best = jnp.max(masked, axis=0, keepdims=True)
        pick = jnp.min(jnp.where(masked == best, eiota, N_EXPERTS), axis=0, keepdims=True)
        hit = eiota == pick
        picks.append(pick)
        weights.append(jnp.sum(jnp.where(hit, scores, 0.0), axis=0, keepdims=True))
        chosen = jnp.where(hit, 1.0, chosen)
        masked = jnp.where(hit, NEG_INF, masked)

    gw = jnp.concatenate(weights, axis=0)
    gw_ref[...] = gw / jnp.sum(gw, axis=0, keepdims=True) * ROUTED_SCALE
    te_ref[...] = jnp.concatenate(picks, axis=0)

    before = (lax.broadcasted_iota(jnp.int32, (tm, tm), 0)
              < lax.broadcasted_iota(jnp.int32, (tm, tm), 1))
    prior = _dot(chosen.astype(BF16), jnp.where(before, 1.0, 0.0).astype(BF16)) + carry_sc[...]
    ranks = [jnp.sum(jnp.where(eiota == p, prior, 0.0), axis=0, keepdims=True) for p in picks]
    rk_ref[...] = jnp.concatenate(ranks, axis=0).astype(jnp.int32)
    carry_sc[...] = carry_sc[...] + jnp.sum(chosen, axis=1, keepdims=True)
    cnt_ref[...] = carry_sc[...]


def _merge_route(x2, lg, lb, oa, z, ga, gc, wao, wco, bco, wout, g1, b1, wrt, rb, tm):
    t, d = x2.shape
    c = d // 2
    row = lambda i: (i, 0)
    col = lambda i: (0, i)
    const = lambda i: (0, 0)
    full = lambda a: pl.BlockSpec(a.shape, const)
    return pl.pallas_call(
        _merge_route_kernel,
        grid=(t // tm,),
        in_specs=[
            pl.BlockSpec((tm, d), row), full(lg), full(lb),
            pl.BlockSpec((tm, V_WIDTH), row), pl.BlockSpec((tm, c), row),
            pl.BlockSpec((tm, d), row), pl.BlockSpec((tm, d), row),
            full(wao), full(wco), full(bco), full(wout), full(g1), full(b1),
            full(wrt), full(rb),
        ],
        out_specs=[
            pl.BlockSpec((tm, d), row),
            pl.BlockSpec((tm * SUBLANES, LANES), row),
            pl.BlockSpec((TOP_K, tm), col),
            pl.BlockSpec((TOP_K, tm), col),
            pl.BlockSpec((TOP_K, tm), col),
            pl.BlockSpec((N_EXPERTS, 1), const),
        ],
        out_shape=[
            jax.ShapeDtypeStruct((t, d), F32),
            jax.ShapeDtypeStruct((t * SUBLANES, LANES), F32),
            jax.ShapeDtypeStruct((TOP_K, t), jnp.int32),
            jax.ShapeDtypeStruct((TOP_K, t), F32),
            jax.ShapeDtypeStruct((TOP_K, t), jnp.int32),
            jax.ShapeDtypeStruct((N_EXPERTS, 1), F32),
        ],
        scratch_shapes=[pltpu.VMEM((N_EXPERTS, 1), F32)],
        compiler_params=pltpu.CompilerParams(
            dimension_semantics=("arbitrary",), vmem_limit_bytes=VMEM_LIMIT),
        name="merge_route",
    )(x2, lg, lb, oa, z, ga, gc, wao, wco, bco, wout, g1, b1, wrt, rb)


def _dest_kernel(te_ref, rk_ref, start_ref, d_ref):
    tm = te_ref.shape[1]
    eiota = lax.broadcasted_iota(jnp.int32, (N_EXPERTS, tm), 0)
    start = start_ref[...]
    rows = [jnp.sum(jnp.where(eiota == te_ref[k:k + 1, :], start, 0.0), axis=0, keepdims=True)
            for k in range(TOP_K)]
    d_ref[...] = (jnp.concatenate(rows, axis=0).astype(jnp.int32) + rk_ref[...]) * SUBLANES


def _dest(top_e, rank, start_col, tm):
    t = top_e.shape[1]
    col = lambda i: (0, i)
    return pl.pallas_call(
        _dest_kernel,
        grid=(t // tm,),
        in_specs=[
            pl.BlockSpec((TOP_K, tm), col),
            pl.BlockSpec((TOP_K, tm), col),
            pl.BlockSpec((N_EXPERTS, 1), lambda i: (0, 0)),
        ],
        out_specs=pl.BlockSpec((TOP_K, tm), col),
        out_shape=jax.ShapeDtypeStruct((TOP_K, t), jnp.int32),
        compiler_params=pltpu.CompilerParams(dimension_semantics=("arbitrary",)),
        name="moe_dest",
    )(top_e, rank, start_col)


def _dispatch_kernel(dest_ref, h_ref, xs_ref, zeros, sem, pad_sem, *, n_tok):
    tm = h_ref.shape[0] // SUBLANES
    base = pl.program_id(0) * tm

    @pl.when(pl.program_id(0) == 0)
    def _():
        zeros[...] = jnp.zeros(zeros.shape, zeros.dtype)
        pad = pltpu.make_async_copy(
            zeros, xs_ref.at[pl.ds(n_tok * TOP_K * SUBLANES, zeros.shape[0])], pad_sem)
        pad.start()
        pad.wait()

    for r in range(tm):
        for k in range(TOP_K):
            slot = pl.multiple_of(dest_ref[k * n_tok + base + r], SUBLANES)
            pltpu.make_async_copy(h_ref.at[pl.ds(r * SUBLANES, SUBLANES)],
                                  xs_ref.at[pl.ds(slot, SUBLANES)], sem).start(priority=k % 2)
    for k in range(TOP_K):
        pltpu.make_async_copy(h_ref, xs_ref.at[pl.ds(0, tm * SUBLANES)], sem).wait()


def _dispatch(dest_flat, h_tiles, tm):
    t = h_tiles.shape[0] // SUBLANES
    return pl.pallas_call(
        functools.partial(_dispatch_kernel, n_tok=t),
        grid_spec=pltpu.PrefetchScalarGridSpec(
            num_scalar_prefetch=1,
            grid=(t // tm,),
            in_specs=[pl.BlockSpec((tm * SUBLANES, LANES), lambda i, dest: (i, 0))],
            out_specs=pl.BlockSpec(memory_space=pl.ANY),
            scratch_shapes=[pltpu.VMEM((EXPERT_CHUNK * SUBLANES, LANES), h_tiles.dtype),
                            pltpu.SemaphoreType.DMA, pltpu.SemaphoreType.DMA],
        ),
        out_shape=jax.ShapeDtypeStruct(((t * TOP_K + EXPERT_CHUNK) * SUBLANES, LANES),
                                       h_tiles.dtype),
        compiler_params=pltpu.CompilerParams(
            dimension_semantics=("arbitrary",), has_side_effects=True),
        name="moe_dispatch",
    )(dest_flat, h_tiles)


EXPERT_CHUNK = 256
CHUNK_PIECES = tuple(EXPERT_CHUNK >> i for i in range(EXPERT_CHUNK.bit_length()))


def _experts_kernel(start_ref, count_ref, next_ref, first_ref,
                    xs_ref, wg_ref, wu_ref, wd_ref, ys_ref,
                    xbuf, ybuf, sem_x, sem_y, state):
    e = pl.program_id(0)
    ch = EXPERT_CHUNK
    count = count_ref[e]
    start = start_ref[e]

    @pl.when(e == 0)
    def _():
        state[0] = 0
        state[1] = 0
        state[2] = 0

    def fetch(row, slot):
        first = pl.multiple_of(row * SUBLANES, SUBLANES)
        return pltpu.make_async_copy(xs_ref.at[pl.ds(first, ch * SUBLANES)], xbuf.at[slot],
                                     sem_x.at[slot])

    def write_back(row, slot, rows, wait_only):
        for piece in CHUNK_PIECES:
            off = (rows // (2 * piece)) * (2 * piece)

            @pl.when((rows & piece) != 0)
            def _():
                src = pl.multiple_of(off * SUBLANES, SUBLANES)
                dst = pl.multiple_of((row + off) * SUBLANES, SUBLANES)
                copy = pltpu.make_async_copy(
                    ybuf.at[slot, pl.ds(src, piece * SUBLANES)],
                    ys_ref.at[pl.ds(dst, piece * SUBLANES)], sem_y.at[slot])
                if wait_only:
                    copy.wait()
                else:
                    copy.start()

    @pl.when(count > 0)
    def _():
        done = state[0]
        n_chunks = (count + ch - 1) // ch

        @pl.when(e == first_ref[0])
        def _():
            fetch(start, done % 2).start()

        def body(j, carry):
            slot = (done + j) % 2
            row = start + j * ch
            more = j + 1 < n_chunks
            following = next_ref[e]

            @pl.when(more)
            def _():
                fetch(row + ch, 1 - slot).start()

            @pl.when(jnp.logical_and(jnp.logical_not(more), following >= 0))
            def _():
                fetch(start_ref[jnp.maximum(following, 0)], 1 - slot).start()

            fetch(row, slot).wait()
            x = _rows_from_tiles(xbuf.at[slot], ch).astype(BF16)
            a = (_silu(_dot(x, wg_ref[...].astype(BF16)))
                 * _dot(x, wu_ref[...].astype(BF16)))
            y = _dot(a.astype(BF16), wd_ref[...].astype(BF16))

            write_back(0, slot, state[1 + slot], wait_only=True)
            _rows_to_tiles(ybuf.at[slot], y)
            rows = jnp.minimum(count - j * ch, ch)
            write_back(row, slot, rows, wait_only=False)
            state[1 + slot] = rows
            return carry

        lax.fori_loop(0, n_chunks, body, 0)
        state[0] = done + n_chunks

    @pl.when(e == pl.num_programs(0) - 1)
    def _():
        for slot in range(2):
            write_back(0, slot, state[1 + slot], wait_only=True)
            state[1 + slot] = 0


def _experts(tables, xs, wg, wu, wd, n_rows):
    d = SUBLANES * LANES
    n_exp, _, hid = wg.shape
    wmap = lambda e, start, count, following, first: (e, 0, 0)
    return pl.pallas_call(
        _experts_kernel,
        grid_spec=pltpu.PrefetchScalarGridSpec(
            num_scalar_prefetch=len(tables),
            grid=(n_exp,),
            in_specs=[
                pl.BlockSpec(memory_space=pl.ANY),
                pl.BlockSpec((None, d, hid), wmap),
                pl.BlockSpec((None, d, hid), wmap),
                pl.BlockSpec((None, hid, d), wmap),
            ],
            out_specs=pl.BlockSpec(memory_space=pl.ANY),
            scratch_shapes=[
                pltpu.VMEM((2, EXPERT_CHUNK * SUBLANES, LANES), F32),
                pltpu.VMEM((2, EXPERT_CHUNK * SUBLANES, LANES), F32),
                pltpu.SemaphoreType.DMA((2,)),
                pltpu.SemaphoreType.DMA((2,)),
                pltpu.SMEM((3,), jnp.int32),
            ],
        ),
        out_shape=jax.ShapeDtypeStruct((n_rows * SUBLANES, LANES), xs.dtype),
        compiler_params=pltpu.CompilerParams(
            dimension_semantics=("arbitrary",), vmem_limit_bytes=VMEM_LIMIT,
            has_side_effects=True),
        name="moe_experts",
    )(*tables, xs, wg, wu, wd)


def _combine_kernel(dest_ref, h_ref, gw_ref, ys_ref, wgs_ref, wus_ref, wds_ref, g2_ref, b2_ref,
                    o_ref, buf, sem, *, n_tok):
    tm = h_ref.shape[0] // 2
    step = pl.program_id(0)

    def gather(tile, into):
        base = tile * tm
        for r in range(tm):
            for k in range(TOP_K):
                slot = pl.multiple_of(dest_ref[k * n_tok + base + r], SUBLANES)
                pltpu.make_async_copy(ys_ref.at[pl.ds(slot, SUBLANES)],
                                      buf.at[into, k, pl.ds(r * SUBLANES, SUBLANES)],
                                      sem.at[into]).start(priority=k % 2)

    def finish(half):
        rows = slice(half * tm, (half + 1) * tm)
        h = h_ref[rows, :]
        hb = h.astype(BF16)
        shared = _dot((_silu(_dot(hb, wgs_ref[...])) * _dot(hb, wus_ref[...])).astype(BF16),
                      wds_ref[...])
        acc = ALPHA * h + shared
        for k in range(TOP_K):
            pltpu.make_async_copy(ys_ref.at[pl.ds(0, tm * SUBLANES)], buf.at[half, k],
                                  sem.at[half]).wait()
        gw = gw_ref[rows, :]
        for k in range(TOP_K):
            acc = acc + _rows_from_tiles(buf.at[half, k], tm) * gw[:, k:k + 1]
        o_ref[rows, :] = _ln(acc, g2_ref[...], b2_ref[...])

    @pl.when(step == 0)
    def _():
        gather(0, 0)

    gather(2 * step + 1, 1)
    finish(0)

    @pl.when(step + 1 < pl.num_programs(0))
    def _():
        gather(2 * step + 2, 0)

    finish(1)


def _combine(dest_flat, h1, gw, ys, wgs, wus, wds, g2, b2, tm):
    t, d = h1.shape
    row = lambda i, dest: (i, 0)
    const = lambda i, dest: (0, 0)
    full = lambda a: pl.BlockSpec(a.shape, const)
    return pl.pallas_call(
        functools.partial(_combine_kernel, n_tok=t),
        grid_spec=pltpu.PrefetchScalarGridSpec(
            num_scalar_prefetch=1,
            grid=(t // (2 * tm),),
            in_specs=[
                pl.BlockSpec((2 * tm, d), row),
                pl.BlockSpec((2 * tm, TOP_K), row),
                pl.BlockSpec(memory_space=pl.ANY),
                full(wgs), full(wus), full(wds), full(g2), full(b2),
            ],
            out_specs=pl.BlockSpec((2 * tm, d), row),
            scratch_shapes=[pltpu.VMEM((2, TOP_K, tm * SUBLANES, LANES), ys.dtype),
                            pltpu.SemaphoreType.DMA((2,))],
        ),
        out_shape=jax.ShapeDtypeStruct((t, d), F32),
        compiler_params=pltpu.CompilerParams(
            dimension_semantics=("arbitrary",), vmem_limit_bytes=VMEM_LIMIT),
        name="moe_combine",
    )(dest_flat, h1, gw, ys, wgs, wus, wds, g2, b2)


def _expert_tables(counts):
    n = counts.shape[0]
    start = (jnp.cumsum(counts) - counts).astype(jnp.int32)
    ids = jnp.arange(n, dtype=jnp.int32)
    later_busy = jnp.logical_and(ids[None, :] > ids[:, None], counts[None, :] > 0)
    following = jnp.min(jnp.where(later_busy, ids[None, :], n), axis=1)
    following = jnp.where(following >= n, -1, following).astype(jnp.int32)
    first = jnp.min(jnp.where(counts > 0, ids, n)).reshape(1).astype(jnp.int32)
    return start, counts.astype(jnp.int32), following, first


def _rope_tables(positions):
    inv = 1.0 / (ROPE_THETA ** (jnp.arange(0, HEAD_DIM, 2, dtype=F32) / HEAD_DIM))
    ang = positions.astype(F32)[:, None] * inv[None, :]
    cos = jnp.cos(ang)
    sin = jnp.sin(ang)
    cos = jnp.concatenate([cos, cos], axis=-1)
    sin = jnp.concatenate([-sin, sin], axis=-1)
    reps = V_DIM // HEAD_DIM
    return jnp.tile(cos, (1, reps)), jnp.tile(sin, (1, reps)), cos.T, sin.T


def _tile(n, pref):
    return pref if n % pref == 0 else n


def _tiles(seq, n_tok):
    return dict(q_rows=_tile(seq, 512), key_rows=_tile(seq, 256),
                merge_rows=_tile(n_tok, 512), dest_rows=_tile(n_tok, 512),
                move_rows=_tile(n_tok, 256))


def kernel(x, meta_tokens, ln_in_g, ln_in_b, w_in, lambda_q1, lambda_k1, lambda_q2, lambda_k2, subln_g, w_attn_o, conv_w, conv_b, conv_ln_g, conv_ln_b, w_conv_o, b_conv_o, w_out, ln1_g, ln1_b, w_router, router_bias, w_gate_e, w_up_e, w_down_e, w_gate_s, w_up_s, w_down_s, ln2_g, ln2_b):
    bsz, seq, d = x.shape
    t = bsz * seq
    l = 0
    row = lambda a: a.reshape(1, -1).astype(F32)
    x2 = x.reshape(t, d)
    lg, lb = row(ln_in_g), row(ln_in_b)
    w_in_bf = w_in[l].astype(BF16)
    o_v = 2 * QK_WIDTH
    wqt = w_in_bf[:, :QK_WIDTH].T
    wk = w_in_bf[:, QK_WIDTH:o_v]
    wvt = w_in_bf[:, o_v:o_v + V_WIDTH].T
    wr = w_in_bf[:, o_v + V_WIDTH:]

    tables_x = _rope_tables(jnp.arange(N_META, N_META + seq))
    tables_m = _rope_tables(jnp.arange(N_META))

    tiles = _tiles(seq, t)
    tq, tk = tiles["q_rows"], tiles["key_rows"]
    qt, k, vt, u, ga, gc = _inproj(x2, lg, lb, wqt, wk, wvt, wr, tables_x, tq, tk)
    _, km, vmt, um, _, _ = _inproj(meta_tokens.astype(F32), lg, lb, wqt, wk, wvt, wr, tables_m,
                                   N_META, N_META)

    lam = (jnp.exp(jnp.sum(lambda_q1[l].astype(F32) * lambda_k1[l].astype(F32)))
           - jnp.exp(jnp.sum(lambda_q2[l].astype(F32) * lambda_k2[l].astype(F32)))
           + LAM_INIT).reshape(1, 1)
    o_attn = _attention(lam, qt, k, vt, km, vmt.reshape(V_WIDTH, N_META),
                        subln_g[l].reshape(-1, 1).astype(F32), bsz, seq, tq, tk)

    z = _conv(u, um, conv_w[l].astype(F32), row(conv_b[l]), row(conv_ln_g[l]),
              row(conv_ln_b[l]), bsz, seq)

    assert d == SUBLANES * LANES, "the row-tile layout assumes one (8, 128) tile per token row"
    h1, h1_tiles, top_e, gw_t, rank, counts = _merge_route(
        x2, lg, lb, o_attn, z, ga, gc,
        w_attn_o[l].astype(BF16), w_conv_o[l].astype(BF16), row(b_conv_o[l]),
        w_out[l].astype(BF16), row(ln1_g[l]), row(ln1_b[l]),
        w_router[l].T.astype(BF16), router_bias[l].reshape(-1, 1).astype(F32),
        tiles["merge_rows"])

    tables = _expert_tables(counts.reshape(-1).astype(jnp.int32))
    start = tables[0]
    dest = _dest(top_e, rank, start.astype(F32).reshape(-1, 1), tiles["dest_rows"]).reshape(-1)

    xs = _dispatch(dest, h1_tiles, tiles["move_rows"])
    ys = _experts(tables, xs, w_gate_e[l], w_up_e[l], w_down_e[l], t * TOP_K)
    out = _combine(dest, h1, gw_t.T, ys,
                   w_gate_s[l].astype(BF16), w_up_s[l].astype(BF16), w_down_s[l].astype(BF16),
                   row(ln2_g[l]), row(ln2_b[l]), tiles["move_rows"])
    return out.reshape(bsz, seq, d)
```

```python
import functools
import math

import jax
import jax.numpy as jnp
from jax import lax
from jax.experimental import pallas as pl
from jax.experimental.pallas import tpu as pltpu

N_META = 16
HEADS = 4
HEAD_DIM = 64
V_DIM = 2 * HEAD_DIM
QK_WIDTH = HEADS * 2 * HEAD_DIM
V_WIDTH = HEADS * V_DIM
ROPE_THETA = 10000.0
CONV_WIDTH = 31
N_EXPERTS = 256
TOP_K = 8
N_GROUPS = 8
TOPK_GROUPS = 4
GROUP_SIZE = N_EXPERTS // N_GROUPS
ROUTED_SCALE = 2.5
LN_EPS = 1e-5
DEPTH = 1
ALPHA = (2.0 * DEPTH) ** 0.25
LAM_INIT = 0.8 - 0.6 * math.exp(-0.3 * 0)

F32 = jnp.float32
BF16 = jnp.bfloat16
NEG_INF = float("-inf")

VMEM_LIMIT = 56 * 1024 * 1024


def _ln(x, g, b):
    mu = jnp.mean(x, axis=-1, keepdims=True)
    xc = x - mu
    var = jnp.mean(xc * xc, axis=-1, keepdims=True)
    return xc * lax.rsqrt(var + LN_EPS) * g + b


def _dot(a, b):
    return jnp.dot(a, b, preferred_element_type=F32)


def _dot_nt(a, b):
    return lax.dot_general(a, b, (((1,), (1,)), ((), ())), preferred_element_type=F32)


def _silu(x):
    return x * jax.nn.sigmoid(x)


SUBLANES = 8
LANES = 128


def _rows_from_tiles(ref, n_rows, first=0):
    return jnp.concatenate(
        [ref[pl.ds(first * SUBLANES + c, n_rows, stride=SUBLANES), :] for c in range(SUBLANES)],
        axis=1)


def _rows_to_tiles(ref, x):
    for c in range(SUBLANES):
        ref[pl.ds(c, x.shape[0], stride=SUBLANES), :] = x[:, c * LANES:(c + 1) * LANES]


LOG2E = math.log2(math.e)
HALF_HEAD = HEAD_DIM // 2


def _inproj_kernel(x_ref, g_ref, b_ref, wqt_ref, wk_ref, wvt_ref, wr_ref,
                   cos_ref, sin_ref, cost_ref, sint_ref,
                   qt_ref, k_ref, vt_ref, u_ref, ga_ref, gc_ref, *, d_model):
    h = _ln(x_ref[...], g_ref[...], b_ref[...]).astype(BF16)
    tm = x_ref.shape[0]
    c = d_model // 2

    qt = _dot_nt(wqt_ref[...], h)
    pieces = []
    for g in range(QK_WIDTH // HEAD_DIM):
        lo = g * HEAD_DIM
        pieces.append(qt[lo + HALF_HEAD:lo + HEAD_DIM, :])
        pieces.append(qt[lo:lo + HALF_HEAD, :])
    partner = jnp.concatenate(pieces, axis=0)
    reps_t = QK_WIDTH // cost_ref.shape[0]
    qt = qt * jnp.tile(cost_ref[...], (reps_t, 1)) + partner * jnp.tile(sint_ref[...], (reps_t, 1))
    qt_ref[...] = (qt * (HEAD_DIM ** -0.5 * LOG2E)).astype(BF16)

    kk = _dot(h, wk_ref[...])
    reps = QK_WIDTH // cos_ref.shape[1]
    lane = lax.broadcasted_iota(jnp.int32, (tm, QK_WIDTH), 1)
    partner = jnp.where((lane % HEAD_DIM) < HALF_HEAD,
                        pltpu.roll(kk, QK_WIDTH - HALF_HEAD, 1),
                        pltpu.roll(kk, HALF_HEAD, 1))
    kk = kk * jnp.tile(cos_ref[...], (1, reps)) + partner * jnp.tile(sin_ref[...], (1, reps))
    k_ref[...] = kk.astype(BF16)

    vt = _dot_nt(wvt_ref[...], h).astype(BF16)
    chunk = vt_ref.shape[2]
    for j in range(vt_ref.shape[0]):
        vt_ref[j] = vt[:, j * chunk:(j + 1) * chunk]

    def proj(lo, width):
        return _dot(h, wr_ref[:, lo:lo + width])

    u_ref[...] = proj(0, c) * jax.nn.sigmoid(proj(c, c))
    ga_ref[...] = jax.nn.sigmoid(proj(2 * c, d_model)).astype(BF16)
    gc_ref[...] = jax.nn.sigmoid(proj(2 * c + d_model, d_model)).astype(BF16)


def _inproj(x2, g, b, wqt, wk, wvt, wr, tables, tm, chunk):
    t, d = x2.shape
    c = d // 2
    cos, sin, cost, sint = tables
    n_pos_tiles = cos.shape[0] // tm
    row = lambda i: (i, 0)
    col = lambda i: (0, i)
    const = lambda i: (0, 0)
    pos = lambda i: (i % n_pos_tiles, 0)
    post = lambda i: (0, i % n_pos_tiles)
    full = lambda a: pl.BlockSpec(a.shape, const)
    return pl.pallas_call(
        functools.partial(_inproj_kernel, d_model=d),
        grid=(t // tm,),
        in_specs=[
            pl.BlockSpec((tm, d), row), full(g), full(b),
            full(wqt), full(wk), full(wvt), full(wr),
            pl.BlockSpec((tm, cos.shape[1]), pos),
            pl.BlockSpec((tm, cos.shape[1]), pos),
            pl.BlockSpec((cost.shape[0], tm), post),
            pl.BlockSpec((cost.shape[0], tm), post),
        ],
        out_specs=[
            pl.BlockSpec((QK_WIDTH, tm), col),
            pl.BlockSpec((tm, QK_WIDTH), row),
            pl.BlockSpec((tm // chunk, V_WIDTH, chunk), lambda i: (i, 0, 0)),
            pl.BlockSpec((tm, c), row),
            pl.BlockSpec((tm, d), row),
            pl.BlockSpec((tm, d), row),
        ],
        out_shape=[
            jax.ShapeDtypeStruct((QK_WIDTH, t), BF16),
            jax.ShapeDtypeStruct((t, QK_WIDTH), BF16),
            jax.ShapeDtypeStruct((t // chunk, V_WIDTH, chunk), BF16),
            jax.ShapeDtypeStruct((t, c), F32),
            jax.ShapeDtypeStruct((t, d), BF16),
            jax.ShapeDtypeStruct((t, d), BF16),
        ],
        compiler_params=pltpu.CompilerParams(
            dimension_semantics=("arbitrary",), vmem_limit_bytes=VMEM_LIMIT),
        name="inproj",
    )(x2, g, b, wqt, wk, wvt, wr, cos, sin, cost, sint)


def _attn_kernel(lam_ref, qt_ref, k_ref, vt_ref, km_ref, vmt_ref, g_ref, o_ref, acc_sc, *, tq, tk):
    i = pl.program_id(2)
    qt = qt_ref[...]
    feat = lax.broadcasted_iota(jnp.int32, qt.shape, 0)
    zero = jnp.zeros_like(qt)
    q2 = jnp.concatenate([jnp.where(feat < HEAD_DIM, qt, zero),
                          jnp.where(feat >= HEAD_DIM, qt, zero)], axis=1)

    def scores(j):
        r0 = pl.multiple_of(j * tk, tk)
        return _dot(k_ref[pl.ds(r0, tk), :], q2)

    def absorb(m_old, l_old, s, vtc):
        m_new = jnp.maximum(m_old, jnp.max(s, axis=0, keepdims=True))
        alpha = jnp.exp2(m_old - m_new)
        p = jnp.exp2(s - m_new)
        acc_sc[...] = alpha * acc_sc[...] + _dot(vtc, p.astype(BF16))
        return m_new, alpha * l_old + jnp.sum(p, axis=0, keepdims=True)

    s = _dot(km_ref[...], q2)
    m = jnp.max(s, axis=0, keepdims=True)
    p = jnp.exp2(s - m)
    acc_sc[...] = _dot(vmt_ref[...], p.astype(BF16))
    l = jnp.sum(p, axis=0, keepdims=True)

    per_tile = tq // tk
    n_full = i * per_tile

    def body(j, carry):
        m_old, l_old, s_cur = carry
        s_next = scores(j + 1)
        m_new, l_new = absorb(m_old, l_old, s_cur, vt_ref[j])
        return m_new, l_new, s_next

    m, l, s = lax.fori_loop(0, n_full, body, (m, l, scores(0)))

    key = lax.broadcasted_iota(jnp.int32, (tk, 2 * tq), 0)
    qry = lax.broadcasted_iota(jnp.int32, (tk, 2 * tq), 1) % tq
    for d in range(per_tile):
        s_cur = s
        if d + 1 < per_tile:
            s = scores(n_full + d + 1)
        m, l = absorb(m, l, jnp.where(key + d * tk <= qry, s_cur, NEG_INF), vt_ref[n_full + d])

    lam = lam_ref[0, 0]
    o = acc_sc[:, :tq] / l[:, :tq] - lam * (acc_sc[:, tq:] / l[:, tq:])
    o = o * lax.rsqrt(jnp.mean(o * o, axis=0, keepdims=True) + LN_EPS) * g_ref[...]
    o_ref[...] = (o * (1.0 - LAM_INIT)).T.astype(o_ref.dtype)


def _attention(lam, qt, k, vt, km, vmt, subln_col, bsz, seq, tq, tk):
    t = k.shape[0]
    nq = seq // tq
    nk = seq // tk
    return pl.pallas_call(
        functools.partial(_attn_kernel, tq=tq, tk=tk),
        grid=(bsz, HEADS, nq),
        in_specs=[
            pl.BlockSpec(memory_space=pltpu.SMEM),
            pl.BlockSpec((V_DIM, tq), lambda b, h, i: (h, b * nq + i)),
            pl.BlockSpec((seq, V_DIM), lambda b, h, i: (b, h)),
            pl.BlockSpec((nk, V_DIM, tk), lambda b, h, i: (b, h, 0)),
            pl.BlockSpec((N_META, V_DIM), lambda b, h, i: (0, h)),
            pl.BlockSpec((V_DIM, N_META), lambda b, h, i: (h, 0)),
            pl.BlockSpec((V_DIM, 1), lambda b, h, i: (0, 0)),
        ],
        out_specs=pl.BlockSpec((tq, V_DIM), lambda b, h, i: (b * nq + i, h)),
        out_shape=jax.ShapeDtypeStruct((t, V_WIDTH), BF16),
        scratch_shapes=[pltpu.VMEM((V_DIM, 2 * tq), F32)],
        compiler_params=pltpu.CompilerParams(
            dimension_semantics=("arbitrary", "arbitrary", "arbitrary"),
            vmem_limit_bytes=VMEM_LIMIT),
        name="diff_attention",
    )(lam, qt, k, vt, km, vmt, subln_col)


CONV_ROWS = 32
CONV_WINDOW = 64
CONV_PAD = 32


def _conv_kernel(u_ref, um_ref, w_ref, cb_ref, g_ref, b_ref, z_ref, ucat, grp, *, seq):
    c = u_ref.shape[1]
    ucat[0:CONV_PAD - N_META, :] = jnp.zeros((CONV_PAD - N_META, c), F32)
    ucat[CONV_PAD - N_META:CONV_PAD, :] = um_ref[...]
    ucat[CONV_PAD:CONV_PAD + seq, :] = u_ref[...]
    shift = CONV_PAD - (CONV_WIDTH - 1)

    def body(t, carry):
        r0 = pl.multiple_of(t * CONV_ROWS, CONV_ROWS)
        acc = jnp.zeros((CONV_ROWS, c), F32)
        for r in range(SUBLANES):
            taps = [j for j in range(CONV_WIDTH) if (shift + j) % SUBLANES == r]
            if not taps:
                continue
            lo = shift + taps[0]
            rows = taps[-1] - taps[0] + CONV_ROWS
            start = pl.multiple_of(r0 + lo - r, SUBLANES)
            span = rows + (SUBLANES if r else 0)
            grp[r, 0:rows, :] = ucat[pl.ds(start, span), :][r:r + rows, :]
            for j in taps:
                off = shift + j - lo
                acc = acc + grp[r, off:off + CONV_ROWS, :] * w_ref[j:j + 1, :]
        y = _ln(acc + cb_ref[...], g_ref[...], b_ref[...])
        z_ref[pl.ds(r0, CONV_ROWS), :] = _silu(y).astype(z_ref.dtype)
        return carry

    lax.fori_loop(0, seq // CONV_ROWS, body, 0)


def _conv(u, um, conv_w, conv_b, ln_g, ln_b, bsz, seq):
    t, c = u.shape
    const = lambda b: (0, 0)
    return pl.pallas_call(
        functools.partial(_conv_kernel, seq=seq),
        grid=(bsz,),
        in_specs=[
            pl.BlockSpec((seq, c), lambda b: (b, 0)),
            pl.BlockSpec((N_META, c), const),
            pl.BlockSpec((CONV_WIDTH, c), const),
            pl.BlockSpec((1, c), const),
            pl.BlockSpec((1, c), const),
            pl.BlockSpec((1, c), const),
        ],
        out_specs=pl.BlockSpec((seq, c), lambda b: (b, 0)),
        out_shape=jax.ShapeDtypeStruct((t, c), BF16),
        scratch_shapes=[pltpu.VMEM((CONV_PAD + seq, c), F32),
                        pltpu.VMEM((SUBLANES, CONV_WINDOW, c), F32)],
        compiler_params=pltpu.CompilerParams(
            dimension_semantics=("arbitrary",), vmem_limit_bytes=VMEM_LIMIT),
        name="conformer_conv",
    )(u, um, conv_w, conv_b, ln_g, ln_b)


def _merge_route_kernel(x_ref, lg_ref, lb_ref, oa_ref, z_ref, ga_ref, gc_ref,
                        wao_ref, wco_ref, bco_ref, wout_ref, g1_ref, b1_ref,
                        wrt_ref, rb_ref,
                        h1_ref, h1t_ref, te_ref, gw_ref, rk_ref, cnt_ref, carry_sc):
    tm = x_ref.shape[0]

    @pl.when(pl.program_id(0) == 0)
    def _():
        carry_sc[...] = jnp.zeros(carry_sc.shape, F32)

    h0 = _ln(x_ref[...], lg_ref[...], lb_ref[...])
    y_attn = _dot(oa_ref[...], wao_ref[...])
    y_conv = _dot(z_ref[...], wco_ref[...]) + bco_ref[...]
    merged = ga_ref[...].astype(F32) * y_attn + gc_ref[...].astype(F32) * y_conv
    h1 = _ln(ALPHA * h0 + _dot(merged.astype(BF16), wout_ref[...]), g1_ref[...], b1_ref[...])
    h1_ref[...] = h1
    _rows_to_tiles(h1t_ref, h1)
    h1b = h1.astype(BF16)

    scores = jax.nn.sigmoid(_dot_nt(wrt_ref[...], h1b))
    sel = scores + rb_ref[...]

    sub = lax.broadcasted_iota(jnp.int32, (GROUP_SIZE, tm), 0)
    gscore = []
    for g in range(N_GROUPS):
        slab = sel[g * GROUP_SIZE:(g + 1) * GROUP_SIZE, :]
        top1 = jnp.max(slab, axis=0, keepdims=True)
        arg1 = jnp.min(jnp.where(slab == top1, sub, GROUP_SIZE), axis=0, keepdims=True)
        top2 = jnp.max(jnp.where(sub == arg1, NEG_INF, slab), axis=0, keepdims=True)
        gscore.append(top1 + top2)

    eiota = lax.broadcasted_iota(jnp.int32, (N_EXPERTS, tm), 0)
    egroup = eiota // GROUP_SIZE
    allowed = jnp.zeros((N_EXPERTS, tm), F32)
    for _ in range(TOPK_GROUPS):
        best = functools.reduce(jnp.maximum, gscore)
        pick = jnp.full((1, tm), N_GROUPS, jnp.int32)
        for g in reversed(range(N_GROUPS)):
            pick = jnp.where(gscore[g] == best, g, pick)
        gscore = [jnp.where(pick == g, NEG_INF, gscore[g]) for g in range(N_GROUPS)]
        allowed = jnp.where(egroup == pick, 1.0, allowed)

    masked = jnp.where(allowed > 0.5, sel, NEG_INF)

    picks = []
    weights = []
    chosen = jnp.zeros((N_EXPERTS, tm), F32)
    for _ in range(TOP_K):
        best = jnp.max(masked, axis=0, keepdims=True)
        pick = jnp.min(jnp.where(masked == best, eiota, N_EXPERTS), axis=0, keepdims=True)
        hit = eiota == pick
        picks.append(pick)
        weights.append(jnp.sum(jnp.where(hit, scores, 0.0), axis=0, keepdims=True))
        chosen = jnp.where(hit, 1.0, chosen)
        masked = jnp.where(hit, NEG_INF, masked)

    gw = jnp.concatenate(weights, axis=0)
    gw_ref[...] = gw / jnp.sum(gw, axis=0, keepdims=True) * ROUTED_SCALE
    te_ref[...] = jnp.concatenate(picks, axis=0)

    before = (lax.broadcasted_iota(jnp.int32, (tm, tm), 0)
              < lax.broadcasted_iota(jnp.int32, (tm, tm), 1))
    prior = _dot(chosen.astype(BF16), jnp.where(before, 1.0, 0.0).astype(BF16)) + carry_sc[...]
    ranks = [jnp.sum(jnp.where(eiota == p, prior, 0.0), axis=0, keepdims=True) for p in picks]
    rk_ref[...] = jnp.concatenate(ranks, axis=0).astype(jnp.int32)
    carry_sc[...] = carry_sc[...] + jnp.sum(chosen, axis=1, keepdims=True)
    cnt_ref[...] = carry_sc[...]


def _merge_route(x2, lg, lb, oa, z, ga, gc, wao, wco, bco, wout, g1, b1, wrt, rb, tm):
    t, d = x2.shape
    c = d // 2
    row = lambda i: (i, 0)
    col = lambda i: (0, i)
    const = lambda i: (0, 0)
    full = lambda a: pl.BlockSpec(a.shape, const)
    return pl.pallas_call(
        _merge_route_kernel,
        grid=(t // tm,),
        in_specs=[
            pl.BlockSpec((tm, d), row), full(lg), full(lb),
            pl.BlockSpec((tm, V_WIDTH), row), pl.BlockSpec((tm, c), row),
            pl.BlockSpec((tm, d), row), pl.BlockSpec((tm, d), row),
            full(wao), full(wco), full(bco), full(wout), full(g1), full(b1),
            full(wrt), full(rb),
        ],
        out_specs=[
            pl.BlockSpec((tm, d), row),
            pl.BlockSpec((tm * SUBLANES, LANES), row),
            pl.BlockSpec((TOP_K, tm), col),
            pl.BlockSpec((TOP_K, tm), col),
            pl.BlockSpec((TOP_K, tm), col),
            pl.BlockSpec((N_EXPERTS, 1), const),
        ],
        out_shape=[
            jax.ShapeDtypeStruct((t, d), F32),
            jax.ShapeDtypeStruct((t * SUBLANES, LANES), F32),
            jax.ShapeDtypeStruct((TOP_K, t), jnp.int32),
            jax.ShapeDtypeStruct((TOP_K, t), F32),
            jax.ShapeDtypeStruct((TOP_K, t), jnp.int32),
            jax.ShapeDtypeStruct((N_EXPERTS, 1), F32),
        ],
        scratch_shapes=[pltpu.VMEM((N_EXPERTS, 1), F32)],
        compiler_params=pltpu.CompilerParams(
            dimension_semantics=("arbitrary",), vmem_limit_bytes=VMEM_LIMIT),
        name="merge_route",
    )(x2, lg, lb, oa, z, ga, gc, wao, wco, bco, wout, g1, b1, wrt, rb)


def _dest_kernel(te_ref, rk_ref, start_ref, d_ref):
    tm = te_ref.shape[1]
    eiota = lax.broadcasted_iota(jnp.int32, (N_EXPERTS, tm), 0)
    start = start_ref[...]
    rows = [jnp.sum(jnp.where(eiota == te_ref[k:k + 1, :], start, 0.0), axis=0, keepdims=True)
            for k in range(TOP_K)]
    d_ref[...] = (jnp.concatenate(rows, axis=0).astype(jnp.int32) + rk_ref[...]) * SUBLANES


def _dest(top_e, rank, start_col, tm):
    t = top_e.shape[1]
    col = lambda i: (0, i)
    return pl.pallas_call(
        _dest_kernel,
        grid=(t // tm,),
        in_specs=[
            pl.BlockSpec((TOP_K, tm), col),
            pl.BlockSpec((TOP_K, tm), col),
            pl.BlockSpec((N_EXPERTS, 1), lambda i: (0, 0)),
        ],
        out_specs=pl.BlockSpec((TOP_K, tm), col),
        out_shape=jax.ShapeDtypeStruct((TOP_K, t), jnp.int32),
        compiler_params=pltpu.CompilerParams(dimension_semantics=("arbitrary",)),
        name="moe_dest",
    )(top_e, rank, start_col)


def _dispatch_kernel(dest_ref, h_ref, xs_ref, zeros, sem, pad_sem, *, n_tok):
    tm = h_ref.shape[0] // SUBLANES
    base = pl.program_id(0) * tm

    @pl.when(pl.program_id(0) == 0)
    def _():
        zeros[...] = jnp.zeros(zeros.shape, zeros.dtype)
        pad = pltpu.make_async_copy(
            zeros, xs_ref.at[pl.ds(n_tok * TOP_K * SUBLANES, zeros.shape[0])], pad_sem)
        pad.start()
        pad.wait()

    for r in range(tm):
        for k in range(TOP_K):
            slot = pl.multiple_of(dest_ref[k * n_tok + base + r], SUBLANES)
            pltpu.make_async_copy(h_ref.at[pl.ds(r * SUBLANES, SUBLANES)],
                                  xs_ref.at[pl.ds(slot, SUBLANES)], sem).start(priority=k % 2)
    for k in range(TOP_K):
        pltpu.make_async_copy(h_ref, xs_ref.at[pl.ds(0, tm * SUBLANES)], sem).wait()


def _dispatch(dest_flat, h_tiles, tm):
    t = h_tiles.shape[0] // SUBLANES
    return pl.pallas_call(
        functools.partial(_dispatch_kernel, n_tok=t),
        grid_spec=pltpu.PrefetchScalarGridSpec(
            num_scalar_prefetch=1,
            grid=(t // tm,),
            in_specs=[pl.BlockSpec((tm * SUBLANES, LANES), lambda i, dest: (i, 0))],
            out_specs=pl.BlockSpec(memory_space=pl.ANY),
            scratch_shapes=[pltpu.VMEM((EXPERT_CHUNK * SUBLANES, LANES), h_tiles.dtype),
                            pltpu.SemaphoreType.DMA, pltpu.SemaphoreType.DMA],
        ),
        out_shape=jax.ShapeDtypeStruct(((t * TOP_K + EXPERT_CHUNK) * SUBLANES, LANES),
                                       h_tiles.dtype),
        compiler_params=pltpu.CompilerParams(
            dimension_semantics=("arbitrary",), has_side_effects=True),
        name="moe_dispatch",
    )(dest_flat, h_tiles)


EXPERT_CHUNK = 512
CHUNK_PIECES = tuple(EXPERT_CHUNK >> i for i in range(EXPERT_CHUNK.bit_length()))


def _experts_kernel(start_ref, count_ref, next_ref, first_ref,
                    xs_ref, wg_ref, wu_ref, wd_ref, ys_ref,
                    xbuf, ybuf, sem_x, sem_y, wg_bf, wu_bf, wd_bf, state):
    e = pl.program_id(0)
    ch = EXPERT_CHUNK
    count = count_ref[e]
    start = start_ref[e]

    @pl.when(e == 0)
    def _():
        state[0] = 0
        state[1] = 0
        state[2] = 0

    def fetch(row, slot):
        first = pl.multiple_of(row * SUBLANES, SUBLANES)
        return pltpu.make_async_copy(xs_ref.at[pl.ds(first, ch * SUBLANES)], xbuf.at[slot],
                                     sem_x.at[slot])

    def write_back(row, slot, rows, wait_only):
        for piece in CHUNK_PIECES:
            off = (rows // (2 * piece)) * (2 * piece)

            @pl.when((rows & piece) != 0)
            def _():
                src = pl.multiple_of(off * SUBLANES, SUBLANES)
                dst = pl.multiple_of((row + off) * SUBLANES, SUBLANES)
                copy = pltpu.make_async_copy(
                    ybuf.at[slot, pl.ds(src, piece * SUBLANES)],
                    ys_ref.at[pl.ds(dst, piece * SUBLANES)], sem_y.at[slot])
                if wait_only:
                    copy.wait()
                else:
                    copy.start()

    @pl.when(count > 0)
    def _():
        wg_bf[...] = wg_ref[...].astype(BF16)
        wu_bf[...] = wu_ref[...].astype(BF16)
        wd_bf[...] = wd_ref[...].astype(BF16)
        done = state[0]
        n_chunks = (count + ch - 1) // ch

        @pl.when(e == first_ref[0])
        def _():
            fetch(start, done % 2).start()

        def body(j, carry):
            slot = (done + j) % 2
            row = start + j * ch
            more = j + 1 < n_chunks
            following = next_ref[e]

            @pl.when(more)
            def _():
                fetch(row + ch, 1 - slot).start()

            @pl.when(jnp.logical_and(jnp.logical_not(more), following >= 0))
            def _():
                fetch(start_ref[jnp.maximum(following, 0)], 1 - slot).start()

            fetch(row, slot).wait()
            x = _rows_from_tiles(xbuf.at[slot], ch).astype(BF16)
            a = _silu(_dot(x, wg_bf[...])) * _dot(x, wu_bf[...])
            y = _dot(a.astype(BF16), wd_bf[...])

            write_back(0, slot, state[1 + slot], wait_only=True)
            _rows_to_tiles(ybuf.at[slot], y)
            rows = jnp.minimum(count - j * ch, ch)
            write_back(row, slot, rows, wait_only=False)
            state[1 + slot] = rows
            return carry

        lax.fori_loop(0, n_chunks, body, 0)
        state[0] = done + n_chunks

    @pl.when(e == pl.num_programs(0) - 1)
    def _():
        for slot in range(2):
            write_back(0, slot, state[1 + slot], wait_only=True)
            state[1 + slot] = 0


def _experts(tables, xs, wg, wu, wd, n_rows):
    d = SUBLANES * LANES
    n_exp, _, hid = wg.shape
    wmap = lambda e, start, count, following, first: (e, 0, 0)
    return pl.pallas_call(
        _experts_kernel,
        grid_spec=pltpu.PrefetchScalarGridSpec(
            num_scalar_prefetch=len(tables),
            grid=(n_exp,),
            in_specs=[
                pl.BlockSpec(memory_space=pl.ANY),
                pl.BlockSpec((None, d, hid), wmap),
                pl.BlockSpec((None, d, hid), wmap),
                pl.BlockSpec((None, hid, d), wmap),
            ],
            out_specs=pl.BlockSpec(memory_space=pl.ANY),
            scratch_shapes=[
                pltpu.VMEM((2, EXPERT_CHUNK * SUBLANES, LANES), F32),
                pltpu.VMEM((2, EXPERT_CHUNK * SUBLANES, LANES), F32),
                pltpu.SemaphoreType.DMA((2,)),
                pltpu.SemaphoreType.DMA((2,)),
                pltpu.VMEM((d, hid), BF16), pltpu.VMEM((d, hid), BF16),
                pltpu.VMEM((hid, d), BF16),
                pltpu.SMEM((3,), jnp.int32),
            ],
        ),
        out_shape=jax.ShapeDtypeStruct((n_rows * SUBLANES, LANES), xs.dtype),
        compiler_params=pltpu.CompilerParams(
            dimension_semantics=("arbitrary",), vmem_limit_bytes=VMEM_LIMIT,
            has_side_effects=True),
        name="moe_experts",
    )(*tables, xs, wg, wu, wd)


def _combine_kernel(dest_ref, h_ref, gw_ref, ys_ref, wgs_ref, wus_ref, wds_ref, g2_ref, b2_ref,
                    o_ref, buf, sem, *, n_tok):
    tm = h_ref.shape[0] // 2
    step = pl.program_id(0)

    def gather(tile, into):
        base = tile * tm
        for r in range(tm):
            for k in range(TOP_K):
                slot = pl.multiple_of(dest_ref[k * n_tok + base + r], SUBLANES)
                pltpu.make_async_copy(ys_ref.at[pl.ds(slot, SUBLANES)],
                                      buf.at[into, k, pl.ds(r * SUBLANES, SUBLANES)],
                                      sem.at[into]).start(priority=k % 2)

    def finish(half):
        rows = slice(half * tm, (half + 1) * tm)
        h = h_ref[rows, :]
        hb = h.astype(BF16)
        shared = _dot((_silu(_dot(hb, wgs_ref[...])) * _dot(hb, wus_ref[...])).astype(BF16),
                      wds_ref[...])
        acc = ALPHA * h + shared
        for k in range(TOP_K):
            pltpu.make_async_copy(ys_ref.at[pl.ds(0, tm * SUBLANES)], buf.at[half, k],
                                  sem.at[half]).wait()
        gw = gw_ref[rows, :]
        for k in range(TOP_K):
            acc = acc + _rows_from_tiles(buf.at[half, k], tm) * gw[:, k:k + 1]
        o_ref[rows, :] = _ln(acc, g2_ref[...], b2_ref[...])

    @pl.when(step == 0)
    def _():
        gather(0, 0)

    gather(2 * step + 1, 1)
    finish(0)

    @pl.when(step + 1 < pl.num_programs(0))
    def _():
        gather(2 * step + 2, 0)

    finish(1)


def _combine(dest_flat, h1, gw, ys, wgs, wus, wds, g2, b2, tm):
    t, d = h1.shape
    row = lambda i, dest: (i, 0)
    const = lambda i, dest: (0, 0)
    full = lambda a: pl.BlockSpec(a.shape, const)
    return pl.pallas_call(
        functools.partial(_combine_kernel, n_tok=t),
        grid_spec=pltpu.PrefetchScalarGridSpec(
            num_scalar_prefetch=1,
            grid=(t // (2 * tm),),
            in_specs=[
                pl.BlockSpec((2 * tm, d), row),
                pl.BlockSpec((2 * tm, TOP_K), row),
                pl.BlockSpec(memory_space=pl.ANY),
                full(wgs), full(wus), full(wds), full(g2), full(b2),
            ],
            out_specs=pl.BlockSpec((2 * tm, d), row),
            scratch_shapes=[pltpu.VMEM((2, TOP_K, tm * SUBLANES, LANES), ys.dtype),
                            pltpu.SemaphoreType.DMA((2,))],
        ),
        out_shape=jax.ShapeDtypeStruct((t, d), F32),
        compiler_params=pltpu.CompilerParams(
            dimension_semantics=("arbitrary",), vmem_limit_bytes=VMEM_LIMIT),
        name="moe_combine",
    )(dest_flat, h1, gw, ys, wgs, wus, wds, g2, b2)


def _expert_tables(counts):
    n = counts.shape[0]
    start = (jnp.cumsum(counts) - counts).astype(jnp.int32)
    ids = jnp.arange(n, dtype=jnp.int32)
    later_busy = jnp.logical_and(ids[None, :] > ids[:, None], counts[None, :] > 0)
    following = jnp.min(jnp.where(later_busy, ids[None, :], n), axis=1)
    following = jnp.where(following >= n, -1, following).astype(jnp.int32)
    first = jnp.min(jnp.where(counts > 0, ids, n)).reshape(1).astype(jnp.int32)
    return start, counts.astype(jnp.int32), following, first


def _rope_tables(positions):
    inv = 1.0 / (ROPE_THETA ** (jnp.arange(0, HEAD_DIM, 2, dtype=F32) / HEAD_DIM))
    ang = positions.astype(F32)[:, None] * inv[None, :]
    cos = jnp.cos(ang)
    sin = jnp.sin(ang)
    cos = jnp.concatenate([cos, cos], axis=-1)
    sin = jnp.concatenate([-sin, sin], axis=-1)
    reps = V_DIM // HEAD_DIM
    return jnp.tile(cos, (1, reps)), jnp.tile(sin, (1, reps)), cos.T, sin.T


def _tile(n, pref):
    return pref if n % pref == 0 else n


def _tiles(seq, n_tok):
    return dict(q_rows=_tile(seq, 512), key_rows=_tile(seq, 256),
                merge_rows=_tile(n_tok, 512), dest_rows=_tile(n_tok, 512),
                move_rows=_tile(n_tok, 256))


def kernel(x, meta_tokens, ln_in_g, ln_in_b, w_in, lambda_q1, lambda_k1, lambda_q2, lambda_k2, subln_g, w_attn_o, conv_w, conv_b, conv_ln_g, conv_ln_b, w_conv_o, b_conv_o, w_out, ln1_g, ln1_b, w_router, router_bias, w_gate_e, w_up_e, w_down_e, w_gate_s, w_up_s, w_down_s, ln2_g, ln2_b):
    bsz, seq, d = x.shape
    t = bsz * seq
    l = 0
    row = lambda a: a.reshape(1, -1).astype(F32)
    x2 = x.reshape(t, d)
    lg, lb = row(ln_in_g), row(ln_in_b)
    w_in_bf = w_in[l].astype(BF16)
    o_v = 2 * QK_WIDTH
    wqt = w_in_bf[:, :QK_WIDTH].T
    wk = w_in_bf[:, QK_WIDTH:o_v]
    wvt = w_in_bf[:, o_v:o_v + V_WIDTH].T
    wr = w_in_bf[:, o_v + V_WIDTH:]

    tables_x = _rope_tables(jnp.arange(N_META, N_META + seq))
    tables_m = _rope_tables(jnp.arange(N_META))

    tiles = _tiles(seq, t)
    tq, tk = tiles["q_rows"], tiles["key_rows"]
    qt, k, vt, u, ga, gc = _inproj(x2, lg, lb, wqt, wk, wvt, wr, tables_x, tq, tk)
    _, km, vmt, um, _, _ = _inproj(meta_tokens.astype(F32), lg, lb, wqt, wk, wvt, wr, tables_m,
                                   N_META, N_META)

    lam = (jnp.exp(jnp.sum(lambda_q1[l].astype(F32) * lambda_k1[l].astype(F32)))
           - jnp.exp(jnp.sum(lambda_q2[l].astype(F32) * lambda_k2[l].astype(F32)))
           + LAM_INIT).reshape(1, 1)
    o_attn = _attention(lam, qt, k, vt, km, vmt.reshape(V_WIDTH, N_META),
                        subln_g[l].reshape(-1, 1).astype(F32), bsz, seq, tq, tk)

    z = _conv(u, um, conv_w[l].astype(F32), row(conv_b[l]), row(conv_ln_g[l]),
              row(conv_ln_b[l]), bsz, seq)

    assert d == SUBLANES * LANES, "the row-tile layout assumes one (8, 128) tile per token row"
    h1, h1_tiles, top_e, gw_t, rank, counts = _merge_route(
        x2, lg, lb, o_attn, z, ga, gc,
        w_attn_o[l].astype(BF16), w_conv_o[l].astype(BF16), row(b_conv_o[l]),
        w_out[l].astype(BF16), row(ln1_g[l]), row(ln1_b[l]),
        w_router[l].T.astype(BF16), router_bias[l].reshape(-1, 1).astype(F32),
        tiles["merge_rows"])

    tables = _expert_tables(counts.reshape(-1).astype(jnp.int32))
    start = tables[0]
    dest = _dest(top_e, rank, start.astype(F32).reshape(-1, 1), tiles["dest_rows"]).reshape(-1)

    xs = _dispatch(dest, h1_tiles, tiles["move_rows"])
    ys = _experts(tables, xs, w_gate_e[l], w_up_e[l], w_down_e[l], t * TOP_K)
    out = _combine(dest, h1, gw_t.T, ys,
                   w_gate_s[l].astype(BF16), w_up_s[l].astype(BF16), w_down_s[l].astype(BF16),
                   row(ln2_g[l]), row(ln2_b[l]), tiles["move_rows"])
    return out.reshape(bsz, seq, d)
```
